```python
import jax, jax.numpy as jnp
from jax import lax
import numpy as np

D_MODEL = 2048
BATCH = 4
SEQ = 2048
DEPTH = 2
DEC_BATCH = 128
DEC_SEQ = 8
PAST_LEN = 16384
PAGE_SIZE = 128

HG_HEADS = 8
HG_DK = 128
HG_DV = 128
HG_WIDTH = HG_HEADS * HG_DK
HG_CHUNK = 64
HG_F_MIN = 1e-30
RW_HEADS = 16
RW_N = 64
RW_WIDTH = RW_HEADS * RW_N
RW_DECAY_LORA = 64
RW_AAA_LORA = 64
RW_GATE_LORA = 128
RW_SHIFT_WIDTH = 3 * RW_WIDTH + RW_DECAY_LORA + RW_AAA_LORA + RW_GATE_LORA
RW_GN_EPS = 64e-5
SC_WIDTH = 1024
SC_KSIZE = 3
N_BRANCH = 3
D_FF = 4 * D_MODEL
NORM_EPS = 1e-6
OFF_HG = 0
OFF_RW = OFF_HG + 4 * HG_WIDTH
OFF_SC = OFF_RW + RW_SHIFT_WIDTH
OFF_GATE = OFF_SC + 3 * SC_WIDTH
IN_TOTAL = OFF_GATE + N_BRANCH * D_MODEL

kernel_name = "hybrid_hgrn2_rwkv7_shortconv_decode_step"


def _rmsnorm(x, g):
    xf = x.astype(jnp.float32)
    return xf * lax.rsqrt(jnp.mean(xf * xf, -1, keepdims=True) + NORM_EPS) * g.astype(jnp.float32)


def _hgrn2_chunked(q, k, v, logf, S0):
    Bn, T, H, dk = q.shape
    C = min(HG_CHUNK, T)
    nc = -(-T // C)
    pad = nc * C - T

    def prep(a):
        a = jnp.pad(a, ((0, 0), (0, pad), (0, 0), (0, 0)))
        return a.reshape(Bn, nc, C, H, a.shape[-1]).transpose(1, 0, 3, 2, 4)

    qc, kc, vc, gc = prep(q), prep(k), prep(v), prep(logf)
    mask = jnp.tril(jnp.ones((C, C), dtype=jnp.float32))[:, :, None]

    def step(S, inp):
        q_, k_, v_, g_ = inp
        b = jnp.cumsum(g_, axis=2)
        inter = jnp.einsum('bhtd,bhde->bhte', q_ * jnp.exp(b), S)
        diff = b[:, :, :, None, :] - b[:, :, None, :, :]
        decay = jnp.exp(jnp.minimum(diff, 0.0)) * mask
        A = jnp.einsum('bhtd,bhsd,bhtsd->bhts', q_, k_, decay)
        o = inter + jnp.einsum('bhts,bhse->bhte', A, v_)
        b_last = b[:, :, -1]
        S = jnp.exp(b_last)[..., None] * S + jnp.einsum(
            'bhsd,bhse->bhde', k_ * jnp.exp(b_last[:, :, None] - b), v_)
        return S, o

    S, o = lax.scan(step, S0, (qc, kc, vc, gc))
    o = o.transpose(1, 0, 3, 2, 4).reshape(Bn, nc * C, H, v.shape[-1])[:, :T]
    return o, S


def _rwkv7_scan(r, w, k, v, kk, a, S0):
    def step(S, inp):
        r_, w_, k_, v_, kk_, a_ = inp
        sa = jnp.einsum('bhij,bhj->bhi', S, -kk_)
        S = (S * w_[:, :, None, :] + sa[..., None] * (kk_ * a_)[:, :, None, :]
             + v_[..., None] * k_[:, :, None, :])
        return S, jnp.einsum('bhij,bhj->bhi', S, r_)

    xs = tuple(t.transpose(1, 0, 2, 3) for t in (r, w, k, v, kk, a))
    S, o = lax.scan(step, S0, xs)
    return o.transpose(1, 0, 2, 3), S


def trunk_layer(x, hg_S0, rw_S0, rw_prev, sc_buf, lb,
                g_pre_mix, g_post_mix, g_pre_mlp, g_post_mlp, w_in,
                hg_norm, w_pa,
                rw_mu, rw_w0, rw_w2, rw_a0, rw_a2, rw_g2, rw_k_k, rw_k_a, rw_r_k,
                rw_ln_w, rw_ln_b, w_pb,
                sc_conv_w, w_pc, w_o, w_ff1, w_ff2):
    f32 = jnp.float32
    Bn, T, _ = x.shape
    h = _rmsnorm(x, g_pre_mix).astype(x.dtype)
    proj = jnp.einsum('btd,de->bte', h, w_in).astype(f32)

    hq, hf, hi, hog = jnp.split(proj[..., OFF_HG:OFF_RW], 4, axis=-1)
    q = jax.nn.silu(hq).reshape(Bn, T, HG_HEADS, HG_DK)
    lbh = lb.astype(f32).reshape(HG_HEADS, HG_DK)
    f = lbh + (1.0 - lbh) * jax.nn.sigmoid(hf.reshape(Bn, T, HG_HEADS, HG_DK))
    logf = jnp.log(jnp.maximum(f, HG_F_MIN))
    k_hg = 1.0 - f
    v_hg = hi.reshape(Bn, T, HG_HEADS, HG_DV)
    o_hg, hg_S = _hgrn2_chunked(q, k_hg, v_hg, logf, hg_S0.astype(f32))
    o_hg = o_hg * lax.rsqrt(jnp.mean(o_hg * o_hg, -1, keepdims=True) + NORM_EPS)
    o_hg = o_hg.reshape(Bn, T, HG_WIDTH) * hg_norm * jax.nn.silu(hog)
    y_a = o_hg @ w_pa

    rw = proj[..., OFF_RW:OFF_SC]
    prev = jnp.concatenate([rw_prev.astype(f32)[:, None], rw[:, :-1]], axis=1)
    xs = rw + (prev - rw) * rw_mu
    cuts = [RW_WIDTH, 2 * RW_WIDTH, 3 * RW_WIDTH, 3 * RW_WIDTH + RW_DECAY_LORA,
            3 * RW_WIDTH + RW_DECAY_LORA + RW_AAA_LORA]
    r, kr, vr, wd, ad, gd = jnp.split(xs, cuts, axis=-1)
    w_log = -jax.nn.softplus(-(rw_w0 + jnp.tanh(wd) @ rw_w2)) - 0.5
    decay = jnp.exp(-jnp.exp(w_log))
    a = jax.nn.sigmoid(rw_a0 + ad @ rw_a2)
    g = jax.nn.sigmoid(gd) @ rw_g2
    hs = lambda t: t.reshape(Bn, T, RW_HEADS, RW_N)
    kk = hs(kr * rw_k_k)
    kk = kk / jnp.maximum(jnp.sqrt(jnp.sum(kk * kk, -1, keepdims=True)), 1e-12)
    k = kr * (1.0 + (a - 1.0) * rw_k_a)
    r_, w_, k_, v_, a_ = hs(r), hs(decay), hs(k), hs(vr), hs(a)
    o_rw, rw_S = _rwkv7_scan(r_, w_, k_, v_, kk, a_, rw_S0.astype(f32))
    mu = jnp.mean(o_rw, -1, keepdims=True)
    var = jnp.mean(jnp.square(o_rw - mu), -1, keepdims=True)
    o_rw = ((o_rw - mu) * lax.rsqrt(var + RW_GN_EPS)).reshape(Bn, T, RW_WIDTH) * rw_ln_w + rw_ln_b
    bonus = jnp.sum(r_ * k_ * rw_r_k, -1, keepdims=True) * v_
    o_rw = (o_rw + bonus.reshape(Bn, T, RW_WIDTH)) * g
    y_b = o_rw @ w_pb
    rw_last = rw[:, -1]

    sb, sc, sh = jnp.split(proj[..., OFF_SC:OFF_GATE], 3, axis=-1)
    u = sc * sh
    full = jnp.concatenate([sc_buf.astype(f32), u], axis=1)
    conv = sum(sc_conv_w[j] * full[:, j:j + T] for j in range(SC_KSIZE))
    y_c = (sb * conv) @ w_pc
    sc_new = full[:, T:]

    gates = jax.nn.sigmoid(proj[..., OFF_GATE:].reshape(Bn, T, N_BRANCH, D_MODEL))
    m = gates[:, :, 0] * y_a + gates[:, :, 1] * y_b + gates[:, :, 2] * y_c
    mix = m @ w_o
    x1 = x.astype(f32) + _rmsnorm(mix, g_post_mix)
    h2 = _rmsnorm(x1, g_pre_mlp).astype(x.dtype)
    ff = jnp.square(jax.nn.relu(h2 @ w_ff1)) @ w_ff2
    y = (x1 + _rmsnorm(ff, g_post_mlp)).astype(x.dtype)
    return y, hg_S, rw_S, rw_last, sc_new


def setup_inputs(seed: int = 0) -> dict:
    key = jax.random.key(seed)
    ks = iter(jax.random.split(key, 40))
    nrm = lambda shape, s: jax.random.normal(next(ks), shape, jnp.float32) * s
    L = DEPTH
    return {
        "x_prompt": nrm((BATCH, SEQ, D_MODEL), 1.0),
        "x_sample": nrm((DEC_BATCH, DEC_SEQ, D_MODEL), 1.0),
        "state_hgrn": nrm((L, DEC_BATCH, HG_HEADS, HG_DK, HG_DV), 0.3),
        "state_rwkv": nrm((L, DEC_BATCH, RW_HEADS, RW_N, RW_N), 0.1),
        "state_rwkv_shift": nrm((L, DEC_BATCH, RW_SHIFT_WIDTH), 1.0),
        "state_conv": nrm((L, DEC_BATCH, SC_KSIZE - 1, SC_WIDTH), 0.5),
        "norm_pre_mix": 1.0 + nrm((L, D_MODEL), 0.02),
        "norm_post_mix": 1.0 + nrm((L, D_MODEL), 0.02),
        "norm_pre_mlp": 1.0 + nrm((L, D_MODEL), 0.02),
        "norm_post_mlp": 1.0 + nrm((L, D_MODEL), 0.02),
        "w_in": nrm((L, D_MODEL, IN_TOTAL), D_MODEL ** -0.5),
        "hg_lb_logits": nrm((L, HG_WIDTH), 0.5),
        "hg_norm": 1.0 + nrm((L, HG_WIDTH), 0.02),
        "w_pa": nrm((L, HG_WIDTH, D_MODEL), HG_WIDTH ** -0.5),
        "rw_mu": jax.random.uniform(next(ks), (L, RW_SHIFT_WIDTH), jnp.float32),
        "rw_w0": -2.0 + nrm((L, RW_WIDTH), 0.5),
        "rw_w2": nrm((L, RW_DECAY_LORA, RW_WIDTH), 0.3 * RW_DECAY_LORA ** -0.5),
        "rw_a0": nrm((L, RW_WIDTH), 0.1),
        "rw_a2": nrm((L, RW_AAA_LORA, RW_WIDTH), 0.5 * RW_AAA_LORA ** -0.5),
        "rw_g2": nrm((L, RW_GATE_LORA, RW_WIDTH), RW_GATE_LORA ** -0.5),
        "rw_k_k": 0.85 + nrm((L, RW_WIDTH), 0.02),
        "rw_k_a": 1.0 + nrm((L, RW_WIDTH), 0.02),
        "rw_r_k": nrm((L, RW_HEADS, RW_N), 0.1),
        "rw_ln_w": 1.0 + nrm((L, RW_WIDTH), 0.02),
        "rw_ln_b": nrm((L, RW_WIDTH), 0.01),
        "w_pb": nrm((L, RW_WIDTH, D_MODEL), RW_WIDTH ** -0.5),
        "sc_conv_w": nrm((L, SC_KSIZE, SC_WIDTH), 0.5),
        "w_pc": nrm((L, SC_WIDTH, D_MODEL), SC_WIDTH ** -0.5),
        "w_o": nrm((L, D_MODEL, D_MODEL), D_MODEL ** -0.5),
        "w_ff1": nrm((L, D_MODEL, D_FF), D_MODEL ** -0.5),
        "w_ff2": nrm((L, D_FF, D_MODEL), D_FF ** -0.5),
    }


def reference(x_prompt, x_sample, state_hgrn, state_rwkv, state_rwkv_shift, state_conv,
              norm_pre_mix, norm_post_mix, norm_pre_mlp, norm_post_mlp, w_in,
              hg_lb_logits, hg_norm, w_pa,
              rw_mu, rw_w0, rw_w2, rw_a0, rw_a2, rw_g2, rw_k_k, rw_k_a, rw_r_k,
              rw_ln_w, rw_ln_b, w_pb,
              sc_conv_w, w_pc, w_o, w_ff1, w_ff2):
    f32 = jnp.float32
    P = jax.nn.softmax(hg_lb_logits.astype(f32), axis=0)
    lb_all = jnp.clip(jnp.cumsum(P, axis=0) - P[0:1], 0.0, 1.0)
    Bp = x_prompt.shape[0]
    yp, ys = x_prompt, x_sample
    p_hg, p_rw, p_sh, p_cv = [], [], [], []
    s_hg, s_rw, s_sh, s_cv = [], [], [], []
    for l in range(DEPTH):
        lw = (norm_pre_mix[l], norm_post_mix[l], norm_pre_mlp[l], norm_post_mlp[l], w_in[l],
              hg_norm[l], w_pa[l],
              rw_mu[l], rw_w0[l], rw_w2[l], rw_a0[l], rw_a2[l], rw_g2[l], rw_k_k[l], rw_k_a[l],
              rw_r_k[l], rw_ln_w[l], rw_ln_b[l], w_pb[l],
              sc_conv_w[l], w_pc[l], w_o[l], w_ff1[l], w_ff2[l])
        yp, a_, b_, c_, d_ = trunk_layer(
            yp,
            jnp.zeros((Bp, HG_HEADS, HG_DK, HG_DV), f32),
            jnp.zeros((Bp, RW_HEADS, RW_N, RW_N), f32),
            jnp.zeros((Bp, RW_SHIFT_WIDTH), f32),
            jnp.zeros((Bp, SC_KSIZE - 1, SC_WIDTH), f32),
            lb_all[l], *lw)
        p_hg.append(a_); p_rw.append(b_); p_sh.append(c_); p_cv.append(d_)
        ys, a_, b_, c_, d_ = trunk_layer(
            ys, state_hgrn[l], state_rwkv[l], state_rwkv_shift[l], state_conv[l],
            lb_all[l], *lw)
        s_hg.append(a_); s_rw.append(b_); s_sh.append(c_); s_cv.append(d_)
    return (yp, ys,
            jnp.stack(p_hg), jnp.stack(p_rw), jnp.stack(p_sh), jnp.stack(p_cv),
            jnp.stack(s_hg), jnp.stack(s_rw), jnp.stack(s_sh), jnp.stack(s_cv))
```

```python
import functools

import jax
import jax.numpy as jnp
from jax import lax
from jax.experimental import pallas as pl
from jax.experimental.pallas import tpu as pltpu

HG_DK = 128
RW_N = 64
HG_F_MIN = 1e-30
RW_GN_EPS = 64e-5
NORM_EPS = 1e-6
SC_KSIZE = 3

V7X_LANES = 128
V7X_SUBLANES = 8
V7X_VMEM_LIMIT_BYTES = 48 * 1024 * 1024
HG_CHUNK = 16
RW_PAIR = 2

f32 = jnp.float32
bf16 = jnp.bfloat16
HIGHEST = lax.Precision.HIGHEST


def _tile(n, pref, align=V7X_SUBLANES):
    if n <= pref:
        return n
    for d in range(pref, 0, -1):
        if n % d == 0 and d % align == 0:
            return d
    return n


def _seq_block(bsz, tlen):
    if tlen >= 256:
        return 1, _tile(tlen, 256)
    return _tile(bsz, max(1, 128 // tlen), align=1), tlen


def _params(*sem):
    return pltpu.CompilerParams(dimension_semantics=sem, vmem_limit_bytes=V7X_VMEM_LIMIT_BYTES)


def _sigmoid(x):
    return 1.0 / (1.0 + jnp.exp(-x))


def _dot(a, b):
    return jnp.dot(a, b, preferred_element_type=f32)


def _group_ones(width, group):
    r = lax.broadcasted_iota(jnp.int32, (width, width), 0) // group
    c = lax.broadcasted_iota(jnp.int32, (width, width), 1) // group
    return jnp.where(r == c, 1.0, 0.0).astype(bf16)


def _group_sum(z, ones_bd):
    outs = []
    for c in range(z.shape[-1] // V7X_LANES):
        zc = z[:, c * V7X_LANES:(c + 1) * V7X_LANES]
        hi = zc.astype(bf16)
        lo = (zc - hi.astype(f32)).astype(bf16)
        outs.append(_dot(hi, ones_bd) + _dot(lo, ones_bd))
    return outs[0] if len(outs) == 1 else jnp.concatenate(outs, axis=-1)


def _lb_kernel(logit_ref, lb_ref):
    z = logit_ref[...]
    depth = z.shape[0]
    m = jnp.max(z, axis=0, keepdims=True)
    e = jnp.exp(z - m)
    p = e / jnp.sum(e, axis=0, keepdims=True)
    acc = jnp.zeros_like(p[0:1])
    for l in range(depth):
        acc = acc + p[l:l + 1]
        lb_ref[l:l + 1, :] = jnp.clip(acc - p[0:1], 0.0, 1.0)


def _lower_bounds(logits):
    return pl.pallas_call(
        _lb_kernel, out_shape=jax.ShapeDtypeStruct(logits.shape, f32), name="hg_lower_bounds",
    )(logits)


def _rms(x, g):
    return x * lax.rsqrt(jnp.mean(x * x, axis=-1, keepdims=True) + NORM_EPS) * g


def _rmsnorm_kernel(x_ref, g_ref, o_ref):
    o_ref[...] = _rms(x_ref[...], g_ref[...]).astype(o_ref.dtype)


def _rmsnorm_cast(x, g):
    n, d = x.shape
    tm = _tile(n, 512)
    return pl.pallas_call(
        _rmsnorm_kernel,
        grid=(n // tm,),
        in_specs=[pl.BlockSpec((tm, d), lambda i: (i, 0)), pl.BlockSpec((1, d), lambda i: (0, 0))],
        out_specs=pl.BlockSpec((tm, d), lambda i: (i, 0)),
        out_shape=jax.ShapeDtypeStruct((n, d), bf16),
        compiler_params=_params("parallel"),
        name="rmsnorm_cast",
    )(x, g)


def _matmul_kernel(a_ref, w_ref, o_ref):
    o_ref[...] = _dot(a_ref[...], w_ref[...]).astype(o_ref.dtype)


def _matmul(a, w, name):
    n, k = a.shape
    wd = w.shape[1]
    tm = _tile(n, 1024)
    tn = _tile(wd, 1024, align=V7X_LANES)
    return pl.pallas_call(
        _matmul_kernel,
        grid=(n // tm, wd // tn),
        in_specs=[pl.BlockSpec((tm, k), lambda i, j: (i, 0)), pl.BlockSpec((k, tn), lambda i, j: (0, j))],
        out_specs=pl.BlockSpec((tm, tn), lambda i, j: (i, j)),
        out_shape=jax.ShapeDtypeStruct((n, wd), f32),
        compiler_params=_params("parallel", "arbitrary"),
        name=name,
    )(a, w)


def _hgrn_kernel(*refs, nb, tb, chunk, has_state):
    if has_state:
        q_ref, f_ref, i_ref, og_ref, lb_ref, gn_ref, s0_ref, o_ref, s_ref, st_scr = refs
    else:
        q_ref, f_ref, i_ref, og_ref, lb_ref, gn_ref, o_ref, s_ref, st_scr = refs
        s0_ref = None
    ti = pl.program_id(2)
    rows = nb * tb
    dk = HG_DK

    @pl.when(ti == 0)
    def _():
        for b in range(nb):
            st_scr[b] = s0_ref[b, 0].T if has_state else jnp.zeros((dk, dk), f32)

    lb = lb_ref[...]
    hq = q_ref[...].reshape(rows, dk)
    qq = hq * _sigmoid(hq)
    f = lb + (1.0 - lb) * _sigmoid(f_ref[...].reshape(rows, dk))
    logf = jnp.log(jnp.maximum(f, HG_F_MIN))
    kk = 1.0 - f
    vv = i_ref[...].reshape(rows, dk)
    hog = og_ref[...].reshape(rows, dk)

    r_i = lax.broadcasted_iota(jnp.int32, (rows, rows), 0)
    c_i = lax.broadcasted_iota(jnp.int32, (rows, rows), 1)
    tri = jnp.where((r_i // chunk == c_i // chunk) & (c_i <= r_i), 1.0, 0.0)
    b_all = jnp.dot(tri, logf, precision=HIGHEST, preferred_element_type=f32)

    ones = jnp.ones((dk, dk), bf16)
    s_idx = lax.broadcasted_iota(jnp.int32, (chunk, dk), 0)
    sel_r = lax.broadcasted_iota(jnp.int32, (chunk, chunk * chunk), 0)
    sel_c = lax.broadcasted_iota(jnp.int32, (chunk, chunk * chunk), 1) // chunk
    sel_t = jnp.where(sel_r == sel_c, 1.0, 0.0).astype(bf16)

    outs = []
    for b in range(nb):
        st = st_scr[b]
        for g in range(tb // chunk):
            r0 = b * tb + g * chunk
            bc = b_all[r0:r0 + chunk]
            bl = bc[chunk - 1:chunk]
            q_c, k_c, v_c = qq[r0:r0 + chunk], kk[r0:r0 + chunk], vv[r0:r0 + chunk]
            inter = lax.dot_general((q_c * jnp.exp(bc)).astype(bf16), st.astype(bf16),
                                    (((1,), (1,)), ((), ())), preferred_element_type=f32)
            xs = []
            for t in range(chunk):
                dec = jnp.exp(jnp.minimum(bc[t:t + 1] - bc, 0.0))
                xs.append(jnp.where(s_idx <= t, k_c * dec * q_c[t:t + 1], 0.0))
            x = jnp.concatenate(xs, axis=0).astype(bf16)
            a_b = _dot(x, ones)
            p = (a_b * jnp.concatenate([v_c] * chunk, axis=0)).astype(bf16)
            o_c = inter + _dot(sel_t, p)
            kt = (k_c * jnp.exp(bl - bc)).astype(bf16)
            st = jnp.exp(bl) * st + lax.dot_general(v_c.astype(bf16), kt, (((0,), (0,)), ((), ())),
                                                    preferred_element_type=f32)
            o_c = o_c * lax.rsqrt(jnp.mean(o_c * o_c, axis=-1, keepdims=True) + NORM_EPS)
            og_c = hog[r0:r0 + chunk]
            outs.append(o_c * gn_ref[...] * (og_c * _sigmoid(og_c)))
        st_scr[b] = st
    o_ref[...] = jnp.concatenate(outs, axis=0).reshape(nb, tb, dk).astype(o_ref.dtype)

    @pl.when(ti == pl.num_programs(2) - 1)
    def _():
        for b in range(nb):
            s_ref[b, 0] = st_scr[b].T


def _hgrn(p_hg, lb, gn, s0, bsz, tlen):
    width = p_hg.shape[1] // 4
    heads = width // HG_DK
    chunk = min(HG_CHUNK, tlen)
    nb, tb = _seq_block(bsz, tlen)
    assert tb % chunk == 0
    p3 = p_hg.reshape(bsz, tlen, 4 * width)
    seg = lambda s: pl.BlockSpec((nb, tb, HG_DK), lambda bi, h, ti, s=s: (bi, ti, s * heads + h))
    vec = pl.BlockSpec((1, HG_DK), lambda bi, h, ti: (0, h))
    st_spec = pl.BlockSpec((nb, 1, HG_DK, HG_DK), lambda bi, h, ti: (bi, h, 0, 0))
    has_state = s0 is not None
    in_specs = [seg(0), seg(1), seg(2), seg(3), vec, vec] + ([st_spec] if has_state else [])
    args = [p3, p3, p3, p3, lb, gn] + ([s0] if has_state else [])
    o, s_new = pl.pallas_call(
        functools.partial(_hgrn_kernel, nb=nb, tb=tb, chunk=chunk, has_state=has_state),
        grid=(bsz // nb, heads, tlen // tb),
        in_specs=in_specs,
        out_specs=[pl.BlockSpec((nb, tb, HG_DK), lambda bi, h, ti: (bi, ti, h)), st_spec],
        out_shape=[jax.ShapeDtypeStruct((bsz, tlen, width), bf16),
                   jax.ShapeDtypeStruct((bsz, heads, HG_DK, HG_DK), f32)],
        scratch_shapes=[pltpu.VMEM((nb, HG_DK, HG_DK), f32)],
        compiler_params=_params("parallel", "parallel", "arbitrary"),
        name="hgrn2_scan",
    )(*args)
    return o.reshape(bsz * tlen, width), s_new


def _shift_rows(x, first, tb, by):
    rows = x.shape[0]
    t_idx = lax.broadcasted_iota(jnp.int32, (rows, 1), 0) % tb
    out = pltpu.roll(x, by, 0) if tb > by else x
    for j in range(by):
        out = jnp.where(t_idx == j, first[j], out)
    return out


def _bcast_rows(v3, nb, tb):
    w = v3.shape[-1]
    return jnp.broadcast_to(v3, (nb, tb, w)).reshape(nb * tb, w)


def _rw_prep_kernel(*refs, nb, tb, width, lora, has_state):
    if has_state:
        (p_ref, sh_ref, mu_ref, w0_ref, w2_ref, a0_ref, a2_ref, g2_ref, kk_ref, ka_ref,
         r_ref, w_ref, k_ref, v_ref, kn_ref, b_ref, g_ref, last_ref) = refs
    else:
        (p_ref, mu_ref, w0_ref, w2_ref, a0_ref, a2_ref, g2_ref, kk_ref, ka_ref,
         r_ref, w_ref, k_ref, v_ref, kn_ref, b_ref, g_ref, last_ref) = refs
        sh_ref = None
    ti = pl.program_id(1)
    rows = nb * tb
    tot = p_ref.shape[-1]
    x = p_ref[...].reshape(rows, tot)

    @pl.when(ti == 0)
    def _():
        last_ref[...] = sh_ref[...] if has_state else jnp.zeros((nb, 1, tot), f32)

    prev = _shift_rows(x, [_bcast_rows(last_ref[...], nb, tb)], tb, 1)
    last_ref[...] = p_ref[:, tb - 1:tb, :]
    xs = x + (prev - x) * mu_ref[...]
    dl, al, gl = lora
    c = 3 * width
    r, kr, vr = xs[:, :width], xs[:, width:2 * width], xs[:, 2 * width:c]
    wd, ad, gd = xs[:, c:c + dl], xs[:, c + dl:c + dl + al], xs[:, c + dl + al:c + dl + al + gl]
    z = -(w0_ref[...] + _dot(jnp.tanh(wd).astype(bf16), w2_ref[...]))
    softplus = jnp.maximum(z, 0.0) + jnp.log(1.0 + jnp.exp(-jnp.abs(z)))
    w_log = -softplus - 0.5
    decay = jnp.exp(-jnp.exp(w_log))
    a = _sigmoid(a0_ref[...] + _dot(ad.astype(bf16), a2_ref[...]))
    g = _dot(_sigmoid(gd).astype(bf16), g2_ref[...])
    kk = kr * kk_ref[...]
    ones_bd = _group_ones(V7X_LANES, RW_N)
    nrm = jnp.sqrt(_group_sum(kk * kk, ones_bd))
    kk = kk / jnp.maximum(nrm, 1e-12)
    k = kr * (1.0 + (a - 1.0) * ka_ref[...])
    shp = (nb, tb, width)
    r_ref[...] = r.reshape(shp)
    w_ref[...] = decay.reshape(shp)
    k_ref[...] = k.reshape(shp)
    v_ref[...] = vr.reshape(shp)
    kn_ref[...] = (-kk).reshape(shp)
    b_ref[...] = (kk * a).reshape(shp)
    g_ref[...] = g.reshape(shp)


def _rw_prep(p_rw, shift, wts, bsz, tlen):
    mu, w0, w2, a0, a2, g2, k_k, k_a = wts
    tot = p_rw.shape[1]
    width = w0.shape[1]
    lora = (w2.shape[0], a2.shape[0], g2.shape[0])
    nb, tb = _seq_block(bsz, tlen)
    has_state = shift is not None
    p3 = p_rw.reshape(bsz, tlen, tot)
    blk = lambda w: pl.BlockSpec((nb, tb, w), lambda bi, ti: (bi, ti, 0))
    full = lambda arr: pl.BlockSpec(arr.shape, lambda bi, ti: (0,) * arr.ndim)
    last_spec = pl.BlockSpec((nb, 1, tot), lambda bi, ti: (bi, 0, 0))
    small = [mu, w0, w2, a0, a2, g2, k_k, k_a]
    in_specs = [blk(tot)] + ([last_spec] if has_state else []) + [full(s) for s in small]
    args = [p3] + ([shift.reshape(bsz, 1, tot)] if has_state else []) + small
    seq = jax.ShapeDtypeStruct((bsz, tlen, width), f32)
    outs = pl.pallas_call(
        functools.partial(_rw_prep_kernel, nb=nb, tb=tb, width=width, lora=lora, has_state=has_state),
        grid=(bsz // nb, tlen // tb),
        in_specs=in_specs,
        out_specs=[blk(width)] * 7 + [last_spec],
        out_shape=[seq] * 7 + [jax.ShapeDtypeStruct((bsz, 1, tot), f32)],
        compiler_params=_params("parallel", "arbitrary"),
        name="rwkv7_prep",
    )(*args)
    return outs[:7], outs[7].reshape(bsz, tot)


def _rw_scan_kernel(*refs, nb, tb, pairs, has_state):
    if has_state:
        (r_ref, w_ref, k_ref, v_ref, kn_ref, b_ref, s0_ref, o_ref, s_ref,
         st_scr, lhs_a, lhs_v, lhs_o) = refs
    else:
        (r_ref, w_ref, k_ref, v_ref, kn_ref, b_ref, o_ref, s_ref,
         st_scr, lhs_a, lhs_v, lhs_o) = refs
        s0_ref = None
    ti = pl.program_id(1)
    n = RW_N
    lanes = RW_PAIR * n
    combos = [(b, p) for b in range(nb) for p in range(pairs)]

    @pl.when(ti == 0)
    def _():
        for c, (b, p) in enumerate(combos):
            if has_state:
                st_scr[c] = jnp.concatenate([s0_ref[b, RW_PAIR * p + h] for h in range(RW_PAIR)], axis=-1)
            else:
                st_scr[c] = jnp.zeros((n, lanes), f32)

    ones_bd = _group_ones(lanes, n)
    diag = (lax.broadcasted_iota(jnp.int32, (n, lanes), 0)
            == lax.broadcasted_iota(jnp.int32, (n, lanes), 1) % n)

    sub = V7X_SUBLANES

    def token_group(grp, carry):
        base = pl.multiple_of(grp * sub, sub)

        def row(ref, b, p, j):
            tile = ref[b, pl.ds(base, sub), pl.ds(p * lanes, lanes)]
            return jnp.broadcast_to(tile[j:j + 1], (n, lanes))

        o_rows = [[] for _ in combos]
        for j in range(sub):
            for c, (b, p) in enumerate(combos):
                lhs_a[c * n:(c + 1) * n, :] = (st_scr[c] * row(kn_ref, b, p, j)).astype(bf16)
                lhs_v[c * n:(c + 1) * n, :] = jnp.where(diag, row(v_ref, b, p, j), 0.0).astype(bf16)
            sa_all = _dot(lhs_a[...], ones_bd)
            vb_all = _dot(lhs_v[...], ones_bd)
            for c, (b, p) in enumerate(combos):
                s_new = (st_scr[c] * row(w_ref, b, p, j)
                         + sa_all[c * n:(c + 1) * n] * row(b_ref, b, p, j)
                         + vb_all[c * n:(c + 1) * n] * row(k_ref, b, p, j))
                st_scr[c] = s_new
                lhs_o[c * n:(c + 1) * n, :] = (s_new * row(r_ref, b, p, j)).astype(bf16)
            o_all = _dot(lhs_o[...], ones_bd)
            for c, (b, p) in enumerate(combos):
                o_rows[c].append(jnp.sum(jnp.where(diag, o_all[c * n:(c + 1) * n], 0.0),
                                         axis=0, keepdims=True))
        for c, (b, p) in enumerate(combos):
            o_ref[b, pl.ds(base, sub), pl.ds(p * lanes, lanes)] = jnp.concatenate(o_rows[c], axis=0)
        return carry

    lax.fori_loop(0, tb // sub, token_group, 0)

    @pl.when(ti == pl.num_programs(1) - 1)
    def _():
        for c, (b, p) in enumerate(combos):
            s = st_scr[c]
            for h in range(RW_PAIR):
                s_ref[b, RW_PAIR * p + h] = s[:, h * n:(h + 1) * n]


def _rw_scan(seqs, s0, bsz, tlen):
    r, w, k, v, kn, bb = seqs
    width = r.shape[-1]
    heads = width // RW_N
    pairs = heads // RW_PAIR
    nb = _tile(bsz, 4, align=1)
    tb = _tile(tlen, 64)
    has_state = s0 is not None
    blk = pl.BlockSpec((nb, tb, width), lambda bi, ti: (bi, ti, 0))
    st_spec = pl.BlockSpec((nb, heads, RW_N, RW_N), lambda bi, ti: (bi, 0, 0, 0))
    m = nb * pairs * RW_N
    o, s_new = pl.pallas_call(
        functools.partial(_rw_scan_kernel, nb=nb, tb=tb, pairs=pairs, has_state=has_state),
        grid=(bsz // nb, tlen // tb),
        in_specs=[blk] * 6 + ([st_spec] if has_state else []),
        out_specs=[blk, st_spec],
        out_shape=[jax.ShapeDtypeStruct((bsz, tlen, width), f32),
                   jax.ShapeDtypeStruct((bsz, heads, RW_N, RW_N), f32)],
        scratch_shapes=[pltpu.VMEM((nb * pairs, RW_N, RW_PAIR * RW_N), f32),
                        pltpu.VMEM((m, RW_PAIR * RW_N), bf16),
                        pltpu.VMEM((m, RW_PAIR * RW_N), bf16),
                        pltpu.VMEM((m, RW_PAIR * RW_N), bf16)],
        compiler_params=_params("parallel", "arbitrary"),
        name="rwkv7_scan",
    )(*([r, w, k, v, kn, bb] + ([s0] if has_state else [])))
    return o, s_new


def _rw_post_kernel(o_ref, r_ref, k_ref, v_ref, g_ref, lw_ref, lbias_ref, rk_ref, y_ref):
    ones_bd = _group_ones(V7X_LANES, RW_N)
    o = o_ref[...]
    mu = _group_sum(o, ones_bd) * (1.0 / RW_N)
    d = o - mu
    var = _group_sum(d * d, ones_bd) * (1.0 / RW_N)
    on = d * lax.rsqrt(var + RW_GN_EPS) * lw_ref[...] + lbias_ref[...]
    bonus = _group_sum(r_ref[...] * k_ref[...] * rk_ref[...], ones_bd) * v_ref[...]
    y_ref[...] = ((on + bonus) * g_ref[...]).astype(y_ref.dtype)


def _rw_post(o, r, k, v, g, ln_w, ln_b, r_k):
    n, width = o.shape
    tm = _tile(n, 256)
    blk = pl.BlockSpec((tm, width), lambda i: (i, 0))
    vec = pl.BlockSpec((1, width), lambda i: (0, 0))
    return pl.pallas_call(
        _rw_post_kernel,
        grid=(n // tm,),
        in_specs=[blk] * 5 + [vec] * 3,
        out_specs=blk,
        out_shape=jax.ShapeDtypeStruct((n, width), bf16),
        compiler_params=_params("parallel"),
        name="rwkv7_post",
    )(o, r, k, v, g, ln_w, ln_b, r_k)


def _conv_kernel(*refs, nb, tb, has_state):
    if has_state:
        sb_ref, sc_ref, sh_ref, st_ref, cw_ref, y_ref, new_ref = refs
    else:
        sb_ref, sc_ref, sh_ref, cw_ref, y_ref, new_ref = refs
        st_ref = None
    ti = pl.program_id(2)
    rows = nb * tb
    w = sb_ref.shape[-1]
    hist = SC_KSIZE - 1

    @pl.when(ti == 0)
    def _():
        new_ref[...] = st_ref[...] if has_state else jnp.zeros((nb, hist, w), f32)

    u = (sc_ref[...] * sh_ref[...]).reshape(rows, w)
    carry = new_ref[...]
    conv = cw_ref[hist:hist + 1, :] * u
    for back in range(1, hist + 1):
        first = [_bcast_rows(carry[:, hist - back + j:hist - back + j + 1, :], nb, tb) for j in range(back)]
        conv = conv + cw_ref[hist - back:hist - back + 1, :] * _shift_rows(u, first, tb, back)
    y_ref[...] = (sb_ref[...].reshape(rows, w) * conv).reshape(nb, tb, w).astype(y_ref.dtype)
    new_ref[...] = u.reshape(nb, tb, w)[:, tb - hist:tb, :]


def _conv(p_sc, state, conv_w, bsz, tlen):
    width = conv_w.shape[1]
    cw = min(width, 256)
    ncol = width // cw
    nb, tb = _seq_block(bsz, tlen)
    assert tb >= SC_KSIZE - 1
    has_state = state is not None
    p3 = p_sc.reshape(bsz, tlen, 3 * width)
    seg = lambda s: pl.BlockSpec((nb, tb, cw), lambda bi, cj, ti, s=s: (bi, ti, s * ncol + cj))
    st_spec = pl.BlockSpec((nb, SC_KSIZE - 1, cw), lambda bi, cj, ti: (bi, 0, cj))
    in_specs = ([seg(0), seg(1), seg(2)] + ([st_spec] if has_state else [])
                + [pl.BlockSpec((SC_KSIZE, cw), lambda bi, cj, ti: (0, cj))])
    args = [p3, p3, p3] + ([state] if has_state else []) + [conv_w]
    y, new = pl.pallas_call(
        functools.partial(_conv_kernel, nb=nb, tb=tb, has_state=has_state),
        grid=(bsz // nb, ncol, tlen // tb),
        in_specs=in_specs,
        out_specs=[pl.BlockSpec((nb, tb, cw), lambda bi, cj, ti: (bi, ti, cj)), st_spec],
        out_shape=[jax.ShapeDtypeStruct((bsz, tlen, width), bf16),
                   jax.ShapeDtypeStruct((bsz, SC_KSIZE - 1, width), f32)],
        compiler_params=_params("parallel", "parallel", "arbitrary"),
        name="short_conv",
    )(*args)
    return y.reshape(bsz * tlen, width), new


def _merge_kernel(a_ref, b_ref, c_ref, ga_ref, gb_ref, gc_ref, pa_ref, pb_ref, pc_ref, m_ref):
    m = (_sigmoid(ga_ref[...]) * _dot(a_ref[...], pa_ref[...])
         + _sigmoid(gb_ref[...]) * _dot(b_ref[...], pb_ref[...])
         + _sigmoid(gc_ref[...]) * _dot(c_ref[...], pc_ref[...]))
    m_ref[...] = m.astype(m_ref.dtype)


def _merge(ya, yb, yc, p_gate, w_pa, w_pb, w_pc):
    n = ya.shape[0]
    d = w_pa.shape[1]
    tm = _tile(n, 512)
    tn = _tile(d, 512)
    nj = d // tn
    act = lambda arr: pl.BlockSpec((tm, arr.shape[1]), lambda i, j: (i, 0))
    gate = lambda s: pl.BlockSpec((tm, tn), lambda i, j, s=s: (i, s * nj + j))
    wt = lambda arr: pl.BlockSpec((arr.shape[0], tn), lambda i, j: (0, j))
    return pl.pallas_call(
        _merge_kernel,
        grid=(n // tm, nj),
        in_specs=[act(ya), act(yb), act(yc), gate(0), gate(1), gate(2), wt(w_pa), wt(w_pb), wt(w_pc)],
        out_specs=pl.BlockSpec((tm, tn), lambda i, j: (i, j)),
        out_shape=jax.ShapeDtypeStruct((n, d), bf16),
        compiler_params=_params("parallel", "arbitrary"),
        name="gated_merge",
    )(ya, yb, yc, p_gate, p_gate, p_gate, w_pa, w_pb, w_pc)


def _oproj_kernel(m_ref, wo_ref, x_ref, g1_ref, g2_ref, x1_ref, h2_ref):
    mix = _dot(m_ref[...], wo_ref[...])
    x1 = x_ref[...] + _rms(mix, g1_ref[...])
    x1_ref[...] = x1
    h2_ref[...] = _rms(x1, g2_ref[...]).astype(h2_ref.dtype)


def _oproj(m, w_o, x, g_post_mix, g_pre_mlp):
    n, d = x.shape
    tm = _tile(n, 256)
    blk = pl.BlockSpec((tm, d), lambda i: (i, 0))
    vec = pl.BlockSpec((1, d), lambda i: (0, 0))
    return pl.pallas_call(
        _oproj_kernel,
        grid=(n // tm,),
        in_specs=[blk, pl.BlockSpec((d, d), lambda i: (0, 0)), blk, vec, vec],
        out_specs=[blk, blk],
        out_shape=[jax.ShapeDtypeStruct((n, d), f32), jax.ShapeDtypeStruct((n, d), bf16)],
        compiler_params=_params("parallel"),
        name="out_proj_norms",
    )(m, w_o, x, g_post_mix, g_pre_mlp)


def _mlp_kernel(h_ref, w1_ref, w2_ref, x1_ref, g_ref, y_ref, acc_ref):
    kf = pl.program_id(1)

    @pl.when(kf == 0)
    def _():
        acc_ref[...] = jnp.zeros_like(acc_ref)

    t = jnp.maximum(_dot(h_ref[...], w1_ref[...]), 0.0)
    acc_ref[...] += _dot((t * t).astype(bf16), w2_ref[...])

    @pl.when(kf == pl.num_programs(1) - 1)
    def _():
        y_ref[...] = x1_ref[...] + _rms(acc_ref[...], g_ref[...])


def _mlp(h2, w1, w2, x1, g_post_mlp):
    n, d = x1.shape
    dff = w1.shape[1]
    tm = _tile(n, 512)
    fc = _tile(dff, 512)
    blk = pl.BlockSpec((tm, d), lambda i, kf: (i, 0))
    return pl.pallas_call(
        _mlp_kernel,
        grid=(n // tm, dff // fc),
        in_specs=[blk, pl.BlockSpec((d, fc), lambda i, kf: (0, kf)), pl.BlockSpec((fc, d), lambda i, kf: (kf, 0)),
                  blk, pl.BlockSpec((1, d), lambda i, kf: (0, 0))],
        out_specs=blk,
        out_shape=jax.ShapeDtypeStruct((n, d), f32),
        scratch_shapes=[pltpu.VMEM((tm, d), f32)],
        compiler_params=_params("parallel", "arbitrary"),
        name="relu2_mlp",
    )(h2, w1, w2, x1, g_post_mlp)


def _layer(x, bsz, tlen, states, lw):
    hg_s0, rw_s0, rw_shift, sc_state = states
    h = _rmsnorm_cast(x, lw["g_pre_mix"])
    p_hg = _matmul(h, lw["w_in_hg"], "in_proj_hgrn")
    p_rw = _matmul(h, lw["w_in_rw"], "in_proj_rwkv")
    p_sc = _matmul(h, lw["w_in_sc"], "in_proj_conv")
    p_gate = _matmul(h, lw["w_in_gate"], "in_proj_gate")

    ya, hg_new = _hgrn(p_hg, lw["lb"], lw["hg_norm"], hg_s0, bsz, tlen)

    seqs, shift_new = _rw_prep(p_rw, rw_shift, lw["rw_prep"], bsz, tlen)
    r, w, k, v, kn, bb, g = seqs
    o_rw, rw_new = _rw_scan((r, w, k, v, kn, bb), rw_s0, bsz, tlen)
    flat = lambda t: t.reshape(bsz * tlen, t.shape[-1])
    yb = _rw_post(flat(o_rw), flat(r), flat(k), flat(v), flat(g), lw["rw_ln_w"], lw["rw_ln_b"], lw["rw_r_k"])

    yc, sc_new = _conv(p_sc, sc_state, lw["sc_conv_w"], bsz, tlen)

    m = _merge(ya, yb, yc, p_gate, lw["w_pa"], lw["w_pb"], lw["w_pc"])
    x1, h2 = _oproj(m, lw["w_o"], x, lw["g_post_mix"], lw["g_pre_mlp"])
    y = _mlp(h2, lw["w_ff1"], lw["w_ff2"], x1, lw["g_post_mlp"])
    return y, (hg_new, rw_new, shift_new, sc_new)


def kernel(x_prompt, x_sample, state_hgrn, state_rwkv, state_rwkv_shift, state_conv, norm_pre_mix, norm_post_mix, norm_pre_mlp, norm_post_mlp, w_in, hg_lb_logits, hg_norm, w_pa, rw_mu, rw_w0, rw_w2, rw_a0, rw_a2, rw_g2, rw_k_k, rw_k_a, rw_r_k, rw_ln_w, rw_ln_b, w_pb, sc_conv_w, w_pc, w_o, w_ff1, w_ff2):
    depth = w_in.shape[0]
    d_model = x_prompt.shape[-1]
    hg_width = hg_norm.shape[1]
    rw_shift_width = rw_mu.shape[1]
    sc_width = sc_conv_w.shape[2]
    off_rw = 4 * hg_width
    off_sc = off_rw + rw_shift_width
    off_gate = off_sc + 3 * sc_width
    assert w_in.shape[2] == off_gate + 3 * d_model

    lb_all = _lower_bounds(hg_lb_logits.astype(f32))
    row = lambda a, l: a[l].reshape(1, -1).astype(f32)
    cast = lambda a: a.astype(bf16)

    bp, tp, _ = x_prompt.shape
    bs, ts, _ = x_sample.shape
    yp = x_prompt.reshape(bp * tp, d_model).astype(f32)
    ys = x_sample.reshape(bs * ts, d_model).astype(f32)
    new_p, new_s = [], []
    for l in range(depth):
        w_in_l = w_in[l]
        lw = {
            "g_pre_mix": row(norm_pre_mix, l), "g_post_mix": row(norm_post_mix, l),
            "g_pre_mlp": row(norm_pre_mlp, l), "g_post_mlp": row(norm_post_mlp, l),
            "w_in_hg": cast(w_in_l[:, :off_rw]), "w_in_rw": cast(w_in_l[:, off_rw:off_sc]),
            "w_in_sc": cast(w_in_l[:, off_sc:off_gate]), "w_in_gate": cast(w_in_l[:, off_gate:]),
            "lb": lb_all[l:l + 1], "hg_norm": row(hg_norm, l),
            "w_pa": cast(w_pa[l]), "w_pb": cast(w_pb[l]), "w_pc": cast(w_pc[l]), "w_o": cast(w_o[l]),
            "rw_prep": (row(rw_mu, l), row(rw_w0, l), cast(rw_w2[l]), row(rw_a0, l), cast(rw_a2[l]),
                        cast(rw_g2[l]), row(rw_k_k, l), row(rw_k_a, l)),
            "rw_ln_w": row(rw_ln_w, l), "rw_ln_b": row(rw_ln_b, l), "rw_r_k": row(rw_r_k, l),
            "sc_conv_w": sc_conv_w[l].astype(f32),
            "w_ff1": cast(w_ff1[l]), "w_ff2": cast(w_ff2[l]),
        }
        yp, st = _layer(yp, bp, tp, (None, None, None, None), lw)
        new_p.append(st)
        ys, st = _layer(ys, bs, ts, (state_hgrn[l].astype(f32), state_rwkv[l].astype(f32),
                                     state_rwkv_shift[l].astype(f32), state_conv[l].astype(f32)), lw)
        new_s.append(st)
    stack = lambda sts, i: jnp.stack([s[i] for s in sts])
    return (yp.reshape(bp, tp, d_model), ys.reshape(bs, ts, d_model),
            stack(new_p, 0), stack(new_p, 1), stack(new_p, 2), stack(new_p, 3),
            stack(new_s, 0), stack(new_s, 1), stack(new_s, 2), stack(new_s, 3))
```

```python
import functools

import jax
import jax.numpy as jnp
from jax import lax
from jax.experimental import pallas as pl
from jax.experimental.pallas import tpu as pltpu

HG_DK = 128
RW_N = 64
HG_F_MIN = 1e-30
RW_GN_EPS = 64e-5
NORM_EPS = 1e-6
SC_KSIZE = 3

V7X_LANES = 128
V7X_SUBLANES = 8
V7X_VMEM_LIMIT_BYTES = 48 * 1024 * 1024
HG_CHUNK = 16
RW_PAIR = 2

f32 = jnp.float32
bf16 = jnp.bfloat16
HIGHEST = lax.Precision.HIGHEST


def _tile(n, pref, align=V7X_SUBLANES):
    if n <= pref:
        return n
    for d in range(pref, 0, -1):
        if n % d == 0 and d % align == 0:
            return d
    return n


def _seq_block(bsz, tlen):
    if tlen >= 256:
        return 1, _tile(tlen, 256)
    return _tile(bsz, max(1, 128 // tlen), align=1), tlen


def _params(*sem):
    return pltpu.CompilerParams(dimension_semantics=sem, vmem_limit_bytes=V7X_VMEM_LIMIT_BYTES)


def _sigmoid(x):
    return 1.0 / (1.0 + jnp.exp(-x))


def _dot(a, b):
    return jnp.dot(a, b, preferred_element_type=f32)


def _group_ones(width, group):
    r = lax.broadcasted_iota(jnp.int32, (width, width), 0) // group
    c = lax.broadcasted_iota(jnp.int32, (width, width), 1) // group
    return jnp.where(r == c, 1.0, 0.0).astype(bf16)


def _group_sum(z, ones_bd):
    outs = []
    for c in range(z.shape[-1] // V7X_LANES):
        zc = z[:, c * V7X_LANES:(c + 1) * V7X_LANES]
        hi = zc.astype(bf16)
        lo = (zc - hi.astype(f32)).astype(bf16)
        outs.append(_dot(hi, ones_bd) + _dot(lo, ones_bd))
    return outs[0] if len(outs) == 1 else jnp.concatenate(outs, axis=-1)


def _lb_kernel(logit_ref, lb_ref):
    z = logit_ref[...]
    depth = z.shape[0]
    m = jnp.max(z, axis=0, keepdims=True)
    e = jnp.exp(z - m)
    p = e / jnp.sum(e, axis=0, keepdims=True)
    acc = jnp.zeros_like(p[0:1])
    for l in range(depth):
        acc = acc + p[l:l + 1]
        lb_ref[l:l + 1, :] = jnp.clip(acc - p[0:1], 0.0, 1.0)


def _lower_bounds(logits):
    return pl.pallas_call(
        _lb_kernel, out_shape=jax.ShapeDtypeStruct(logits.shape, f32), name="hg_lower_bounds",
    )(logits)


def _rms(x, g):
    return x * lax.rsqrt(jnp.mean(x * x, axis=-1, keepdims=True) + NORM_EPS) * g


def _rmsnorm_kernel(x_ref, g_ref, o_ref):
    o_ref[...] = _rms(x_ref[...], g_ref[...]).astype(o_ref.dtype)


def _rmsnorm_cast(x, g):
    n, d = x.shape
    tm = _tile(n, 512)
    return pl.pallas_call(
        _rmsnorm_kernel,
        grid=(n // tm,),
        in_specs=[pl.BlockSpec((tm, d), lambda i: (i, 0)), pl.BlockSpec((1, d), lambda i: (0, 0))],
        out_specs=pl.BlockSpec((tm, d), lambda i: (i, 0)),
        out_shape=jax.ShapeDtypeStruct((n, d), bf16),
        compiler_params=_params("parallel"),
        name="rmsnorm_cast",
    )(x, g)


def _matmul_kernel(a_ref, w_ref, o_ref):
    o_ref[...] = _dot(a_ref[...], w_ref[...]).astype(o_ref.dtype)


def _matmul(a, w, name):
    n, k = a.shape
    wd = w.shape[1]
    tm = _tile(n, 1024)
    tn = _tile(wd, 1024, align=V7X_LANES)
    return pl.pallas_call(
        _matmul_kernel,
        grid=(n // tm, wd // tn),
        in_specs=[pl.BlockSpec((tm, k), lambda i, j: (i, 0)), pl.BlockSpec((k, tn), lambda i, j: (0, j))],
        out_specs=pl.BlockSpec((tm, tn), lambda i, j: (i, j)),
        out_shape=jax.ShapeDtypeStruct((n, wd), f32),
        compiler_params=_params("parallel", "arbitrary"),
        name=name,
    )(a, w)


def _hgrn_kernel(*refs, nb, tb, chunk, has_state):
    if has_state:
        q_ref, f_ref, i_ref, og_ref, lb_ref, gn_ref, rsel_ref, s0_ref, o_ref, s_ref, st_scr = refs
    else:
        q_ref, f_ref, i_ref, og_ref, lb_ref, gn_ref, rsel_ref, o_ref, s_ref, st_scr = refs
        s0_ref = None
    ti = pl.program_id(2)
    rows = nb * tb
    dk = HG_DK
    nchunk = rows // chunk

    @pl.when(ti == 0)
    def _():
        for b in range(nb):
            st_scr[b] = s0_ref[b, 0].T if has_state else jnp.zeros((dk, dk), f32)

    lb = lb_ref[...]
    hq = q_ref[...].reshape(rows, dk)
    qq = hq * _sigmoid(hq)
    f = lb + (1.0 - lb) * _sigmoid(f_ref[...].reshape(rows, dk))
    logf = jnp.log(jnp.maximum(f, HG_F_MIN))
    kk = 1.0 - f
    vv = i_ref[...].reshape(rows, dk)
    hog = og_ref[...].reshape(rows, dk)

    t_idx = lax.broadcasted_iota(jnp.int32, (rows, dk), 0) % chunk
    bcum = logf
    step = 1
    while step < chunk:
        bcum = bcum + jnp.where(t_idx >= step, pltpu.roll(bcum, step, 0), 0.0)
        step *= 2

    def chunk_row(a, s):
        a3 = a.reshape(nchunk, chunk, dk)
        return jnp.broadcast_to(a3[:, s:s + 1, :], (nchunk, chunk, dk)).reshape(rows, dk)

    b_last = chunk_row(bcum, chunk - 1)
    q_dec = (qq * jnp.exp(bcum)).astype(bf16)
    k_dec = (kk * jnp.exp(b_last - bcum)).astype(bf16)

    zs = []
    for s in range(chunk):
        dec = jnp.exp(jnp.minimum(bcum - chunk_row(bcum, s), 0.0))
        zs.append(jnp.where(t_idx >= s, qq * chunk_row(kk, s) * dec, 0.0).astype(bf16))
    x2 = jnp.concatenate(zs, axis=-1)
    a_rep = _dot(x2, rsel_ref[...])
    same_chunk = (lax.broadcasted_iota(jnp.int32, (rows, rows), 0) // chunk
                  == lax.broadcasted_iota(jnp.int32, (rows, rows), 1) // chunk)
    o_intra = _dot(jnp.where(same_chunk, a_rep, 0.0).astype(bf16), vv.astype(bf16))

    v_t = vv.T
    col_chunk = lax.broadcasted_iota(jnp.int32, (dk, rows), 1) // chunk
    lhs = jnp.concatenate([jnp.where(col_chunk == c, v_t, 0.0).astype(bf16) for c in range(nchunk)], axis=0)
    u_all = _dot(lhs, k_dec)

    outs = []
    for b in range(nb):
        st = st_scr[b]
        for g in range(tb // chunk):
            c = b * (tb // chunk) + g
            r0 = c * chunk
            inter = lax.dot_general(q_dec[r0:r0 + chunk], st.astype(bf16),
                                    (((1,), (1,)), ((), ())), preferred_element_type=f32)
            outs.append(inter + o_intra[r0:r0 + chunk])
            st = jnp.exp(bcum[r0 + chunk - 1:r0 + chunk]) * st + u_all[c * dk:(c + 1) * dk]
        st_scr[b] = st
    o = jnp.concatenate(outs, axis=0)
    o = o * lax.rsqrt(jnp.mean(o * o, axis=-1, keepdims=True) + NORM_EPS)
    o_ref[...] = (o * gn_ref[...] * (hog * _sigmoid(hog))).reshape(nb, tb, dk).astype(o_ref.dtype)

    @pl.when(ti == pl.num_programs(2) - 1)
    def _():
        for b in range(nb):
            s_ref[b, 0] = st_scr[b].T


def _hgrn(p_hg, lb, gn, s0, bsz, tlen):
    width = p_hg.shape[1] // 4
    heads = width // HG_DK
    chunk = min(HG_CHUNK, tlen)
    nb, tb = _seq_block(bsz, tlen)
    assert tb % chunk == 0
    p3 = p_hg.reshape(bsz, tlen, 4 * width)
    seg = lambda s: pl.BlockSpec((nb, tb, HG_DK), lambda bi, h, ti, s=s: (bi, ti, s * heads + h))
    vec = pl.BlockSpec((1, HG_DK), lambda bi, h, ti: (0, h))
    st_spec = pl.BlockSpec((nb, 1, HG_DK, HG_DK), lambda bi, h, ti: (bi, h, 0, 0))
    has_state = s0 is not None
    rows = nb * tb
    rsel = (jnp.arange(chunk * HG_DK)[:, None] // HG_DK == jnp.arange(rows)[None, :] % chunk).astype(bf16)
    rsel_spec = pl.BlockSpec((chunk * HG_DK, rows), lambda bi, h, ti: (0, 0))
    in_specs = [seg(0), seg(1), seg(2), seg(3), vec, vec, rsel_spec] + ([st_spec] if has_state else [])
    args = [p3, p3, p3, p3, lb, gn, rsel] + ([s0] if has_state else [])
    o, s_new = pl.pallas_call(
        functools.partial(_hgrn_kernel, nb=nb, tb=tb, chunk=chunk, has_state=has_state),
        grid=(bsz // nb, heads, tlen // tb),
        in_specs=in_specs,
        out_specs=[pl.BlockSpec((nb, tb, HG_DK), lambda bi, h, ti: (bi, ti, h)), st_spec],
        out_shape=[jax.ShapeDtypeStruct((bsz, tlen, width), bf16),
                   jax.ShapeDtypeStruct((bsz, heads, HG_DK, HG_DK), f32)],
        scratch_shapes=[pltpu.VMEM((nb, HG_DK, HG_DK), f32)],
        compiler_params=_params("parallel", "parallel", "arbitrary"),
        name="hgrn2_scan",
    )(*args)
    return o.reshape(bsz * tlen, width), s_new


def _shift_rows(x, first, tb, by):
    rows = x.shape[0]
    t_idx = lax.broadcasted_iota(jnp.int32, (rows, 1), 0) % tb
    out = pltpu.roll(x, by, 0) if tb > by else x
    for j in range(by):
        out = jnp.where(t_idx == j, first[j], out)
    return out


def _bcast_rows(v3, nb, tb):
    w = v3.shape[-1]
    return jnp.broadcast_to(v3, (nb, tb, w)).reshape(nb * tb, w)


def _rw_prep_kernel(*refs, nb, tb, width, lora, has_state):
    if has_state:
        (p_ref, sh_ref, mu_ref, w0_ref, w2_ref, a0_ref, a2_ref, g2_ref, kk_ref, ka_ref,
         r_ref, w_ref, k_ref, v_ref, kn_ref, b_ref, g_ref, last_ref) = refs
    else:
        (p_ref, mu_ref, w0_ref, w2_ref, a0_ref, a2_ref, g2_ref, kk_ref, ka_ref,
         r_ref, w_ref, k_ref, v_ref, kn_ref, b_ref, g_ref, last_ref) = refs
        sh_ref = None
    ti = pl.program_id(1)
    rows = nb * tb
    tot = p_ref.shape[-1]
    x = p_ref[...].reshape(rows, tot)

    @pl.when(ti == 0)
    def _():
        last_ref[...] = sh_ref[...] if has_state else jnp.zeros((nb, 1, tot), f32)

    prev = _shift_rows(x, [_bcast_rows(last_ref[...], nb, tb)], tb, 1)
    last_ref[...] = p_ref[:, tb - 1:tb, :]
    xs = x + (prev - x) * mu_ref[...]
    dl, al, gl = lora
    c = 3 * width
    r, kr, vr = xs[:, :width], xs[:, width:2 * width], xs[:, 2 * width:c]
    wd, ad, gd = xs[:, c:c + dl], xs[:, c + dl:c + dl + al], xs[:, c + dl + al:c + dl + al + gl]
    z = -(w0_ref[...] + _dot(jnp.tanh(wd).astype(bf16), w2_ref[...]))
    softplus = jnp.maximum(z, 0.0) + jnp.log(1.0 + jnp.exp(-jnp.abs(z)))
    w_log = -softplus - 0.5
    decay = jnp.exp(-jnp.exp(w_log))
    a = _sigmoid(a0_ref[...] + _dot(ad.astype(bf16), a2_ref[...]))
    g = _dot(_sigmoid(gd).astype(bf16), g2_ref[...])
    kk = kr * kk_ref[...]
    ones_bd = _group_ones(V7X_LANES, RW_N)
    nrm = jnp.sqrt(_group_sum(kk * kk, ones_bd))
    kk = kk / jnp.maximum(nrm, 1e-12)
    k = kr * (1.0 + (a - 1.0) * ka_ref[...])
    shp = (nb, tb, width)
    r_ref[...] = r.reshape(shp)
    w_ref[...] = decay.reshape(shp)
    k_ref[...] = k.reshape(shp)
    v_ref[...] = vr.reshape(shp)
    kn_ref[...] = (-kk).reshape(shp)
    b_ref[...] = (kk * a).reshape(shp)
    g_ref[...] = g.reshape(shp)


def _rw_prep(p_rw, shift, wts, bsz, tlen):
    mu, w0, w2, a0, a2, g2, k_k, k_a = wts
    tot = p_rw.shape[1]
    width = w0.shape[1]
    lora = (w2.shape[0], a2.shape[0], g2.shape[0])
    nb, tb = _seq_block(bsz, tlen)
    has_state = shift is not None
    p3 = p_rw.reshape(bsz, tlen, tot)
    blk = lambda w: pl.BlockSpec((nb, tb, w), lambda bi, ti: (bi, ti, 0))
    full = lambda arr: pl.BlockSpec(arr.shape, lambda bi, ti: (0,) * arr.ndim)
    last_spec = pl.BlockSpec((nb, 1, tot), lambda bi, ti: (bi, 0, 0))
    small = [mu, w0, w2, a0, a2, g2, k_k, k_a]
    in_specs = [blk(tot)] + ([last_spec] if has_state else []) + [full(s) for s in small]
    args = [p3] + ([shift.reshape(bsz, 1, tot)] if has_state else []) + small
    seq = jax.ShapeDtypeStruct((bsz, tlen, width), f32)
    outs = pl.pallas_call(
        functools.partial(_rw_prep_kernel, nb=nb, tb=tb, width=width, lora=lora, has_state=has_state),
        grid=(bsz // nb, tlen // tb),
        in_specs=in_specs,
        out_specs=[blk(width)] * 7 + [last_spec],
        out_shape=[seq] * 7 + [jax.ShapeDtypeStruct((bsz, 1, tot), f32)],
        compiler_params=_params("parallel", "arbitrary"),
        name="rwkv7_prep",
    )(*args)
    return outs[:7], outs[7].reshape(bsz, tot)


def _rw_scan_kernel(*refs, nb, tb, pairs, has_state):
    if has_state:
        (r_ref, w_ref, k_ref, v_ref, kn_ref, b_ref, s0_ref, o_ref, s_ref,
         st_scr, lhs_a, lhs_v, lhs_o) = refs
    else:
        (r_ref, w_ref, k_ref, v_ref, kn_ref, b_ref, o_ref, s_ref,
         st_scr, lhs_a, lhs_v, lhs_o) = refs
        s0_ref = None
    ti = pl.program_id(1)
    n = RW_N
    lanes = RW_PAIR * n
    combos = [(b, p) for b in range(nb) for p in range(pairs)]

    @pl.when(ti == 0)
    def _():
        for c, (b, p) in enumerate(combos):
            if has_state:
                st_scr[c] = jnp.concatenate([s0_ref[b, RW_PAIR * p + h] for h in range(RW_PAIR)], axis=-1)
            else:
                st_scr[c] = jnp.zeros((n, lanes), f32)

    ones_bd = _group_ones(lanes, n)
    diag = (lax.broadcasted_iota(jnp.int32, (n, lanes), 0)
            == lax.broadcasted_iota(jnp.int32, (n, lanes), 1) % n)

    sub = V7X_SUBLANES

    def token_group(grp, carry):
        base = pl.multiple_of(grp * sub, sub)

        def row(ref, b, p, j):
            tile = ref[b, pl.ds(base, sub), pl.ds(p * lanes, lanes)]
            return jnp.broadcast_to(tile[j:j + 1], (n, lanes))

        o_rows = [[] for _ in combos]
        for j in range(sub):
            for c, (b, p) in enumerate(combos):
                lhs_a[c * n:(c + 1) * n, :] = (st_scr[c] * row(kn_ref, b, p, j)).astype(bf16)
                lhs_v[c * n:(c + 1) * n, :] = jnp.where(diag, row(v_ref, b, p, j), 0.0).astype(bf16)
            sa_all = _dot(lhs_a[...], ones_bd)
            vb_all = _dot(lhs_v[...], ones_bd)
            for c, (b, p) in enumerate(combos):
                s_new = (st_scr[c] * row(w_ref, b, p, j)
                         + sa_all[c * n:(c + 1) * n] * row(b_ref, b, p, j)
                         + vb_all[c * n:(c + 1) * n] * row(k_ref, b, p, j))
                st_scr[c] = s_new
                lhs_o[c * n:(c + 1) * n, :] = (s_new * row(r_ref, b, p, j)).astype(bf16)
            o_all = _dot(lhs_o[...], ones_bd)
            for c, (b, p) in enumerate(combos):
                o_rows[c].append(jnp.sum(jnp.where(diag, o_all[c * n:(c + 1) * n], 0.0),
                                         axis=0, keepdims=True))
        for c, (b, p) in enumerate(combos):
            o_ref[b, pl.ds(base, sub), pl.ds(p * lanes, lanes)] = jnp.concatenate(o_rows[c], axis=0)
        return carry

    lax.fori_loop(0, tb // sub, token_group, 0)

    @pl.when(ti == pl.num_programs(1) - 1)
    def _():
        for c, (b, p) in enumerate(combos):
            s = st_scr[c]
            for h in range(RW_PAIR):
                s_ref[b, RW_PAIR * p + h] = s[:, h * n:(h + 1) * n]


def _rw_scan(seqs, s0, bsz, tlen):
    r, w, k, v, kn, bb = seqs
    width = r.shape[-1]
    heads = width // RW_N
    pairs = heads // RW_PAIR
    nb = _tile(bsz, 4, align=1)
    tb = _tile(tlen, 64)
    has_state = s0 is not None
    blk = pl.BlockSpec((nb, tb, width), lambda bi, ti: (bi, ti, 0))
    st_spec = pl.BlockSpec((nb, heads, RW_N, RW_N), lambda bi, ti: (bi, 0, 0, 0))
    m = nb * pairs * RW_N
    o, s_new = pl.pallas_call(
        functools.partial(_rw_scan_kernel, nb=nb, tb=tb, pairs=pairs, has_state=has_state),
        grid=(bsz // nb, tlen // tb),
        in_specs=[blk] * 6 + ([st_spec] if has_state else []),
        out_specs=[blk, st_spec],
        out_shape=[jax.ShapeDtypeStruct((bsz, tlen, width), f32),
                   jax.ShapeDtypeStruct((bsz, heads, RW_N, RW_N), f32)],
        scratch_shapes=[pltpu.VMEM((nb * pairs, RW_N, RW_PAIR * RW_N), f32),
                        pltpu.VMEM((m, RW_PAIR * RW_N), bf16),
                        pltpu.VMEM((m, RW_PAIR * RW_N), bf16),
                        pltpu.VMEM((m, RW_PAIR * RW_N), bf16)],
        compiler_params=_params("parallel", "arbitrary"),
        name="rwkv7_scan",
    )(*([r, w, k, v, kn, bb] + ([s0] if has_state else [])))
    return o, s_new


def _rw_post_kernel(o_ref, r_ref, k_ref, v_ref, g_ref, lw_ref, lbias_ref, rk_ref, y_ref):
    ones_bd = _group_ones(V7X_LANES, RW_N)
    o = o_ref[...]
    mu = _group_sum(o, ones_bd) * (1.0 / RW_N)
    d = o - mu
    var = _group_sum(d * d, ones_bd) * (1.0 / RW_N)
    on = d * lax.rsqrt(var + RW_GN_EPS) * lw_ref[...] + lbias_ref[...]
    bonus = _group_sum(r_ref[...] * k_ref[...] * rk_ref[...], ones_bd) * v_ref[...]
    y_ref[...] = ((on + bonus) * g_ref[...]).astype(y_ref.dtype)


def _rw_post(o, r, k, v, g, ln_w, ln_b, r_k):
    n, width = o.shape
    tm = _tile(n, 256)
    blk = pl.BlockSpec((tm, width), lambda i: (i, 0))
    vec = pl.BlockSpec((1, width), lambda i: (0, 0))
    return pl.pallas_call(
        _rw_post_kernel,
        grid=(n // tm,),
        in_specs=[blk] * 5 + [vec] * 3,
        out_specs=blk,
        out_shape=jax.ShapeDtypeStruct((n, width), bf16),
        compiler_params=_params("parallel"),
        name="rwkv7_post",
    )(o, r, k, v, g, ln_w, ln_b, r_k)


def _conv_kernel(*refs, nb, tb, has_state):
    if has_state:
        sb_ref, sc_ref, sh_ref, st_ref, cw_ref, y_ref, new_ref = refs
    else:
        sb_ref, sc_ref, sh_ref, cw_ref, y_ref, new_ref = refs
        st_ref = None
    ti = pl.program_id(2)
    rows = nb * tb
    w = sb_ref.shape[-1]
    hist = SC_KSIZE - 1

    @pl.when(ti == 0)
    def _():
        new_ref[...] = st_ref[...] if has_state else jnp.zeros((nb, hist, w), f32)

    u = (sc_ref[...] * sh_ref[...]).reshape(rows, w)
    carry = new_ref[...]
    conv = cw_ref[hist:hist + 1, :] * u
    for back in range(1, hist + 1):
        first = [_bcast_rows(carry[:, hist - back + j:hist - back + j + 1, :], nb, tb) for j in range(back)]
        conv = conv + cw_ref[hist - back:hist - back + 1, :] * _shift_rows(u, first, tb, back)
    y_ref[...] = (sb_ref[...].reshape(rows, w) * conv).reshape(nb, tb, w).astype(y_ref.dtype)
    new_ref[...] = u.reshape(nb, tb, w)[:, tb - hist:tb, :]


def _conv(p_sc, state, conv_w, bsz, tlen):
    width = conv_w.shape[1]
    cw = min(width, 256)
    ncol = width // cw
    nb, tb = _seq_block(bsz, tlen)
    assert tb >= SC_KSIZE - 1
    has_state = state is not None
    p3 = p_sc.reshape(bsz, tlen, 3 * width)
    seg = lambda s: pl.BlockSpec((nb, tb, cw), lambda bi, cj, ti, s=s: (bi, ti, s * ncol + cj))
    st_spec = pl.BlockSpec((nb, SC_KSIZE - 1, cw), lambda bi, cj, ti: (bi, 0, cj))
    in_specs = ([seg(0), seg(1), seg(2)] + ([st_spec] if has_state else [])
                + [pl.BlockSpec((SC_KSIZE, cw), lambda bi, cj, ti: (0, cj))])
    args = [p3, p3, p3] + ([state] if has_state else []) + [conv_w]
    y, new = pl.pallas_call(
        functools.partial(_conv_kernel, nb=nb, tb=tb, has_state=has_state),
        grid=(bsz // nb, ncol, tlen // tb),
        in_specs=in_specs,
        out_specs=[pl.BlockSpec((nb, tb, cw), lambda bi, cj, ti: (bi, ti, cj)), st_spec],
        out_shape=[jax.ShapeDtypeStruct((bsz, tlen, width), bf16),
                   jax.ShapeDtypeStruct((bsz, SC_KSIZE - 1, width), f32)],
        compiler_params=_params("parallel", "parallel", "arbitrary"),
        name="short_conv",
    )(*args)
    return y.reshape(bsz * tlen, width), new


def _merge_kernel(a_ref, b_ref, c_ref, ga_ref, gb_ref, gc_ref, pa_ref, pb_ref, pc_ref, m_ref):
    m = (_sigmoid(ga_ref[...]) * _dot(a_ref[...], pa_ref[...])
         + _sigmoid(gb_ref[...]) * _dot(b_ref[...], pb_ref[...])
         + _sigmoid(gc_ref[...]) * _dot(c_ref[...], pc_ref[...]))
    m_ref[...] = m.astype(m_ref.dtype)


def _merge(ya, yb, yc, p_gate, w_pa, w_pb, w_pc):
    n = ya.shape[0]
    d = w_pa.shape[1]
    tm = _tile(n, 512)
    tn = _tile(d, 512)
    nj = d // tn
    act = lambda arr: pl.BlockSpec((tm, arr.shape[1]), lambda i, j: (i, 0))
    gate = lambda s: pl.BlockSpec((tm, tn), lambda i, j, s=s: (i, s * nj + j))
    wt = lambda arr: pl.BlockSpec((arr.shape[0], tn), lambda i, j: (0, j))
    return pl.pallas_call(
        _merge_kernel,
        grid=(n // tm, nj),
        in_specs=[act(ya), act(yb), act(yc), gate(0), gate(1), gate(2), wt(w_pa), wt(w_pb), wt(w_pc)],
        out_specs=pl.BlockSpec((tm, tn), lambda i, j: (i, j)),
        out_shape=jax.ShapeDtypeStruct((n, d), bf16),
        compiler_params=_params("parallel", "arbitrary"),
        name="gated_merge",
    )(ya, yb, yc, p_gate, p_gate, p_gate, w_pa, w_pb, w_pc)


def _oproj_kernel(m_ref, wo_ref, x_ref, g1_ref, g2_ref, x1_ref, h2_ref):
    mix = _dot(m_ref[...], wo_ref[...])
    x1 = x_ref[...] + _rms(mix, g1_ref[...])
    x1_ref[...] = x1
    h2_ref[...] = _rms(x1, g2_ref[...]).astype(h2_ref.dtype)


def _oproj(m, w_o, x, g_post_mix, g_pre_mlp):
    n, d = x.shape
    tm = _tile(n, 256)
    blk = pl.BlockSpec((tm, d), lambda i: (i, 0))
    vec = pl.BlockSpec((1, d), lambda i: (0, 0))
    return pl.pallas_call(
        _oproj_kernel,
        grid=(n // tm,),
        in_specs=[blk, pl.BlockSpec((d, d), lambda i: (0, 0)), blk, vec, vec],
        out_specs=[blk, blk],
        out_shape=[jax.ShapeDtypeStruct((n, d), f32), jax.ShapeDtypeStruct((n, d), bf16)],
        compiler_params=_params("parallel"),
        name="out_proj_norms",
    )(m, w_o, x, g_post_mix, g_pre_mlp)


def _mlp_kernel(h_ref, w1_ref, w2_ref, x1_ref, g_ref, y_ref, acc_ref):
    kf = pl.program_id(1)

    @pl.when(kf == 0)
    def _():
        acc_ref[...] = jnp.zeros_like(acc_ref)

    t = jnp.maximum(_dot(h_ref[...], w1_ref[...]), 0.0)
    acc_ref[...] += _dot((t * t).astype(bf16), w2_ref[...])

    @pl.when(kf == pl.num_programs(1) - 1)
    def _():
        y_ref[...] = x1_ref[...] + _rms(acc_ref[...], g_ref[...])


def _mlp(h2, w1, w2, x1, g_post_mlp):
    n, d = x1.shape
    dff = w1.shape[1]
    tm = _tile(n, 512)
    fc = _tile(dff, 512)
    blk = pl.BlockSpec((tm, d), lambda i, kf: (i, 0))
    return pl.pallas_call(
        _mlp_kernel,
        grid=(n // tm, dff // fc),
        in_specs=[blk, pl.BlockSpec((d, fc), lambda i, kf: (0, kf)), pl.BlockSpec((fc, d), lambda i, kf: (kf, 0)),
                  blk, pl.BlockSpec((1, d), lambda i, kf: (0, 0))],
        out_specs=blk,
        out_shape=jax.ShapeDtypeStruct((n, d), f32),
        scratch_shapes=[pltpu.VMEM((tm, d), f32)],
        compiler_params=_params("parallel", "arbitrary"),
        name="relu2_mlp",
    )(h2, w1, w2, x1, g_post_mlp)


def _layer(x, bsz, tlen, states, lw):
    hg_s0, rw_s0, rw_shift, sc_state = states
    h = _rmsnorm_cast(x, lw["g_pre_mix"])
    p_hg = _matmul(h, lw["w_in_hg"], "in_proj_hgrn")
    p_rw = _matmul(h, lw["w_in_rw"], "in_proj_rwkv")
    p_sc = _matmul(h, lw["w_in_sc"], "in_proj_conv")
    p_gate = _matmul(h, lw["w_in_gate"], "in_proj_gate")

    ya, hg_new = _hgrn(p_hg, lw["lb"], lw["hg_norm"], hg_s0, bsz, tlen)

    seqs, shift_new = _rw_prep(p_rw, rw_shift, lw["rw_prep"], bsz, tlen)
    r, w, k, v, kn, bb, g = seqs
    o_rw, rw_new = _rw_scan((r, w, k, v, kn, bb), rw_s0, bsz, tlen)
    flat = lambda t: t.reshape(bsz * tlen, t.shape[-1])
    yb = _rw_post(flat(o_rw), flat(r), flat(k), flat(v), flat(g), lw["rw_ln_w"], lw["rw_ln_b"], lw["rw_r_k"])

    yc, sc_new = _conv(p_sc, sc_state, lw["sc_conv_w"], bsz, tlen)

    m = _merge(ya, yb, yc, p_gate, lw["w_pa"], lw["w_pb"], lw["w_pc"])
    x1, h2 = _oproj(m, lw["w_o"], x, lw["g_post_mix"], lw["g_pre_mlp"])
    y = _mlp(h2, lw["w_ff1"], lw["w_ff2"], x1, lw["g_post_mlp"])
    return y, (hg_new, rw_new, shift_new, sc_new)


def kernel(x_prompt, x_sample, state_hgrn, state_rwkv, state_rwkv_shift, state_conv, norm_pre_mix, norm_post_mix, norm_pre_mlp, norm_post_mlp, w_in, hg_lb_logits, hg_norm, w_pa, rw_mu, rw_w0, rw_w2, rw_a0, rw_a2, rw_g2, rw_k_k, rw_k_a, rw_r_k, rw_ln_w, rw_ln_b, w_pb, sc_conv_w, w_pc, w_o, w_ff1, w_ff2):
    depth = w_in.shape[0]
    d_model = x_prompt.shape[-1]
    hg_width = hg_norm.shape[1]
    rw_shift_width = rw_mu.shape[1]
    sc_width = sc_conv_w.shape[2]
    off_rw = 4 * hg_width
    off_sc = off_rw + rw_shift_width
    off_gate = off_sc + 3 * sc_width
    assert w_in.shape[2] == off_gate + 3 * d_model

    lb_all = _lower_bounds(hg_lb_logits.astype(f32))
    row = lambda a, l: a[l].reshape(1, -1).astype(f32)
    cast = lambda a: a.astype(bf16)

    bp, tp, _ = x_prompt.shape
    bs, ts, _ = x_sample.shape
    yp = x_prompt.reshape(bp * tp, d_model).astype(f32)
    ys = x_sample.reshape(bs * ts, d_model).astype(f32)
    new_p, new_s = [], []
    for l in range(depth):
        w_in_l = w_in[l]
        lw = {
            "g_pre_mix": row(norm_pre_mix, l), "g_post_mix": row(norm_post_mix, l),
            "g_pre_mlp": row(norm_pre_mlp, l), "g_post_mlp": row(norm_post_mlp, l),
            "w_in_hg": cast(w_in_l[:, :off_rw]), "w_in_rw": cast(w_in_l[:, off_rw:off_sc]),
            "w_in_sc": cast(w_in_l[:, off_sc:off_gate]), "w_in_gate": cast(w_in_l[:, off_gate:]),
            "lb": lb_all[l:l + 1], "hg_norm": row(hg_norm, l),
            "w_pa": cast(w_pa[l]), "w_pb": cast(w_pb[l]), "w_pc": cast(w_pc[l]), "w_o": cast(w_o[l]),
            "rw_prep": (row(rw_mu, l), row(rw_w0, l), cast(rw_w2[l]), row(rw_a0, l), cast(rw_a2[l]),
                        cast(rw_g2[l]), row(rw_k_k, l), row(rw_k_a, l)),
            "rw_ln_w": row(rw_ln_w, l), "rw_ln_b": row(rw_ln_b, l), "rw_r_k": row(rw_r_k, l),
            "sc_conv_w": sc_conv_w[l].astype(f32),
            "w_ff1": cast(w_ff1[l]), "w_ff2": cast(w_ff2[l]),
        }
        yp, st = _layer(yp, bp, tp, (None, None, None, None), lw)
        new_p.append(st)
        ys, st = _layer(ys, bs, ts, (state_hgrn[l].astype(f32), state_rwkv[l].astype(f32),
                                     state_rwkv_shift[l].astype(f32), state_conv[l].astype(f32)), lw)
        new_s.append(st)
    stack = lambda sts, i: jnp.stack([s[i] for s in sts])
    return (yp.reshape(bp, tp, d_model), ys.reshape(bs, ts, d_model),
            stack(new_p, 0), stack(new_p, 1), stack(new_p, 2), stack(new_p, 3),
            stack(new_s, 0), stack(new_s, 1), stack(new_s, 2), stack(new_s, 3))
```

```python
import functools

import jax
import jax.numpy as jnp
from jax import lax
from jax.experimental import pallas as pl
from jax.experimental.pallas import tpu as pltpu

HG_DK = 128
RW_N = 64
HG_F_MIN = 1e-30
RW_GN_EPS = 64e-5
NORM_EPS = 1e-6
SC_KSIZE = 3

V7X_LANES = 128
V7X_SUBLANES = 8
V7X_BF16_ROWS = 16
V7X_VMEM_LIMIT_BYTES = 48 * 1024 * 1024
HG_CHUNK = 16
RW_PAIR = 2

f32 = jnp.float32
bf16 = jnp.bfloat16


def _tile(n, pref, align=V7X_SUBLANES):
    if n <= pref:
        return n
    for d in range(pref, 0, -1):
        if n % d == 0 and d % align == 0:
            return d
    return n


def _seq_block(bsz, tlen):
    if tlen >= 256:
        return 1, _tile(tlen, 256)
    return _tile(bsz, max(1, 128 // tlen), align=1), tlen


def _params(*sem):
    return pltpu.CompilerParams(dimension_semantics=sem, vmem_limit_bytes=V7X_VMEM_LIMIT_BYTES)


def _sigmoid(x):
    return 1.0 / (1.0 + jnp.exp(-x))


def _dot(a, b):
    return jnp.dot(a, b, preferred_element_type=f32)


def _rms(x, g):
    return x * lax.rsqrt(jnp.mean(x * x, axis=-1, keepdims=True) + NORM_EPS) * g


def _group_ones(width, group):
    r = lax.broadcasted_iota(jnp.int32, (width, width), 0) // group
    c = lax.broadcasted_iota(jnp.int32, (width, width), 1) // group
    return jnp.where(r == c, 1.0, 0.0).astype(bf16)


def _group_sum(z, ones_bd):
    outs = []
    for c in range(z.shape[-1] // V7X_LANES):
        zc = z[:, c * V7X_LANES:(c + 1) * V7X_LANES]
        hi = zc.astype(bf16)
        lo = (zc - hi.astype(f32)).astype(bf16)
        outs.append(_dot(hi, ones_bd) + _dot(lo, ones_bd))
    return outs[0] if len(outs) == 1 else jnp.concatenate(outs, axis=-1)


def _threaded_state(args, in_specs, prev_out, out_index):
    if prev_out is None:
        return {}
    args.append(prev_out)
    in_specs.append(pl.BlockSpec(memory_space=pl.ANY))
    return {len(args) - 1: out_index}


def _lb_kernel(logit_ref, lb_ref):
    z = logit_ref[...]
    depth = z.shape[0]
    m = jnp.max(z, axis=0, keepdims=True)
    e = jnp.exp(z - m)
    p = e / jnp.sum(e, axis=0, keepdims=True)
    acc = jnp.zeros_like(p[0:1])
    for l in range(depth):
        acc = acc + p[l:l + 1]
        lb_ref[l:l + 1, :] = jnp.clip(acc - p[0:1], 0.0, 1.0)


def _lower_bounds(logits):
    return pl.pallas_call(
        _lb_kernel, out_shape=jax.ShapeDtypeStruct(logits.shape, f32), name="hg_lower_bounds",
    )(logits)


def _rmsnorm_kernel(x_ref, g_ref, o_ref):
    o_ref[...] = _rms(x_ref[...], g_ref[...]).astype(o_ref.dtype)


def _rmsnorm_cast(x, g):
    n, d = x.shape
    tm = _tile(n, 512)
    return pl.pallas_call(
        _rmsnorm_kernel,
        grid=(n // tm,),
        in_specs=[pl.BlockSpec((tm, d), lambda i: (i, 0)), pl.BlockSpec((1, d), lambda i: (0, 0))],
        out_specs=pl.BlockSpec((tm, d), lambda i: (i, 0)),
        out_shape=jax.ShapeDtypeStruct((n, d), bf16),
        compiler_params=_params("parallel"),
        name="rmsnorm_cast",
    )(x, g)


def _proj_kernel(a_ref, w_ref, o_ref):
    o_ref[...] = _dot(a_ref[...], w_ref[0].astype(bf16))


def _proj(h, w_stack, layer, off, width, name):
    n, k = h.shape
    tm = _tile(n, 1024)
    tn = _tile(width, 512, align=V7X_LANES)
    w_spec = pl.BlockSpec((pl.Element(1), pl.Element(k), pl.Element(tn)),
                          lambda i, j: (layer, 0, pl.multiple_of(off + j * tn, V7X_LANES)))
    return pl.pallas_call(
        _proj_kernel,
        grid=(n // tm, width // tn),
        in_specs=[pl.BlockSpec((tm, k), lambda i, j: (i, 0)), w_spec],
        out_specs=pl.BlockSpec((tm, tn), lambda i, j: (i, j)),
        out_shape=jax.ShapeDtypeStruct((n, width), f32),
        compiler_params=_params("parallel", "arbitrary"),
        name=name,
    )(h, w_stack)


def _hgrn_kernel(*refs, nb, tb, chunk, has_state, has_prev):
    q_ref, f_ref, i_ref, og_ref, lb_ref, gn_ref, rsel_ref = refs[:7]
    s0_ref = refs[7] if has_state else None
    o_ref, s_ref, st_scr = refs[7 + has_state + has_prev:]
    ti = pl.program_id(2)
    rows = nb * tb
    dk = HG_DK
    nchunk = rows // chunk

    @pl.when(ti == 0)
    def _():
        for b in range(nb):
            st_scr[b] = s0_ref[b, 0].T if has_state else jnp.zeros((dk, dk), f32)

    lb = lb_ref[...]
    hq = q_ref[...].reshape(rows, dk)
    qq = hq * _sigmoid(hq)
    f = lb + (1.0 - lb) * _sigmoid(f_ref[...].reshape(rows, dk))
    logf = jnp.log(jnp.maximum(f, HG_F_MIN))
    kk = 1.0 - f
    vv = i_ref[...].reshape(rows, dk)
    hog = og_ref[...].reshape(rows, dk)

    t_idx = lax.broadcasted_iota(jnp.int32, (rows, dk), 0) % chunk
    bcum = logf
    step = 1
    while step < chunk:
        bcum = bcum + jnp.where(t_idx >= step, pltpu.roll(bcum, step, 0), 0.0)
        step *= 2

    def chunk_row(a, s):
        a3 = a.reshape(nchunk, chunk, dk)
        return jnp.broadcast_to(a3[:, s:s + 1, :], (nchunk, chunk, dk)).reshape(rows, dk)

    b_last = chunk_row(bcum, chunk - 1)
    q_dec = (qq * jnp.exp(bcum)).astype(bf16)
    k_dec = (kk * jnp.exp(b_last - bcum)).astype(bf16)

    zs = []
    for s in range(chunk):
        dec = jnp.exp(jnp.minimum(bcum - chunk_row(bcum, s), 0.0))
        zs.append(jnp.where(t_idx >= s, qq * chunk_row(kk, s) * dec, 0.0).astype(bf16))
    x2 = jnp.concatenate(zs, axis=-1)
    a_rep = _dot(x2, rsel_ref[...])
    same_chunk = (lax.broadcasted_iota(jnp.int32, (rows, rows), 0) // chunk
                  == lax.broadcasted_iota(jnp.int32, (rows, rows), 1) // chunk)
    o_intra = _dot(jnp.where(same_chunk, a_rep, 0.0).astype(bf16), vv.astype(bf16))

    v_t = vv.T
    col_chunk = lax.broadcasted_iota(jnp.int32, (dk, rows), 1) // chunk
    lhs = jnp.concatenate([jnp.where(col_chunk == c, v_t, 0.0).astype(bf16) for c in range(nchunk)], axis=0)
    u_all = _dot(lhs, k_dec)

    outs = []
    for b in range(nb):
        st = st_scr[b]
        for g in range(tb // chunk):
            c = b * (tb // chunk) + g
            r0 = c * chunk
            inter = lax.dot_general(q_dec[r0:r0 + chunk], st.astype(bf16),
                                    (((1,), (1,)), ((), ())), preferred_element_type=f32)
            outs.append(inter + o_intra[r0:r0 + chunk])
            st = jnp.exp(bcum[r0 + chunk - 1:r0 + chunk]) * st + u_all[c * dk:(c + 1) * dk]
        st_scr[b] = st
    o = jnp.concatenate(outs, axis=0)
    o = o * lax.rsqrt(jnp.mean(o * o, axis=-1, keepdims=True) + NORM_EPS)
    o_ref[...] = (o * gn_ref[...] * (hog * _sigmoid(hog))).reshape(nb, tb, dk).astype(o_ref.dtype)

    @pl.when(ti == pl.num_programs(2) - 1)
    def _():
        for b in range(nb):
            s_ref[b, 0] = st_scr[b].T


def _hgrn(p_hg, lb, gn, s_in, s_prev, layer, depth, bsz, tlen):
    width = p_hg.shape[1] // 4
    heads = width // HG_DK
    chunk = min(HG_CHUNK, tlen)
    nb, tb = _seq_block(bsz, tlen)
    assert tb % chunk == 0
    p3 = p_hg.reshape(bsz, tlen, 4 * width)
    seg = lambda s: pl.BlockSpec((nb, tb, HG_DK), lambda bi, h, ti, s=s: (bi, ti, s * heads + h))
    vec = pl.BlockSpec((1, HG_DK), lambda bi, h, ti: (0, h))
    st_spec = pl.BlockSpec((None, nb, 1, HG_DK, HG_DK), lambda bi, h, ti: (layer, bi, h, 0, 0))
    has_state = s_in is not None
    rows = nb * tb
    rsel = (jnp.arange(chunk * HG_DK)[:, None] // HG_DK == jnp.arange(rows)[None, :] % chunk).astype(bf16)
    rsel_spec = pl.BlockSpec((chunk * HG_DK, rows), lambda bi, h, ti: (0, 0))
    in_specs = [seg(0), seg(1), seg(2), seg(3), vec, vec, rsel_spec] + ([st_spec] if has_state else [])
    args = [p3, p3, p3, p3, lb, gn, rsel] + ([s_in] if has_state else [])
    aliases = _threaded_state(args, in_specs, s_prev, 1)
    o, s_new = pl.pallas_call(
        functools.partial(_hgrn_kernel, nb=nb, tb=tb, chunk=chunk, has_state=has_state,
                          has_prev=s_prev is not None),
        grid=(bsz // nb, heads, tlen // tb),
        in_specs=in_specs,
        out_specs=[pl.BlockSpec((nb, tb, HG_DK), lambda bi, h, ti: (bi, ti, h)), st_spec],
        out_shape=[jax.ShapeDtypeStruct((bsz, tlen, width), bf16),
                   jax.ShapeDtypeStruct((depth, bsz, heads, HG_DK, HG_DK), f32)],
        scratch_shapes=[pltpu.VMEM((nb, HG_DK, HG_DK), f32)],
        input_output_aliases=aliases,
        compiler_params=_params("parallel", "parallel", "arbitrary"),
        name="hgrn2_scan",
    )(*args)
    return o.reshape(bsz * tlen, width), s_new


def _shift_rows(x, first, tb, by):
    rows = x.shape[0]
    t_idx = lax.broadcasted_iota(jnp.int32, (rows, 1), 0) % tb
    out = pltpu.roll(x, by, 0) if tb > by else x
    for j in range(by):
        out = jnp.where(t_idx == j, first[j], out)
    return out


def _bcast_rows(v3, nb, tb):
    w = v3.shape[-1]
    return jnp.broadcast_to(v3, (nb, tb, w)).reshape(nb * tb, w)


def _rw_prep_kernel(*refs, nb, tb, width, lora, has_state, has_prev):
    pm_ref, pl_ref = refs[:2]
    sh_ref = refs[2] if has_state else None
    mu_ref, w0_ref, w2_ref, a0_ref, a2_ref, g2_ref, kk_ref, ka_ref = refs[2 + has_state:10 + has_state]
    r_ref, w_ref, k_ref, v_ref, kn_ref, b_ref, g_ref, last_ref = refs[10 + has_state + has_prev:]
    ti = pl.program_id(1)
    rows = nb * tb
    tot = last_ref.shape[-1]
    x = jnp.concatenate([pm_ref[...], pl_ref[...]], axis=-1).reshape(rows, tot)

    @pl.when(ti == 0)
    def _():
        last_ref[...] = sh_ref[...] if has_state else jnp.zeros((nb, 1, tot), f32)

    prev = _shift_rows(x, [_bcast_rows(last_ref[...], nb, tb)], tb, 1)
    last_ref[...] = jnp.concatenate([pm_ref[:, tb - 1:tb, :], pl_ref[:, tb - 1:tb, :]], axis=-1)
    xs = x + (prev - x) * mu_ref[...]
    dl, al, gl = lora
    c = 3 * width
    r, kr, vr = xs[:, :width], xs[:, width:2 * width], xs[:, 2 * width:c]
    wd, ad, gd = xs[:, c:c + dl], xs[:, c + dl:c + dl + al], xs[:, c + dl + al:c + dl + al + gl]
    z = -(w0_ref[...] + _dot(jnp.tanh(wd).astype(bf16), w2_ref[...]))
    softplus = jnp.maximum(z, 0.0) + jnp.log(1.0 + jnp.exp(-jnp.abs(z)))
    w_log = -softplus - 0.5
    decay = jnp.exp(-jnp.exp(w_log))
    a = _sigmoid(a0_ref[...] + _dot(ad.astype(bf16), a2_ref[...]))
    g = _dot(_sigmoid(gd).astype(bf16), g2_ref[...])
    kk = kr * kk_ref[...]
    ones_bd = _group_ones(V7X_LANES, RW_N)
    nrm = jnp.sqrt(_group_sum(kk * kk, ones_bd))
    kk = kk / jnp.maximum(nrm, 1e-12)
    k = kr * (1.0 + (a - 1.0) * ka_ref[...])
    shp = (nb, tb, width)
    r_ref[...] = r.reshape(shp)
    w_ref[...] = decay.reshape(shp)
    k_ref[...] = k.reshape(shp)
    v_ref[...] = vr.reshape(shp)
    kn_ref[...] = (-kk).reshape(shp)
    b_ref[...] = (kk * a).reshape(shp)
    g_ref[...] = g.reshape(shp)


def _rw_prep(p_main, p_lora, shift_in, shift_prev, layer, depth, wts, bsz, tlen):
    mu, w0, w2, a0, a2, g2, k_k, k_a = wts
    width = w0.shape[1]
    wl = p_lora.shape[1]
    tot = 3 * width + wl
    lora = (w2.shape[0], a2.shape[0], g2.shape[0])
    nb, tb = _seq_block(bsz, tlen)
    has_state = shift_in is not None
    blk = lambda w: pl.BlockSpec((nb, tb, w), lambda bi, ti: (bi, ti, 0))
    full = lambda arr: pl.BlockSpec(arr.shape, lambda bi, ti: (0,) * arr.ndim)
    last_spec = pl.BlockSpec((None, nb, 1, tot), lambda bi, ti: (layer, bi, 0, 0))
    small = [mu, w0, w2, a0, a2, g2, k_k, k_a]
    in_specs = [blk(3 * width), blk(wl)] + ([last_spec] if has_state else []) + [full(s) for s in small]
    args = ([p_main.reshape(bsz, tlen, 3 * width), p_lora.reshape(bsz, tlen, wl)]
            + ([shift_in] if has_state else []) + small)
    aliases = _threaded_state(args, in_specs, shift_prev, 7)
    seq = jax.ShapeDtypeStruct((bsz, tlen, width), f32)
    outs = pl.pallas_call(
        functools.partial(_rw_prep_kernel, nb=nb, tb=tb, width=width, lora=lora, has_state=has_state,
                          has_prev=shift_prev is not None),
        grid=(bsz // nb, tlen // tb),
        in_specs=in_specs,
        out_specs=[blk(width)] * 7 + [last_spec],
        out_shape=[seq] * 7 + [jax.ShapeDtypeStruct((depth, bsz, 1, tot), f32)],
        input_output_aliases=aliases,
        compiler_params=_params("parallel", "arbitrary"),
        name="rwkv7_prep",
    )(*args)
    return outs[:7], outs[7]


def _rw_scan_kernel(*refs, nb, tb, pairs, has_state, has_prev):
    r_ref, w_ref, k_ref, v_ref, kn_ref, b_ref = refs[:6]
    s0_ref = refs[6] if has_state else None
    o_ref, s_ref, st_scr, sb_scr, lhs_a, lhs_v, lhs_o = refs[6 + has_state + has_prev:]
    ti = pl.program_id(1)
    n = RW_N
    lanes = RW_PAIR * n
    sub = V7X_SUBLANES
    pk = V7X_BF16_ROWS
    combos = [(b, p) for b in range(nb) for p in range(pairs)]

    @pl.when(ti == 0)
    def _():
        for c, (b, p) in enumerate(combos):
            if has_state:
                s = jnp.concatenate([s0_ref[b, RW_PAIR * p + h] for h in range(RW_PAIR)], axis=-1)
            else:
                s = jnp.zeros((n, lanes), f32)
            st_scr[c] = s
            sb_scr[c] = s.astype(bf16)

    ones_bd = _group_ones(lanes, n)
    diag = (lax.broadcasted_iota(jnp.int32, (n, lanes), 0)
            == lax.broadcasted_iota(jnp.int32, (n, lanes), 1) % n)
    diag_bf = jnp.where(diag, 1.0, 0.0).astype(bf16)

    def times_row(x_bf, row_bf):
        return (x_bf.reshape(n // pk, pk, lanes) * row_bf[None]).reshape(n, lanes)

    def token_group(grp, carry):
        base = pl.multiple_of(grp * sub, sub)

        def row(ref, b, p, j, height):
            tile = ref[b, pl.ds(base, sub), pl.ds(p * lanes, lanes)]
            return jnp.broadcast_to(tile[j:j + 1], (height, lanes))

        o_rows = [[] for _ in combos]
        for j in range(sub):
            for c, (b, p) in enumerate(combos):
                lhs_a[c * n:(c + 1) * n, :] = times_row(sb_scr[c], row(kn_ref, b, p, j, pk).astype(bf16))
                lhs_v[c * n:(c + 1) * n, :] = times_row(diag_bf, row(v_ref, b, p, j, pk).astype(bf16))
            sa_all = _dot(lhs_a[...], ones_bd)
            vb_all = _dot(lhs_v[...], ones_bd)
            for c, (b, p) in enumerate(combos):
                s_new = (st_scr[c] * row(w_ref, b, p, j, n)
                         + sa_all[c * n:(c + 1) * n] * row(b_ref, b, p, j, n)
                         + vb_all[c * n:(c + 1) * n] * row(k_ref, b, p, j, n))
                st_scr[c] = s_new
                s_bf = s_new.astype(bf16)
                sb_scr[c] = s_bf
                lhs_o[c * n:(c + 1) * n, :] = times_row(s_bf, row(r_ref, b, p, j, pk).astype(bf16))
            o_all = _dot(lhs_o[...], ones_bd)
            for c, (b, p) in enumerate(combos):
                o_rows[c].append(jnp.sum(jnp.where(diag, o_all[c * n:(c + 1) * n], 0.0),
                                         axis=0, keepdims=True))
        for c, (b, p) in enumerate(combos):
            o_ref[b, pl.ds(base, sub), pl.ds(p * lanes, lanes)] = jnp.concatenate(o_rows[c], axis=0)
        return carry

    lax.fori_loop(0, tb // sub, token_group, 0)

    @pl.when(ti == pl.num_programs(1) - 1)
    def _():
        for c, (b, p) in enumerate(combos):
            s = st_scr[c]
            for h in range(RW_PAIR):
                s_ref[b, RW_PAIR * p + h] = s[:, h * n:(h + 1) * n]


def _rw_scan(seqs, s_in, s_prev, layer, depth, bsz, tlen):
    r, w, k, v, kn, bb = seqs
    width = r.shape[-1]
    heads = width // RW_N
    pairs = heads // RW_PAIR
    nb = _tile(bsz, 4, align=1)
    tb = _tile(tlen, 64)
    has_state = s_in is not None
    blk = pl.BlockSpec((nb, tb, width), lambda bi, ti: (bi, ti, 0))
    st_spec = pl.BlockSpec((None, nb, heads, RW_N, RW_N), lambda bi, ti: (layer, bi, 0, 0, 0))
    m = nb * pairs * RW_N
    lanes = RW_PAIR * RW_N
    in_specs = [blk] * 6 + ([st_spec] if has_state else [])
    args = [r, w, k, v, kn, bb] + ([s_in] if has_state else [])
    aliases = _threaded_state(args, in_specs, s_prev, 1)
    o, s_new = pl.pallas_call(
        functools.partial(_rw_scan_kernel, nb=nb, tb=tb, pairs=pairs, has_state=has_state,
                          has_prev=s_prev is not None),
        grid=(bsz // nb, tlen // tb),
        in_specs=in_specs,
        out_specs=[blk, st_spec],
        out_shape=[jax.ShapeDtypeStruct((bsz, tlen, width), f32),
                   jax.ShapeDtypeStruct((depth, bsz, heads, RW_N, RW_N), f32)],
        scratch_shapes=[pltpu.VMEM((nb * pairs, RW_N, lanes), f32),
                        pltpu.VMEM((nb * pairs, RW_N, lanes), bf16),
                        pltpu.VMEM((m, lanes), bf16),
                        pltpu.VMEM((m, lanes), bf16),
                        pltpu.VMEM((m, lanes), bf16)],
        input_output_aliases=aliases,
        compiler_params=_params("parallel", "arbitrary"),
        name="rwkv7_scan",
    )(*args)
    return o, s_new


def _rw_post_kernel(o_ref, r_ref, k_ref, v_ref, g_ref, lw_ref, lbias_ref, rk_ref, y_ref):
    ones_bd = _group_ones(V7X_LANES, RW_N)
    o = o_ref[...]
    mu = _group_sum(o, ones_bd) * (1.0 / RW_N)
    d = o - mu
    var = _group_sum(d * d, ones_bd) * (1.0 / RW_N)
    on = d * lax.rsqrt(var + RW_GN_EPS) * lw_ref[...] + lbias_ref[...]
    bonus = _group_sum(r_ref[...] * k_ref[...] * rk_ref[...], ones_bd) * v_ref[...]
    y_ref[...] = ((on + bonus) * g_ref[...]).astype(y_ref.dtype)


def _rw_post(o, r, k, v, g, ln_w, ln_b, r_k):
    n, width = o.shape
    tm = _tile(n, 256)
    blk = pl.BlockSpec((tm, width), lambda i: (i, 0))
    vec = pl.BlockSpec((1, width), lambda i: (0, 0))
    return pl.pallas_call(
        _rw_post_kernel,
        grid=(n // tm,),
        in_specs=[blk] * 5 + [vec] * 3,
        out_specs=blk,
        out_shape=jax.ShapeDtypeStruct((n, width), bf16),
        compiler_params=_params("parallel"),
        name="rwkv7_post",
    )(o, r, k, v, g, ln_w, ln_b, r_k)


def _conv_kernel(*refs, nb, tb, has_state, has_prev):
    sb_ref, sc_ref, sh_ref = refs[:3]
    st_ref = refs[3] if has_state else None
    cw_ref = refs[3 + has_state]
    y_ref, new_ref = refs[4 + has_state + has_prev:]
    ti = pl.program_id(2)
    rows = nb * tb
    w = sb_ref.shape[-1]
    hist = SC_KSIZE - 1

    @pl.when(ti == 0)
    def _():
        new_ref[...] = st_ref[...] if has_state else jnp.zeros((nb, hist, w), f32)

    u = (sc_ref[...] * sh_ref[...]).reshape(rows, w)
    carry = new_ref[...]
    conv = cw_ref[hist:hist + 1, :] * u
    for back in range(1, hist + 1):
        first = [_bcast_rows(carry[:, hist - back + j:hist - back + j + 1, :], nb, tb) for j in range(back)]
        conv = conv + cw_ref[hist - back:hist - back + 1, :] * _shift_rows(u, first, tb, back)
    y_ref[...] = (sb_ref[...].reshape(rows, w) * conv).reshape(nb, tb, w).astype(y_ref.dtype)
    new_ref[...] = u.reshape(nb, tb, w)[:, tb - hist:tb, :]


def _conv(p_sc, st_in, st_prev, layer, depth, conv_w, bsz, tlen):
    width = conv_w.shape[1]
    cw = min(width, 256)
    ncol = width // cw
    nb, tb = _seq_block(bsz, tlen)
    assert tb >= SC_KSIZE - 1
    has_state = st_in is not None
    p3 = p_sc.reshape(bsz, tlen, 3 * width)
    seg = lambda s: pl.BlockSpec((nb, tb, cw), lambda bi, cj, ti, s=s: (bi, ti, s * ncol + cj))
    st_spec = pl.BlockSpec((None, nb, SC_KSIZE - 1, cw), lambda bi, cj, ti: (layer, bi, 0, cj))
    in_specs = ([seg(0), seg(1), seg(2)] + ([st_spec] if has_state else [])
                + [pl.BlockSpec((SC_KSIZE, cw), lambda bi, cj, ti: (0, cj))])
    args = [p3, p3, p3] + ([st_in] if has_state else []) + [conv_w]
    aliases = _threaded_state(args, in_specs, st_prev, 1)
    y, new = pl.pallas_call(
        functools.partial(_conv_kernel, nb=nb, tb=tb, has_state=has_state, has_prev=st_prev is not None),
        grid=(bsz // nb, ncol, tlen // tb),
        in_specs=in_specs,
        out_specs=[pl.BlockSpec((nb, tb, cw), lambda bi, cj, ti: (bi, ti, cj)), st_spec],
        out_shape=[jax.ShapeDtypeStruct((bsz, tlen, width), bf16),
                   jax.ShapeDtypeStruct((depth, bsz, SC_KSIZE - 1, width), f32)],
        input_output_aliases=aliases,
        compiler_params=_params("parallel", "parallel", "arbitrary"),
        name="short_conv",
    )(*args)
    return y.reshape(bsz * tlen, width), new


def _merge_kernel(a_ref, b_ref, c_ref, ga_ref, gb_ref, gc_ref, pa_ref, pb_ref, pc_ref, m_ref):
    m = (_sigmoid(ga_ref[...]) * _dot(a_ref[...], pa_ref[...].astype(bf16))
         + _sigmoid(gb_ref[...]) * _dot(b_ref[...], pb_ref[...].astype(bf16))
         + _sigmoid(gc_ref[...]) * _dot(c_ref[...], pc_ref[...].astype(bf16)))
    m_ref[...] = m.astype(m_ref.dtype)


def _merge(ya, yb, yc, p_gate, w_pa, w_pb, w_pc, layer):
    n = ya.shape[0]
    d = w_pa.shape[2]
    tm = _tile(n, 512)
    tn = _tile(d, 512, align=V7X_LANES)
    nj = d // tn
    act = lambda arr: pl.BlockSpec((tm, arr.shape[1]), lambda i, j: (i, 0))
    gate = lambda s: pl.BlockSpec((tm, tn), lambda i, j, s=s: (i, s * nj + j))
    wt = lambda arr: pl.BlockSpec((None, arr.shape[1], tn), lambda i, j: (layer, 0, j))
    return pl.pallas_call(
        _merge_kernel,
        grid=(n // tm, nj),
        in_specs=[act(ya), act(yb), act(yc), gate(0), gate(1), gate(2), wt(w_pa), wt(w_pb), wt(w_pc)],
        out_specs=pl.BlockSpec((tm, tn), lambda i, j: (i, j)),
        out_shape=jax.ShapeDtypeStruct((n, d), bf16),
        compiler_params=_params("parallel", "arbitrary"),
        name="gated_merge",
    )(ya, yb, yc, p_gate, p_gate, p_gate, w_pa, w_pb, w_pc)


def _oproj_kernel(m_ref, wo_ref, x_ref, g1_ref, g2_ref, x1_ref, h2_ref):
    mix = _dot(m_ref[...], wo_ref[...])
    x1 = x_ref[...] + _rms(mix, g1_ref[...])
    x1_ref[...] = x1
    h2_ref[...] = _rms(x1, g2_ref[...]).astype(h2_ref.dtype)


def _oproj(m, w_o, layer, x, g_post_mix, g_pre_mlp):
    n, d = x.shape
    tm = _tile(n, 256)
    blk = pl.BlockSpec((tm, d), lambda i: (i, 0))
    vec = pl.BlockSpec((1, d), lambda i: (0, 0))
    return pl.pallas_call(
        _oproj_kernel,
        grid=(n // tm,),
        in_specs=[blk, pl.BlockSpec((None, d, d), lambda i: (layer, 0, 0)), blk, vec, vec],
        out_specs=[blk, blk],
        out_shape=[jax.ShapeDtypeStruct((n, d), f32), jax.ShapeDtypeStruct((n, d), bf16)],
        compiler_params=_params("parallel"),
        name="out_proj_norms",
    )(m, w_o, x, g_post_mix, g_pre_mlp)


def _mlp_kernel(h_ref, w1_ref, w2_ref, ff_ref):
    kf = pl.program_id(1)

    @pl.when(kf == 0)
    def _():
        ff_ref[...] = jnp.zeros_like(ff_ref)

    t = jnp.maximum(_dot(h_ref[...], w1_ref[...].astype(bf16)), 0.0)
    ff_ref[...] += _dot((t * t).astype(bf16), w2_ref[...].astype(bf16))


def _mlp(h2, w1, w2, layer):
    n, d = h2.shape
    dff = w1.shape[2]
    tm = _tile(n, 1024)
    fc = _tile(dff, 512, align=V7X_LANES)
    once = pl.Buffered(1)
    return pl.pallas_call(
        _mlp_kernel,
        grid=(n // tm, dff // fc),
        in_specs=[pl.BlockSpec((tm, d), lambda i, kf: (i, 0), pipeline_mode=once),
                  pl.BlockSpec((None, d, fc), lambda i, kf: (layer, 0, kf)),
                  pl.BlockSpec((None, fc, d), lambda i, kf: (layer, kf, 0))],
        out_specs=pl.BlockSpec((tm, d), lambda i, kf: (i, 0), pipeline_mode=once),
        out_shape=jax.ShapeDtypeStruct((n, d), f32),
        compiler_params=_params("parallel", "arbitrary"),
        name="relu2_mlp",
    )(h2, w1, w2)


def _residual_kernel(*refs, has_next):
    if has_next:
        x1_ref, ff_ref, g_ref, gn_ref, y_ref, h_ref = refs
    else:
        x1_ref, ff_ref, g_ref, y_ref = refs
    y = x1_ref[...] + _rms(ff_ref[...], g_ref[...])
    y_ref[...] = y
    if has_next:
        h_ref[...] = _rms(y, gn_ref[...]).astype(h_ref.dtype)


def _residual(x1, ff, g_post_mlp, g_next):
    n, d = x1.shape
    tm = _tile(n, 512)
    blk = pl.BlockSpec((tm, d), lambda i: (i, 0))
    vec = pl.BlockSpec((1, d), lambda i: (0, 0))
    has_next = g_next is not None
    outs = pl.pallas_call(
        functools.partial(_residual_kernel, has_next=has_next),
        grid=(n // tm,),
        in_specs=[blk, blk, vec] + ([vec] if has_next else []),
        out_specs=[blk] + ([blk] if has_next else []),
        out_shape=[jax.ShapeDtypeStruct((n, d), f32)] + ([jax.ShapeDtypeStruct((n, d), bf16)] if has_next else []),
        compiler_params=_params("parallel"),
        name="mlp_residual_norm",
    )(*([x1, ff, g_post_mlp] + ([g_next] if has_next else [])))
    return (outs[0], outs[1]) if has_next else (outs[0], None)


def _layer(x, h, bsz, tlen, states_in, states_prev, layer, depth, wts, lw, g_next):
    hg_in, rw_in, shift_in, sc_in = states_in
    hg_prev, rw_prev, shift_prev, sc_prev = states_prev
    w_in, offs = wts["w_in"], wts["offs"]
    p_hg = _proj(h, w_in, layer, offs[0], offs[1] - offs[0], "in_proj_hgrn")
    p_rkv = _proj(h, w_in, layer, offs[1], offs[2] - offs[1], "in_proj_rwkv")
    p_lora = _proj(h, w_in, layer, offs[2], offs[3] - offs[2], "in_proj_rwkv_lora")
    p_sc = _proj(h, w_in, layer, offs[3], offs[4] - offs[3], "in_proj_conv")
    p_gate = _proj(h, w_in, layer, offs[4], offs[5] - offs[4], "in_proj_gate")

    ya, hg_new = _hgrn(p_hg, lw["lb"], lw["hg_norm"], hg_in, hg_prev, layer, depth, bsz, tlen)

    seqs, shift_new = _rw_prep(p_rkv, p_lora, shift_in, shift_prev, layer, depth, lw["rw_prep"], bsz, tlen)
    r, w, k, v, kn, bb, g = seqs
    o_rw, rw_new = _rw_scan((r, w, k, v, kn, bb), rw_in, rw_prev, layer, depth, bsz, tlen)
    flat = lambda t: t.reshape(bsz * tlen, t.shape[-1])
    yb = _rw_post(flat(o_rw), flat(r), flat(k), flat(v), flat(g), lw["rw_ln_w"], lw["rw_ln_b"], lw["rw_r_k"])

    yc, sc_new = _conv(p_sc, sc_in, sc_prev, layer, depth, lw["sc_conv_w"], bsz, tlen)

    m = _merge(ya, yb, yc, p_gate, wts["w_pa"], wts["w_pb"], wts["w_pc"], layer)
    x1, h2 = _oproj(m, wts["w_o"], layer, x, lw["g_post_mix"], lw["g_pre_mlp"])
    ff = _mlp(h2, wts["w_ff1"], wts["w_ff2"], layer)
    y, h_next = _residual(x1, ff, lw["g_post_mlp"], g_next)
    return y, h_next, (hg_new, rw_new, shift_new, sc_new)


def kernel(x_prompt, x_sample, state_hgrn, state_rwkv, state_rwkv_shift, state_conv, norm_pre_mix, norm_post_mix, norm_pre_mlp, norm_post_mlp, w_in, hg_lb_logits, hg_norm, w_pa, rw_mu, rw_w0, rw_w2, rw_a0, rw_a2, rw_g2, rw_k_k, rw_k_a, rw_r_k, rw_ln_w, rw_ln_b, w_pb, sc_conv_w, w_pc, w_o, w_ff1, w_ff2):
    depth = w_in.shape[0]
    d_model = x_prompt.shape[-1]
    hg_width = hg_norm.shape[1]
    rw_width = rw_w0.shape[1]
    rw_shift_width = rw_mu.shape[1]
    sc_width = sc_conv_w.shape[2]
    off_rw = 4 * hg_width
    off_lora = off_rw + 3 * rw_width
    off_sc = off_rw + rw_shift_width
    off_gate = off_sc + 3 * sc_width
    assert w_in.shape[2] == off_gate + 3 * d_model

    lb_all = _lower_bounds(hg_lb_logits.astype(f32))
    row = lambda a, l: a[l].reshape(1, -1).astype(f32)
    wts = {"w_in": w_in.astype(f32), "offs": (0, off_rw, off_lora, off_sc, off_gate, w_in.shape[2]),
           "w_pa": w_pa.astype(f32), "w_pb": w_pb.astype(f32), "w_pc": w_pc.astype(f32),
           "w_o": w_o.astype(bf16), "w_ff1": w_ff1.astype(f32), "w_ff2": w_ff2.astype(f32)}

    bp, tp, _ = x_prompt.shape
    bs, ts, _ = x_sample.shape
    yp = x_prompt.reshape(bp * tp, d_model).astype(f32)
    ys = x_sample.reshape(bs * ts, d_model).astype(f32)
    hp = _rmsnorm_cast(yp, row(norm_pre_mix, 0))
    hs = _rmsnorm_cast(ys, row(norm_pre_mix, 0))
    sample_in = (state_hgrn.astype(f32), state_rwkv.astype(f32),
                 state_rwkv_shift.astype(f32).reshape(depth, bs, 1, rw_shift_width), state_conv.astype(f32))
    new_p = new_s = (None, None, None, None)
    for l in range(depth):
        lw = {
            "g_post_mix": row(norm_post_mix, l), "g_pre_mlp": row(norm_pre_mlp, l),
            "g_post_mlp": row(norm_post_mlp, l),
            "lb": lb_all[l:l + 1], "hg_norm": row(hg_norm, l),
            "rw_prep": (row(rw_mu, l), row(rw_w0, l), rw_w2[l].astype(bf16), row(rw_a0, l), rw_a2[l].astype(bf16),
                        rw_g2[l].astype(bf16), row(rw_k_k, l), row(rw_k_a, l)),
            "rw_ln_w": row(rw_ln_w, l), "rw_ln_b": row(rw_ln_b, l), "rw_r_k": row(rw_r_k, l),
            "sc_conv_w": sc_conv_w[l].astype(f32),
        }
        g_next = row(norm_pre_mix, l + 1) if l + 1 < depth else None
        yp, hp, new_p = _layer(yp, hp, bp, tp, (None, None, None, None), new_p, l, depth, wts, lw, g_next)
        ys, hs, new_s = _layer(ys, hs, bs, ts, sample_in, new_s, l, depth, wts, lw, g_next)
    shift = lambda s, b: s.reshape(depth, b, rw_shift_width)
    return (yp.reshape(bp, tp, d_model), ys.reshape(bs, ts, d_model),
            new_p[0], new_p[1], shift(new_p[2], bp), new_p[3],
            new_s[0], new_s[1], shift(new_s[2], bs), new_s[3])
```

```python
import functools

import jax
import jax.numpy as jnp
from jax import lax
from jax.experimental import pallas as pl
from jax.experimental.pallas import tpu as pltpu

HG_DK = 128
RW_N = 64
HG_F_MIN = 1e-30
RW_GN_EPS = 64e-5
NORM_EPS = 1e-6
SC_KSIZE = 3

V7X_LANES = 128
V7X_SUBLANES = 8
V7X_BF16_ROWS = 16
V7X_VMEM_LIMIT_BYTES = 48 * 1024 * 1024
HG_CHUNK = 16
RW_PAIR = 2
RW_CHUNK = 16

f32 = jnp.float32
bf16 = jnp.bfloat16


def _tile(n, pref, align=V7X_SUBLANES):
    if n <= pref:
        return n
    for d in range(pref, 0, -1):
        if n % d == 0 and d % align == 0:
            return d
    return n


def _seq_block(bsz, tlen):
    if tlen >= 256:
        return 1, _tile(tlen, 256)
    return _tile(bsz, max(1, 128 // tlen), align=1), tlen


def _params(*sem):
    return pltpu.CompilerParams(dimension_semantics=sem, vmem_limit_bytes=V7X_VMEM_LIMIT_BYTES)


def _sigmoid(x):
    return 1.0 / (1.0 + jnp.exp(-x))


def _dot(a, b):
    return jnp.dot(a, b, preferred_element_type=f32)


def _rms(x, g):
    return x * lax.rsqrt(jnp.mean(x * x, axis=-1, keepdims=True) + NORM_EPS) * g


def _group_ones(width, group):
    r = lax.broadcasted_iota(jnp.int32, (width, width), 0) // group
    c = lax.broadcasted_iota(jnp.int32, (width, width), 1) // group
    return jnp.where(r == c, 1.0, 0.0).astype(bf16)


def _group_sum(z, ones_bd):
    outs = []
    for c in range(z.shape[-1] // V7X_LANES):
        zc = z[:, c * V7X_LANES:(c + 1) * V7X_LANES]
        hi = zc.astype(bf16)
        lo = (zc - hi.astype(f32)).astype(bf16)
        outs.append(_dot(hi, ones_bd) + _dot(lo, ones_bd))
    return outs[0] if len(outs) == 1 else jnp.concatenate(outs, axis=-1)


def _threaded_state(args, in_specs, prev_out, out_index):
    if prev_out is None:
        return {}
    args.append(prev_out)
    in_specs.append(pl.BlockSpec(memory_space=pl.ANY))
    return {len(args) - 1: out_index}


def _lb_kernel(logit_ref, lb_ref):
    z = logit_ref[...]
    depth = z.shape[0]
    m = jnp.max(z, axis=0, keepdims=True)
    e = jnp.exp(z - m)
    p = e / jnp.sum(e, axis=0, keepdims=True)
    acc = jnp.zeros_like(p[0:1])
    for l in range(depth):
        acc = acc + p[l:l + 1]
        lb_ref[l:l + 1, :] = jnp.clip(acc - p[0:1], 0.0, 1.0)


def _lower_bounds(logits):
    return pl.pallas_call(
        _lb_kernel, out_shape=jax.ShapeDtypeStruct(logits.shape, f32), name="hg_lower_bounds",
    )(logits)


def _rmsnorm_kernel(x_ref, g_ref, o_ref):
    o_ref[...] = _rms(x_ref[...], g_ref[...]).astype(o_ref.dtype)


def _rmsnorm_cast(x, g):
    n, d = x.shape
    tm = _tile(n, 512)
    return pl.pallas_call(
        _rmsnorm_kernel,
        grid=(n // tm,),
        in_specs=[pl.BlockSpec((tm, d), lambda i: (i, 0)), pl.BlockSpec((1, d), lambda i: (0, 0))],
        out_specs=pl.BlockSpec((tm, d), lambda i: (i, 0)),
        out_shape=jax.ShapeDtypeStruct((n, d), bf16),
        compiler_params=_params("parallel"),
        name="rmsnorm_cast",
    )(x, g)


def _proj_kernel(a_ref, w_ref, o_ref, wbf_scr):
    @pl.when(pl.program_id(1) == 0)
    def _():
        wbf_scr[...] = w_ref[0].astype(bf16)

    o_ref[...] = _dot(a_ref[...], wbf_scr[...])


def _proj(h, w_stack, layer, off, width, name):
    n, k = h.shape
    tm = _tile(n, 1024)
    tn = _tile(width, 1024, align=V7X_LANES)
    w_spec = pl.BlockSpec((pl.Element(1), pl.Element(k), pl.Element(tn)),
                          lambda j, i: (layer, 0, pl.multiple_of(off + j * tn, V7X_LANES)))
    return pl.pallas_call(
        _proj_kernel,
        grid=(width // tn, n // tm),
        in_specs=[pl.BlockSpec((tm, k), lambda j, i: (i, 0)), w_spec],
        out_specs=pl.BlockSpec((tm, tn), lambda j, i: (i, j)),
        out_shape=jax.ShapeDtypeStruct((n, width), f32),
        scratch_shapes=[pltpu.VMEM((k, tn), bf16)],
        compiler_params=_params("arbitrary", "arbitrary"),
        name=name,
    )(h, w_stack)


def _hgrn_kernel(*refs, nb, tb, chunk, has_state, has_prev):
    q_ref, f_ref, i_ref, og_ref, lb_ref, gn_ref, rsel_ref = refs[:7]
    s0_ref = refs[7] if has_state else None
    o_ref, s_ref, st_scr = refs[7 + has_state + has_prev:]
    ti = pl.program_id(2)
    rows = nb * tb
    dk = HG_DK
    nchunk = rows // chunk

    @pl.when(ti == 0)
    def _():
        for b in range(nb):
            st_scr[b] = s0_ref[b, 0].T if has_state else jnp.zeros((dk, dk), f32)

    lb = lb_ref[...]
    hq = q_ref[...].reshape(rows, dk)
    qq = hq * _sigmoid(hq)
    f = lb + (1.0 - lb) * _sigmoid(f_ref[...].reshape(rows, dk))
    logf = jnp.log(jnp.maximum(f, HG_F_MIN))
    kk = 1.0 - f
    vv = i_ref[...].reshape(rows, dk)
    hog = og_ref[...].reshape(rows, dk)

    t_idx = lax.broadcasted_iota(jnp.int32, (rows, dk), 0) % chunk
    bcum = logf
    step = 1
    while step < chunk:
        bcum = bcum + jnp.where(t_idx >= step, pltpu.roll(bcum, step, 0), 0.0)
        step *= 2

    def chunk_row(a, s):
        a3 = a.reshape(nchunk, chunk, dk)
        return jnp.broadcast_to(a3[:, s:s + 1, :], (nchunk, chunk, dk)).reshape(rows, dk)

    b_last = chunk_row(bcum, chunk - 1)
    q_dec = (qq * jnp.exp(bcum)).astype(bf16)
    k_dec = (kk * jnp.exp(b_last - bcum)).astype(bf16)

    zs = []
    for s in range(chunk):
        dec = jnp.exp(jnp.minimum(bcum - chunk_row(bcum, s), 0.0))
        zs.append(jnp.where(t_idx >= s, qq * chunk_row(kk, s) * dec, 0.0).astype(bf16))
    x2 = jnp.concatenate(zs, axis=-1)
    a_rep = _dot(x2, rsel_ref[...])
    same_chunk = (lax.broadcasted_iota(jnp.int32, (rows, rows), 0) // chunk
                  == lax.broadcasted_iota(jnp.int32, (rows, rows), 1) // chunk)
    o_intra = _dot(jnp.where(same_chunk, a_rep, 0.0).astype(bf16), vv.astype(bf16))

    v_t = vv.T
    col_chunk = lax.broadcasted_iota(jnp.int32, (dk, rows), 1) // chunk
    lhs = jnp.concatenate([jnp.where(col_chunk == c, v_t, 0.0).astype(bf16) for c in range(nchunk)], axis=0)
    u_all = _dot(lhs, k_dec)

    outs = []
    for b in range(nb):
        st = st_scr[b]
        for g in range(tb // chunk):
            c = b * (tb // chunk) + g
            r0 = c * chunk
            inter = lax.dot_general(q_dec[r0:r0 + chunk], st.astype(bf16),
                                    (((1,), (1,)), ((), ())), preferred_element_type=f32)
            outs.append(inter + o_intra[r0:r0 + chunk])
            st = jnp.exp(bcum[r0 + chunk - 1:r0 + chunk]) * st + u_all[c * dk:(c + 1) * dk]
        st_scr[b] = st
    o = jnp.concatenate(outs, axis=0)
    o = o * lax.rsqrt(jnp.mean(o * o, axis=-1, keepdims=True) + NORM_EPS)
    o_ref[...] = (o * gn_ref[...] * (hog * _sigmoid(hog))).reshape(nb, tb, dk).astype(o_ref.dtype)

    @pl.when(ti == pl.num_programs(2) - 1)
    def _():
        for b in range(nb):
            s_ref[b, 0] = st_scr[b].T


def _hgrn(p_hg, lb, gn, s_in, s_prev, layer, depth, bsz, tlen):
    width = p_hg.shape[1] // 4
    heads = width // HG_DK
    chunk = min(HG_CHUNK, tlen)
    nb, tb = _seq_block(bsz, tlen)
    assert tb % chunk == 0
    p3 = p_hg.reshape(bsz, tlen, 4 * width)
    seg = lambda s: pl.BlockSpec((nb, tb, HG_DK), lambda bi, h, ti, s=s: (bi, ti, s * heads + h))
    vec = pl.BlockSpec((1, HG_DK), lambda bi, h, ti: (0, h))
    st_spec = pl.BlockSpec((None, nb, 1, HG_DK, HG_DK), lambda bi, h, ti: (layer, bi, h, 0, 0))
    has_state = s_in is not None
    rows = nb * tb
    rsel = (jnp.arange(chunk * HG_DK)[:, None] // HG_DK == jnp.arange(rows)[None, :] % chunk).astype(bf16)
    rsel_spec = pl.BlockSpec((chunk * HG_DK, rows), lambda bi, h, ti: (0, 0))
    in_specs = [seg(0), seg(1), seg(2), seg(3), vec, vec, rsel_spec] + ([st_spec] if has_state else [])
    args = [p3, p3, p3, p3, lb, gn, rsel] + ([s_in] if has_state else [])
    aliases = _threaded_state(args, in_specs, s_prev, 1)
    o, s_new = pl.pallas_call(
        functools.partial(_hgrn_kernel, nb=nb, tb=tb, chunk=chunk, has_state=has_state,
                          has_prev=s_prev is not None),
        grid=(bsz // nb, heads, tlen // tb),
        in_specs=in_specs,
        out_specs=[pl.BlockSpec((nb, tb, HG_DK), lambda bi, h, ti: (bi, ti, h)), st_spec],
        out_shape=[jax.ShapeDtypeStruct((bsz, tlen, width), bf16),
                   jax.ShapeDtypeStruct((depth, bsz, heads, HG_DK, HG_DK), f32)],
        scratch_shapes=[pltpu.VMEM((nb, HG_DK, HG_DK), f32)],
        input_output_aliases=aliases,
        compiler_params=_params("parallel", "parallel", "arbitrary"),
        name="hgrn2_scan",
    )(*args)
    return o.reshape(bsz * tlen, width), s_new


def _shift_rows(x, first, tb, by):
    rows = x.shape[0]
    t_idx = lax.broadcasted_iota(jnp.int32, (rows, 1), 0) % tb
    out = pltpu.roll(x, by, 0) if tb > by else x
    for j in range(by):
        out = jnp.where(t_idx == j, first[j], out)
    return out


def _bcast_rows(v3, nb, tb):
    w = v3.shape[-1]
    return jnp.broadcast_to(v3, (nb, tb, w)).reshape(nb * tb, w)


def _rw_prep_kernel(*refs, nb, tb, width, lora, has_state, has_prev):
    pm_ref, pl_ref = refs[:2]
    sh_ref = refs[2] if has_state else None
    mu_ref, w0_ref, w2_ref, a0_ref, a2_ref, g2_ref, kk_ref, ka_ref = refs[2 + has_state:10 + has_state]
    r_ref, w_ref, k_ref, v_ref, kn_ref, b_ref, g_ref, last_ref = refs[10 + has_state + has_prev:]
    ti = pl.program_id(1)
    rows = nb * tb
    tot = last_ref.shape[-1]
    x = jnp.concatenate([pm_ref[...], pl_ref[...]], axis=-1).reshape(rows, tot)

    @pl.when(ti == 0)
    def _():
        last_ref[...] = sh_ref[...] if has_state else jnp.zeros((nb, 1, tot), f32)

    prev = _shift_rows(x, [_bcast_rows(last_ref[...], nb, tb)], tb, 1)
    last_ref[...] = jnp.concatenate([pm_ref[:, tb - 1:tb, :], pl_ref[:, tb - 1:tb, :]], axis=-1)
    xs = x + (prev - x) * mu_ref[...]
    dl, al, gl = lora
    c = 3 * width
    r, kr, vr = xs[:, :width], xs[:, width:2 * width], xs[:, 2 * width:c]
    wd, ad, gd = xs[:, c:c + dl], xs[:, c + dl:c + dl + al], xs[:, c + dl + al:c + dl + al + gl]
    z = -(w0_ref[...] + _dot(jnp.tanh(wd).astype(bf16), w2_ref[...]))
    softplus = jnp.maximum(z, 0.0) + jnp.log(1.0 + jnp.exp(-jnp.abs(z)))
    w_log = -softplus - 0.5
    log_decay = -jnp.exp(w_log)
    a = _sigmoid(a0_ref[...] + _dot(ad.astype(bf16), a2_ref[...]))
    g = _dot(_sigmoid(gd).astype(bf16), g2_ref[...])
    kk = kr * kk_ref[...]
    ones_bd = _group_ones(V7X_LANES, RW_N)
    nrm = jnp.sqrt(_group_sum(kk * kk, ones_bd))
    kk = kk / jnp.maximum(nrm, 1e-12)
    k = kr * (1.0 + (a - 1.0) * ka_ref[...])
    shp = (nb, tb, width)
    r_ref[...] = r.reshape(shp)
    w_ref[...] = log_decay.reshape(shp)
    k_ref[...] = k.reshape(shp)
    v_ref[...] = vr.reshape(shp)
    kn_ref[...] = (-kk).reshape(shp)
    b_ref[...] = (kk * a).reshape(shp)
    g_ref[...] = g.reshape(shp)


def _rw_prep(p_main, p_lora, shift_in, shift_prev, layer, depth, wts, bsz, tlen):
    mu, w0, w2, a0, a2, g2, k_k, k_a = wts
    width = w0.shape[1]
    wl = p_lora.shape[1]
    tot = 3 * width + wl
    lora = (w2.shape[0], a2.shape[0], g2.shape[0])
    nb, tb = _seq_block(bsz, tlen)
    has_state = shift_in is not None
    blk = lambda w: pl.BlockSpec((nb, tb, w), lambda bi, ti: (bi, ti, 0))
    full = lambda arr: pl.BlockSpec(arr.shape, lambda bi, ti: (0,) * arr.ndim)
    last_spec = pl.BlockSpec((None, nb, 1, tot), lambda bi, ti: (layer, bi, 0, 0))
    small = [mu, w0, w2, a0, a2, g2, k_k, k_a]
    in_specs = [blk(3 * width), blk(wl)] + ([last_spec] if has_state else []) + [full(s) for s in small]
    args = ([p_main.reshape(bsz, tlen, 3 * width), p_lora.reshape(bsz, tlen, wl)]
            + ([shift_in] if has_state else []) + small)
    aliases = _threaded_state(args, in_specs, shift_prev, 7)
    seq = jax.ShapeDtypeStruct((bsz, tlen, width), f32)
    outs = pl.pallas_call(
        functools.partial(_rw_prep_kernel, nb=nb, tb=tb, width=width, lora=lora, has_state=has_state,
                          has_prev=shift_prev is not None),
        grid=(bsz // nb, tlen // tb),
        in_specs=in_specs,
        out_specs=[blk(width)] * 7 + [last_spec],
        out_shape=[seq] * 7 + [jax.ShapeDtypeStruct((depth, bsz, 1, tot), f32)],
        input_output_aliases=aliases,
        compiler_params=_params("parallel", "arbitrary"),
        name="rwkv7_prep",
    )(*args)
    return outs[:7], outs[7]


def _rw_state_io(s0_ref, s_ref, st_scr, combos, has_state, ti, last):
    n = RW_N

    @pl.when(ti == 0)
    def _():
        for c, (b, p) in enumerate(combos):
            if has_state:
                st_scr[c] = jnp.concatenate([s0_ref[b, RW_PAIR * p + h] for h in range(RW_PAIR)], axis=-1)
            else:
                st_scr[c] = jnp.zeros((n, RW_PAIR * n), f32)

    def finish():
        @pl.when(ti == last)
        def _():
            for c, (b, p) in enumerate(combos):
                s = st_scr[c]
                for h in range(RW_PAIR):
                    s_ref[b, RW_PAIR * p + h] = s[:, h * n:(h + 1) * n]

    return finish


def _rw_scan_kernel(*refs, nb, tb, pairs, has_state, has_prev):
    r_ref, w_ref, k_ref, v_ref, kn_ref, b_ref = refs[:6]
    s0_ref = refs[6] if has_state else None
    o_ref, s_ref, st_scr, lhs_a, lhs_v, lhs_o = refs[6 + has_state + has_prev:]
    ti = pl.program_id(1)
    n = RW_N
    lanes = RW_PAIR * n
    sub = V7X_SUBLANES
    combos = [(b, p) for b in range(nb) for p in range(pairs)]
    finish = _rw_state_io(s0_ref, s_ref, st_scr, combos, has_state, ti, pl.num_programs(1) - 1)

    ones_bd = _group_ones(lanes, n)
    diag = (lax.broadcasted_iota(jnp.int32, (n, lanes), 0)
            == lax.broadcasted_iota(jnp.int32, (n, lanes), 1) % n)

    def token_group(grp, carry):
        base = pl.multiple_of(grp * sub, sub)

        def row(ref, b, p, j, decay=False):
            tile = ref[b, pl.ds(base, sub), pl.ds(p * lanes, lanes)]
            tile = jnp.exp(tile) if decay else tile
            return jnp.broadcast_to(tile[j:j + 1], (n, lanes))

        o_rows = [[] for _ in combos]
        for j in range(sub):
            for c, (b, p) in enumerate(combos):
                lhs_a[c * n:(c + 1) * n, :] = (st_scr[c] * row(kn_ref, b, p, j)).astype(bf16)
                lhs_v[c * n:(c + 1) * n, :] = jnp.where(diag, row(v_ref, b, p, j), 0.0).astype(bf16)
            sa_all = _dot(lhs_a[...], ones_bd)
            vb_all = _dot(lhs_v[...], ones_bd)
            for c, (b, p) in enumerate(combos):
                s_new = (st_scr[c] * row(w_ref, b, p, j, decay=True)
                         + sa_all[c * n:(c + 1) * n] * row(b_ref, b, p, j)
                         + vb_all[c * n:(c + 1) * n] * row(k_ref, b, p, j))
                st_scr[c] = s_new
                lhs_o[c * n:(c + 1) * n, :] = (s_new * row(r_ref, b, p, j)).astype(bf16)
            o_all = _dot(lhs_o[...], ones_bd)
            for c, (b, p) in enumerate(combos):
                o_rows[c].append(jnp.sum(jnp.where(diag, o_all[c * n:(c + 1) * n], 0.0),
                                         axis=0, keepdims=True))
        for c, (b, p) in enumerate(combos):
            o_ref[b, pl.ds(base, sub), pl.ds(p * lanes, lanes)] = jnp.concatenate(o_rows[c], axis=0)
        return carry

    lax.fori_loop(0, tb // sub, token_group, 0)
    finish()


def _rw_chunk_kernel(*refs, nb, tb, pairs, has_state, has_prev):
    r_ref, w_ref, k_ref, v_ref, kn_ref, b_ref = refs[:6]
    s0_ref = refs[6] if has_state else None
    o_ref, s_ref, st_scr, coef_scr, x_scr = refs[6 + has_state + has_prev:]
    ti = pl.program_id(1)
    n = RW_N
    lanes = RW_PAIR * n
    ln = RW_CHUNK
    combos = [(b, p) for b in range(nb) for p in range(pairs)]
    finish = _rw_state_io(s0_ref, s_ref, st_scr, combos, has_state, ti, pl.num_programs(1) - 1)

    ones_bd = _group_ones(lanes, n)
    t_idx = lax.broadcasted_iota(jnp.int32, (ln, lanes), 0)
    lane_idx = lax.broadcasted_iota(jnp.int32, (ln, lanes), 1)
    row_i = lax.broadcasted_iota(jnp.int32, (4 * ln, 2 * ln), 0)
    col_i = lax.broadcasted_iota(jnp.int32, (4 * ln, 2 * ln), 1)
    same_head = (row_i % (2 * ln)) // ln == col_i // ln
    strict = jnp.where(row_i < 2 * ln, 1, 0)
    lower = jnp.where(same_head & (col_i % ln + strict <= row_i % ln), 1.0, 0.0)
    nt_dims = (((1,), (1,)), ((), ()))
    tn_dims = (((0,), (0,)), ((), ()))

    def by_head(x):
        head0 = lane_idx < n
        return jnp.concatenate([jnp.where(head0, x, 0.0), jnp.where(head0, 0.0, x)], axis=0)

    def stack_heads(x):
        return jnp.concatenate([x[:, :n], x[:, n:]], axis=0)

    def pack_heads(xs):
        return jnp.concatenate([xs[:ln], xs[ln:]], axis=-1)

    def chunk(ci, carry):
        base = pl.multiple_of(ci * ln, ln)
        pre = []
        for c, (b, p) in enumerate(combos):
            tile = lambda ref: ref[b, pl.ds(base, ln), pl.ds(p * lanes, lanes)]
            lw = tile(w_ref)
            cum = lw
            step = 1
            while step < ln:
                cum = cum + jnp.where(t_idx >= step, pltpu.roll(cum, step, 0), 0.0)
                step *= 2
            c_last = jnp.broadcast_to(cum[ln - 1:ln], (ln, lanes))
            inv_c = jnp.exp(-cum)
            to_end = jnp.exp(c_last - cum)
            kn, bb, kk, vv = tile(kn_ref), tile(b_ref), tile(k_ref), tile(v_ref)
            n_dec = kn * jnp.exp(cum - lw)
            b_dec = bb * inv_c
            k_dec = kk * inv_c
            r_dec = tile(r_ref) * jnp.exp(cum)
            lhs_nr = jnp.concatenate([by_head(n_dec), by_head(r_dec)], axis=0).astype(bf16)
            sc_b = lower * lax.dot_general(lhs_nr, by_head(b_dec).astype(bf16), nt_dims,
                                           preferred_element_type=f32)
            sc_k = lower * lax.dot_general(lhs_nr, by_head(k_dec).astype(bf16), nt_dims,
                                           preferred_element_type=f32)
            lhs_c = jnp.concatenate(
                [jnp.where(t_idx > s, n_dec * jnp.broadcast_to(b_dec[s:s + 1], (ln, lanes)), 0.0)
                 for s in range(ln - 1)], axis=0).astype(bf16)
            coef_scr[c] = _dot(lhs_c, ones_bd)
            rhs_end = jnp.concatenate([by_head(bb * to_end), by_head(kk * to_end)], axis=0).astype(bf16)
            pre.append((lhs_nr, sc_k[:2 * ln].astype(bf16), sc_b[2 * ln:].astype(bf16),
                        sc_k[2 * ln:].astype(bf16), stack_heads(vv).astype(bf16), rhs_end,
                        jnp.exp(cum[ln - 1:ln])))

        gs = [lax.dot_general(pre[c][0], st_scr[c].astype(bf16), nt_dims, preferred_element_type=f32)
              for c in range(len(combos))]
        for c in range(len(combos)):
            x_scr[c] = pack_heads(gs[c][:2 * ln] + _dot(pre[c][1], pre[c][4]))
        for s in range(ln - 1):
            for c in range(len(combos)):
                x = x_scr[c]
                x_scr[c] = x + coef_scr[c, s * ln:(s + 1) * ln, :] * jnp.broadcast_to(x[s:s + 1], (ln, lanes))
        for c, (b, p) in enumerate(combos):
            _, _, sc_b_out, sc_k_out, vs_bf, rhs_end, d_last = pre[c]
            xs_bf = stack_heads(x_scr[c]).astype(bf16)
            o = gs[c][2 * ln:] + _dot(sc_b_out, xs_bf) + _dot(sc_k_out, vs_bf)
            o_ref[b, pl.ds(base, ln), pl.ds(p * lanes, lanes)] = pack_heads(o)
            upd = lax.dot_general(jnp.concatenate([xs_bf, vs_bf], axis=0), rhs_end, tn_dims,
                                  preferred_element_type=f32)
            st_scr[c] = st_scr[c] * d_last + upd
        return carry

    lax.fori_loop(0, tb // ln, chunk, 0)
    finish()


def _rw_scan(seqs, s_in, s_prev, layer, depth, bsz, tlen):
    r, w, k, v, kn, bb = seqs
    width = r.shape[-1]
    heads = width // RW_N
    pairs = heads // RW_PAIR
    nb = _tile(bsz, 4, align=1)
    tb = _tile(tlen, 64)
    chunked = tb % RW_CHUNK == 0
    has_state = s_in is not None
    blk = pl.BlockSpec((nb, tb, width), lambda bi, ti: (bi, ti, 0))
    st_spec = pl.BlockSpec((None, nb, heads, RW_N, RW_N), lambda bi, ti: (layer, bi, 0, 0, 0))
    m = nb * pairs * RW_N
    lanes = RW_PAIR * RW_N
    in_specs = [blk] * 6 + ([st_spec] if has_state else [])
    args = [r, w, k, v, kn, bb] + ([s_in] if has_state else [])
    aliases = _threaded_state(args, in_specs, s_prev, 1)
    scratch = [pltpu.VMEM((nb * pairs, RW_N, lanes), f32)]
    if chunked:
        scratch += [pltpu.VMEM((nb * pairs, (RW_CHUNK - 1) * RW_CHUNK, lanes), f32),
                    pltpu.VMEM((nb * pairs, RW_CHUNK, lanes), f32)]
    else:
        scratch += [pltpu.VMEM((m, lanes), bf16)] * 3
    o, s_new = pl.pallas_call(
        functools.partial(_rw_chunk_kernel if chunked else _rw_scan_kernel, nb=nb, tb=tb, pairs=pairs,
                          has_state=has_state, has_prev=s_prev is not None),
        grid=(bsz // nb, tlen // tb),
        in_specs=in_specs,
        out_specs=[blk, st_spec],
        out_shape=[jax.ShapeDtypeStruct((bsz, tlen, width), f32),
                   jax.ShapeDtypeStruct((depth, bsz, heads, RW_N, RW_N), f32)],
        scratch_shapes=scratch,
        input_output_aliases=aliases,
        compiler_params=_params("parallel", "arbitrary"),
        name="rwkv7_chunk_scan" if chunked else "rwkv7_scan",
    )(*args)
    return o, s_new


def _rw_post_kernel(o_ref, r_ref, k_ref, v_ref, g_ref, lw_ref, lbias_ref, rk_ref, y_ref):
    ones_bd = _group_ones(V7X_LANES, RW_N)
    o = o_ref[...]
    mu = _group_sum(o, ones_bd) * (1.0 / RW_N)
    d = o - mu
    var = _group_sum(d * d, ones_bd) * (1.0 / RW_N)
    on = d * lax.rsqrt(var + RW_GN_EPS) * lw_ref[...] + lbias_ref[...]
    bonus = _group_sum(r_ref[...] * k_ref[...] * rk_ref[...], ones_bd) * v_ref[...]
    y_ref[...] = ((on + bonus) * g_ref[...]).astype(y_ref.dtype)


def _rw_post(o, r, k, v, g, ln_w, ln_b, r_k):
    n, width = o.shape
    tm = _tile(n, 256)
    blk = pl.BlockSpec((tm, width), lambda i: (i, 0))
    vec = pl.BlockSpec((1, width), lambda i: (0, 0))
    return pl.pallas_call(
        _rw_post_kernel,
        grid=(n // tm,),
        in_specs=[blk] * 5 + [vec] * 3,
        out_specs=blk,
        out_shape=jax.ShapeDtypeStruct((n, width), bf16),
        compiler_params=_params("parallel"),
        name="rwkv7_post",
    )(o, r, k, v, g, ln_w, ln_b, r_k)


def _conv_kernel(*refs, nb, tb, has_state, has_prev):
    sb_ref, sc_ref, sh_ref = refs[:3]
    st_ref = refs[3] if has_state else None
    cw_ref = refs[3 + has_state]
    y_ref, new_ref = refs[4 + has_state + has_prev:]
    ti = pl.program_id(2)
    rows = nb * tb
    w = sb_ref.shape[-1]
    hist = SC_KSIZE - 1

    @pl.when(ti == 0)
    def _():
        new_ref[...] = st_ref[...] if has_state else jnp.zeros((nb, hist, w), f32)

    u = (sc_ref[...] * sh_ref[...]).reshape(rows, w)
    carry = new_ref[...]
    conv = cw_ref[hist:hist + 1, :] * u
    for back in range(1, hist + 1):
        first = [_bcast_rows(carry[:, hist - back + j:hist - back + j + 1, :], nb, tb) for j in range(back)]
        conv = conv + cw_ref[hist - back:hist - back + 1, :] * _shift_rows(u, first, tb, back)
    y_ref[...] = (sb_ref[...].reshape(rows, w) * conv).reshape(nb, tb, w).astype(y_ref.dtype)
    new_ref[...] = u.reshape(nb, tb, w)[:, tb - hist:tb, :]


def _conv(p_sc, st_in, st_prev, layer, depth, conv_w, bsz, tlen):
    width = conv_w.shape[1]
    cw = min(width, 256)
    ncol = width // cw
    nb, tb = _seq_block(bsz, tlen)
    assert tb >= SC_KSIZE - 1
    has_state = st_in is not None
    p3 = p_sc.reshape(bsz, tlen, 3 * width)
    seg = lambda s: pl.BlockSpec((nb, tb, cw), lambda bi, cj, ti, s=s: (bi, ti, s * ncol + cj))
    st_spec = pl.BlockSpec((None, nb, SC_KSIZE - 1, cw), lambda bi, cj, ti: (layer, bi, 0, cj))
    in_specs = ([seg(0), seg(1), seg(2)] + ([st_spec] if has_state else [])
                + [pl.BlockSpec((SC_KSIZE, cw), lambda bi, cj, ti: (0, cj))])
    args = [p3, p3, p3] + ([st_in] if has_state else []) + [conv_w]
    aliases = _threaded_state(args, in_specs, st_prev, 1)
    y, new = pl.pallas_call(
        functools.partial(_conv_kernel, nb=nb, tb=tb, has_state=has_state, has_prev=st_prev is not None),
        grid=(bsz // nb, ncol, tlen // tb),
        in_specs=in_specs,
        out_specs=[pl.BlockSpec((nb, tb, cw), lambda bi, cj, ti: (bi, ti, cj)), st_spec],
        out_shape=[jax.ShapeDtypeStruct((bsz, tlen, width), bf16),
                   jax.ShapeDtypeStruct((depth, bsz, SC_KSIZE - 1, width), f32)],
        input_output_aliases=aliases,
        compiler_params=_params("parallel", "parallel", "arbitrary"),
        name="short_conv",
    )(*args)
    return y.reshape(bsz * tlen, width), new


def _merge_kernel(a_ref, b_ref, c_ref, ga_ref, gb_ref, gc_ref, pa_ref, pb_ref, pc_ref, m_ref, w_scr):
    @pl.when(pl.program_id(1) == 0)
    def _():
        for s, w_ref in enumerate((pa_ref, pb_ref, pc_ref)):
            w_scr[s] = w_ref[...].astype(bf16)

    m = (_sigmoid(ga_ref[...]) * _dot(a_ref[...], w_scr[0])
         + _sigmoid(gb_ref[...]) * _dot(b_ref[...], w_scr[1])
         + _sigmoid(gc_ref[...]) * _dot(c_ref[...], w_scr[2]))
    m_ref[...] = m.astype(m_ref.dtype)


def _merge(ya, yb, yc, p_gate, w_pa, w_pb, w_pc, layer):
    n = ya.shape[0]
    width, d = w_pa.shape[1:]
    assert w_pb.shape[1] == width and w_pc.shape[1] == width
    tm = _tile(n, 512)
    tn = _tile(d, 512, align=V7X_LANES)
    nj = d // tn
    act = pl.BlockSpec((tm, width), lambda j, i: (i, 0))
    gate = lambda s: pl.BlockSpec((tm, tn), lambda j, i, s=s: (i, s * nj + j))
    wt = pl.BlockSpec((None, width, tn), lambda j, i: (layer, 0, j))
    return pl.pallas_call(
        _merge_kernel,
        grid=(nj, n // tm),
        in_specs=[act, act, act, gate(0), gate(1), gate(2), wt, wt, wt],
        out_specs=pl.BlockSpec((tm, tn), lambda j, i: (i, j)),
        out_shape=jax.ShapeDtypeStruct((n, d), bf16),
        scratch_shapes=[pltpu.VMEM((3, width, tn), bf16)],
        compiler_params=_params("arbitrary", "arbitrary"),
        name="gated_merge",
    )(ya, yb, yc, p_gate, p_gate, p_gate, w_pa, w_pb, w_pc)


def _oproj_kernel(m_ref, wo_ref, x_ref, g1_ref, g2_ref, x1_ref, h2_ref):
    mix = _dot(m_ref[...], wo_ref[...])
    x1 = x_ref[...] + _rms(mix, g1_ref[...])
    x1_ref[...] = x1
    h2_ref[...] = _rms(x1, g2_ref[...]).astype(h2_ref.dtype)


def _oproj(m, w_o, layer, x, g_post_mix, g_pre_mlp):
    n, d = x.shape
    tm = _tile(n, 256)
    blk = pl.BlockSpec((tm, d), lambda i: (i, 0))
    vec = pl.BlockSpec((1, d), lambda i: (0, 0))
    return pl.pallas_call(
        _oproj_kernel,
        grid=(n // tm,),
        in_specs=[blk, pl.BlockSpec((None, d, d), lambda i: (layer, 0, 0)), blk, vec, vec],
        out_specs=[blk, blk],
        out_shape=[jax.ShapeDtypeStruct((n, d), f32), jax.ShapeDtypeStruct((n, d), bf16)],
        compiler_params=_params("parallel"),
        name="out_proj_norms",
    )(m, w_o, x, g_post_mix, g_pre_mlp)


def _mlp_kernel(h_ref, w1_ref, w2_ref, ff_ref):
    kf = pl.program_id(1)

    @pl.when(kf == 0)
    def _():
        ff_ref[...] = jnp.zeros_like(ff_ref)

    t = jnp.maximum(_dot(h_ref[...], w1_ref[...].astype(bf16)), 0.0)
    ff_ref[...] += _dot((t * t).astype(bf16), w2_ref[...].astype(bf16))


def _mlp(h2, w1, w2, layer):
    n, d = h2.shape
    dff = w1.shape[2]
    tm = _tile(n, 1024)
    fc = _tile(dff, 512, align=V7X_LANES)
    once = pl.Buffered(1)
    return pl.pallas_call(
        _mlp_kernel,
        grid=(n // tm, dff // fc),
        in_specs=[pl.BlockSpec((tm, d), lambda i, kf: (i, 0), pipeline_mode=once),
                  pl.BlockSpec((None, d, fc), lambda i, kf: (layer, 0, kf)),
                  pl.BlockSpec((None, fc, d), lambda i, kf: (layer, kf, 0))],
        out_specs=pl.BlockSpec((tm, d), lambda i, kf: (i, 0), pipeline_mode=once),
        out_shape=jax.ShapeDtypeStruct((n, d), f32),
        compiler_params=_params("parallel", "arbitrary"),
        name="relu2_mlp",
    )(h2, w1, w2)


def _residual_kernel(*refs, has_next):
    if has_next:
        x1_ref, ff_ref, g_ref, gn_ref, y_ref, h_ref = refs
    else:
        x1_ref, ff_ref, g_ref, y_ref = refs
    y = x1_ref[...] + _rms(ff_ref[...], g_ref[...])
    y_ref[...] = y
    if has_next:
        h_ref[...] = _rms(y, gn_ref[...]).astype(h_ref.dtype)


def _residual(x1, ff, g_post_mlp, g_next):
    n, d = x1.shape
    tm = _tile(n, 512)
    blk = pl.BlockSpec((tm, d), lambda i: (i, 0))
    vec = pl.BlockSpec((1, d), lambda i: (0, 0))
    has_next = g_next is not None
    outs = pl.pallas_call(
        functools.partial(_residual_kernel, has_next=has_next),
        grid=(n // tm,),
        in_specs=[blk, blk, vec] + ([vec] if has_next else []),
        out_specs=[blk] + ([blk] if has_next else []),
        out_shape=[jax.ShapeDtypeStruct((n, d), f32)] + ([jax.ShapeDtypeStruct((n, d), bf16)] if has_next else []),
        compiler_params=_params("parallel"),
        name="mlp_residual_norm",
    )(*([x1, ff, g_post_mlp] + ([g_next] if has_next else [])))
    return (outs[0], outs[1]) if has_next else (outs[0], None)


def _layer(x, h, bsz, tlen, states_in, states_prev, layer, depth, wts, lw, g_next):
    hg_in, rw_in, shift_in, sc_in = states_in
    hg_prev, rw_prev, shift_prev, sc_prev = states_prev
    w_in, offs = wts["w_in"], wts["offs"]
    p_hg = _proj(h, w_in, layer, offs[0], offs[1] - offs[0], "in_proj_hgrn")
    p_rkv = _proj(h, w_in, layer, offs[1], offs[2] - offs[1], "in_proj_rwkv")
    p_lora = _proj(h, w_in, layer, offs[2], offs[3] - offs[2], "in_proj_rwkv_lora")
    p_sc = _proj(h, w_in, layer, offs[3], offs[4] - offs[3], "in_proj_conv")
    p_gate = _proj(h, w_in, layer, offs[4], offs[5] - offs[4], "in_proj_gate")

    ya, hg_new = _hgrn(p_hg, lw["lb"], lw["hg_norm"], hg_in, hg_prev, layer, depth, bsz, tlen)

    seqs, shift_new = _rw_prep(p_rkv, p_lora, shift_in, shift_prev, layer, depth, lw["rw_prep"], bsz, tlen)
    r, w, k, v, kn, bb, g = seqs
    o_rw, rw_new = _rw_scan((r, w, k, v, kn, bb), rw_in, rw_prev, layer, depth, bsz, tlen)
    flat = lambda t: t.reshape(bsz * tlen, t.shape[-1])
    yb = _rw_post(flat(o_rw), flat(r), flat(k), flat(v), flat(g), lw["rw_ln_w"], lw["rw_ln_b"], lw["rw_r_k"])

    yc, sc_new = _conv(p_sc, sc_in, sc_prev, layer, depth, lw["sc_conv_w"], bsz, tlen)

    m = _merge(ya, yb, yc, p_gate, wts["w_pa"], wts["w_pb"], wts["w_pc"], layer)
    x1, h2 = _oproj(m, wts["w_o"], layer, x, lw["g_post_mix"], lw["g_pre_mlp"])
    ff = _mlp(h2, wts["w_ff1"], wts["w_ff2"], layer)
    y, h_next = _residual(x1, ff, lw["g_post_mlp"], g_next)
    return y, h_next, (hg_new, rw_new, shift_new, sc_new)


def kernel(x_prompt, x_sample, state_hgrn, state_rwkv, state_rwkv_shift, state_conv, norm_pre_mix, norm_post_mix, norm_pre_mlp, norm_post_mlp, w_in, hg_lb_logits, hg_norm, w_pa, rw_mu, rw_w0, rw_w2, rw_a0, rw_a2, rw_g2, rw_k_k, rw_k_a, rw_r_k, rw_ln_w, rw_ln_b, w_pb, sc_conv_w, w_pc, w_o, w_ff1, w_ff2):
    depth = w_in.shape[0]
    d_model = x_prompt.shape[-1]
    hg_width = hg_norm.shape[1]
    rw_width = rw_w0.shape[1]
    rw_shift_width = rw_mu.shape[1]
    sc_width = sc_conv_w.shape[2]
    off_rw = 4 * hg_width
    off_lora = off_rw + 3 * rw_width
    off_sc = off_rw + rw_shift_width
    off_gate = off_sc + 3 * sc_width
    assert w_in.shape[2] == off_gate + 3 * d_model

    lb_all = _lower_bounds(hg_lb_logits.astype(f32))
    row = lambda a, l: a[l].reshape(1, -1).astype(f32)
    wts = {"w_in": w_in.astype(f32), "offs": (0, off_rw, off_lora, off_sc, off_gate, w_in.shape[2]),
           "w_pa": w_pa.astype(f32), "w_pb": w_pb.astype(f32), "w_pc": w_pc.astype(f32),
           "w_o": w_o.astype(bf16), "w_ff1": w_ff1.astype(f32), "w_ff2": w_ff2.astype(f32)}

    bp, tp, _ = x_prompt.shape
    bs, ts, _ = x_sample.shape
    yp = x_prompt.reshape(bp * tp, d_model).astype(f32)
    ys = x_sample.reshape(bs * ts, d_model).astype(f32)
    hp = _rmsnorm_cast(yp, row(norm_pre_mix, 0))
    hs = _rmsnorm_cast(ys, row(norm_pre_mix, 0))
    sample_in = (state_hgrn.astype(f32), state_rwkv.astype(f32),
                 state_rwkv_shift.astype(f32).reshape(depth, bs, 1, rw_shift_width), state_conv.astype(f32))
    new_p = new_s = (None, None, None, None)
    for l in range(depth):
        lw = {
            "g_post_mix": row(norm_post_mix, l), "g_pre_mlp": row(norm_pre_mlp, l),
            "g_post_mlp": row(norm_post_mlp, l),
            "lb": lb_all[l:l + 1], "hg_norm": row(hg_norm, l),
            "rw_prep": (row(rw_mu, l), row(rw_w0, l), rw_w2[l].astype(bf16), row(rw_a0, l), rw_a2[l].astype(bf16),
                        rw_g2[l].astype(bf16), row(rw_k_k, l), row(rw_k_a, l)),
            "rw_ln_w": row(rw_ln_w, l), "rw_ln_b": row(rw_ln_b, l), "rw_r_k": row(rw_r_k, l),
            "sc_conv_w": sc_conv_w[l].astype(f32),
        }
        g_next = row(norm_pre_mix, l + 1) if l + 1 < depth else None
        yp, hp, new_p = _layer(yp, hp, bp, tp, (None, None, None, None), new_p, l, depth, wts, lw, g_next)
        ys, hs, new_s = _layer(ys, hs, bs, ts, sample_in, new_s, l, depth, wts, lw, g_next)
    shift = lambda s, b: s.reshape(depth, b, rw_shift_width)
    return (yp.reshape(bp, tp, d_model), ys.reshape(bs, ts, d_model),
            new_p[0], new_p[1], shift(new_p[2], bp), new_p[3],
            new_s[0], new_s[1], shift(new_s[2], bs), new_s[3])
```

```python
import functools

import jax
import jax.numpy as jnp
from jax import lax
from jax.experimental import pallas as pl
from jax.experimental.pallas import tpu as pltpu

HG_DK = 128
RW_N = 64
HG_F_MIN = 1e-30
RW_GN_EPS = 64e-5
NORM_EPS = 1e-6
SC_KSIZE = 3

V7X_LANES = 128
V7X_SUBLANES = 8
V7X_BF16_ROWS = 16
V7X_VMEM_LIMIT_BYTES = 48 * 1024 * 1024
HG_CHUNK = 16
HG_HEADS_PER_STEP = 4
RW_PAIR = 2
RW_CHUNK = 16

f32 = jnp.float32
bf16 = jnp.bfloat16


def _tile(n, pref, align=V7X_SUBLANES):
    if n <= pref:
        return n
    for d in range(pref, 0, -1):
        if n % d == 0 and d % align == 0:
            return d
    return n


def _seq_block(bsz, tlen):
    if tlen >= 256:
        return 1, _tile(tlen, 256)
    return _tile(bsz, max(1, 128 // tlen), align=1), tlen


def _params(*sem):
    return pltpu.CompilerParams(dimension_semantics=sem, vmem_limit_bytes=V7X_VMEM_LIMIT_BYTES)


def _sigmoid(x):
    return 1.0 / (1.0 + jnp.exp(-x))


def _dot(a, b):
    return jnp.dot(a, b, preferred_element_type=f32)


def _rms(x, g):
    return x * lax.rsqrt(jnp.mean(x * x, axis=-1, keepdims=True) + NORM_EPS) * g


def _group_ones(width, group):
    r = lax.broadcasted_iota(jnp.int32, (width, width), 0) // group
    c = lax.broadcasted_iota(jnp.int32, (width, width), 1) // group
    return jnp.where(r == c, 1.0, 0.0).astype(bf16)


def _group_sum(z, ones_bd):
    outs = []
    for c in range(z.shape[-1] // V7X_LANES):
        zc = z[:, c * V7X_LANES:(c + 1) * V7X_LANES]
        hi = zc.astype(bf16)
        lo = (zc - hi.astype(f32)).astype(bf16)
        outs.append(_dot(hi, ones_bd) + _dot(lo, ones_bd))
    return outs[0] if len(outs) == 1 else jnp.concatenate(outs, axis=-1)


def _threaded_state(args, in_specs, prev_out, out_index):
    if prev_out is None:
        return {}
    args.append(prev_out)
    in_specs.append(pl.BlockSpec(memory_space=pl.ANY))
    return {len(args) - 1: out_index}


def _lb_kernel(logit_ref, lb_ref):
    z = logit_ref[...]
    depth = z.shape[0]
    m = jnp.max(z, axis=0, keepdims=True)
    e = jnp.exp(z - m)
    p = e / jnp.sum(e, axis=0, keepdims=True)
    acc = jnp.zeros_like(p[0:1])
    for l in range(depth):
        acc = acc + p[l:l + 1]
        lb_ref[l:l + 1, :] = jnp.clip(acc - p[0:1], 0.0, 1.0)


def _lower_bounds(logits):
    return pl.pallas_call(
        _lb_kernel, out_shape=jax.ShapeDtypeStruct(logits.shape, f32), name="hg_lower_bounds",
    )(logits)


def _rmsnorm_kernel(x_ref, g_ref, o_ref):
    o_ref[...] = _rms(x_ref[...], g_ref[...]).astype(o_ref.dtype)


def _rmsnorm_cast(x, g):
    n, d = x.shape
    tm = _tile(n, 512)
    return pl.pallas_call(
        _rmsnorm_kernel,
        grid=(n // tm,),
        in_specs=[pl.BlockSpec((tm, d), lambda i: (i, 0)), pl.BlockSpec((1, d), lambda i: (0, 0))],
        out_specs=pl.BlockSpec((tm, d), lambda i: (i, 0)),
        out_shape=jax.ShapeDtypeStruct((n, d), bf16),
        compiler_params=_params("parallel"),
        name="rmsnorm_cast",
    )(x, g)


def _proj_kernel(a_ref, w_ref, o_ref, wbf_scr):
    @pl.when(pl.program_id(1) == 0)
    def _():
        wbf_scr[...] = w_ref[0].astype(bf16)

    o_ref[...] = _dot(a_ref[...], wbf_scr[...]).astype(o_ref.dtype)


def _proj(h, w_stack, layer, off, width, name, out_dtype=f32):
    n, k = h.shape
    tm = _tile(n, 1024)
    tn = _tile(width, 1024, align=V7X_LANES)
    w_spec = pl.BlockSpec((pl.Element(1), pl.Element(k), pl.Element(tn)),
                          lambda j, i: (layer, 0, pl.multiple_of(off + j * tn, V7X_LANES)))
    return pl.pallas_call(
        _proj_kernel,
        grid=(width // tn, n // tm),
        in_specs=[pl.BlockSpec((tm, k), lambda j, i: (i, 0)), w_spec],
        out_specs=pl.BlockSpec((tm, tn), lambda j, i: (i, j)),
        out_shape=jax.ShapeDtypeStruct((n, width), out_dtype),
        scratch_shapes=[pltpu.VMEM((k, tn), bf16)],
        compiler_params=_params("arbitrary", "arbitrary"),
        name=name,
    )(h, w_stack)


def _hgrn_kernel(*refs, nb, tb, hb, chunk, has_state, has_prev):
    q_ref, f_ref, i_ref, og_ref, lb_ref, gn_ref, rsel_ref = refs[:7]
    s0_ref = refs[7] if has_state else None
    o_ref, s_ref, st_scr = refs[7 + has_state + has_prev:]
    ti = pl.program_id(2)
    rows = nb * tb
    dk = HG_DK
    nchunk = rows // chunk
    heads = range(hb)

    @pl.when(ti == 0)
    def _():
        for b in range(nb):
            for h in heads:
                st_scr[b, h] = s0_ref[b, h].T if has_state else jnp.zeros((dk, dk), f32)

    t_idx = lax.broadcasted_iota(jnp.int32, (rows, dk), 0) % chunk
    same_chunk = (lax.broadcasted_iota(jnp.int32, (rows, rows), 0) // chunk
                  == lax.broadcasted_iota(jnp.int32, (rows, rows), 1) // chunk)
    col_chunk = lax.broadcasted_iota(jnp.int32, (dk, rows), 1) // chunk

    def chunk_row(a, s):
        a3 = a.reshape(nchunk, chunk, dk)
        return jnp.broadcast_to(a3[:, s:s + 1, :], (nchunk, chunk, dk)).reshape(rows, dk)

    def head_cols(ref, h):
        return ref[:, :, h * dk:(h + 1) * dk].reshape(rows, dk)

    bcums, q_decs, k_decs, vvs, x2s = [], [], [], [], []
    for h in heads:
        lb = lb_ref[:, h * dk:(h + 1) * dk]
        hq = head_cols(q_ref, h)
        qq = hq * _sigmoid(hq)
        f = lb + (1.0 - lb) * _sigmoid(head_cols(f_ref, h))
        kk = 1.0 - f
        bcum = jnp.log(jnp.maximum(f, HG_F_MIN))
        step = 1
        while step < chunk:
            bcum = bcum + jnp.where(t_idx >= step, pltpu.roll(bcum, step, 0), 0.0)
            step *= 2
        b_last = chunk_row(bcum, chunk - 1)
        zs = []
        for s in range(chunk):
            dec = jnp.exp(jnp.minimum(bcum - chunk_row(bcum, s), 0.0))
            zs.append(jnp.where(t_idx >= s, qq * chunk_row(kk, s) * dec, 0.0).astype(bf16))
        bcums.append(bcum)
        q_decs.append((qq * jnp.exp(bcum)).astype(bf16))
        k_decs.append((kk * jnp.exp(b_last - bcum)).astype(bf16))
        vvs.append(head_cols(i_ref, h))
        x2s.append(jnp.concatenate(zs, axis=-1))

    a_reps = [_dot(x2s[h], rsel_ref[...]) for h in heads]
    o_intras = [_dot(jnp.where(same_chunk, a_reps[h], 0.0).astype(bf16), vvs[h].astype(bf16)) for h in heads]
    u_alls = []
    for h in heads:
        v_t = vvs[h].T
        lhs = jnp.concatenate([jnp.where(col_chunk == c, v_t, 0.0).astype(bf16) for c in range(nchunk)], axis=0)
        u_alls.append(_dot(lhs, k_decs[h]))

    outs = [[] for _ in heads]
    for b in range(nb):
        sts = [st_scr[b, h] for h in heads]
        for g in range(tb // chunk):
            c = b * (tb // chunk) + g
            r0 = c * chunk
            for h in heads:
                inter = lax.dot_general(q_decs[h][r0:r0 + chunk], sts[h].astype(bf16),
                                        (((1,), (1,)), ((), ())), preferred_element_type=f32)
                outs[h].append(inter + o_intras[h][r0:r0 + chunk])
                sts[h] = jnp.exp(bcums[h][r0 + chunk - 1:r0 + chunk]) * sts[h] + u_alls[h][c * dk:(c + 1) * dk]
        for h in heads:
            st_scr[b, h] = sts[h]
    for h in heads:
        o = jnp.concatenate(outs[h], axis=0)
        o = o * lax.rsqrt(jnp.mean(o * o, axis=-1, keepdims=True) + NORM_EPS)
        hog = head_cols(og_ref, h)
        o = o * gn_ref[:, h * dk:(h + 1) * dk] * (hog * _sigmoid(hog))
        o_ref[:, :, h * dk:(h + 1) * dk] = o.reshape(nb, tb, dk).astype(o_ref.dtype)

    @pl.when(ti == pl.num_programs(2) - 1)
    def _():
        for b in range(nb):
            for h in heads:
                s_ref[b, h] = st_scr[b, h].T


def _hgrn(p_hg, lb, gn, s_in, s_prev, layer, depth, bsz, tlen):
    width = p_hg.shape[1] // 4
    heads = width // HG_DK
    chunk = min(HG_CHUNK, tlen)
    nb, tb = _seq_block(bsz, tlen)
    assert tb % chunk == 0
    p3 = p_hg.reshape(bsz, tlen, 4 * width)
    hb = HG_HEADS_PER_STEP if heads % HG_HEADS_PER_STEP == 0 else 1
    hgroups = heads // hb
    wb = hb * HG_DK
    seg = lambda s: pl.BlockSpec((nb, tb, wb), lambda bi, h, ti, s=s: (bi, ti, s * hgroups + h))
    vec = pl.BlockSpec((1, wb), lambda bi, h, ti: (0, h))
    st_spec = pl.BlockSpec((None, nb, hb, HG_DK, HG_DK), lambda bi, h, ti: (layer, bi, h, 0, 0))
    has_state = s_in is not None
    rows = nb * tb
    rsel = (jnp.arange(chunk * HG_DK)[:, None] // HG_DK == jnp.arange(rows)[None, :] % chunk).astype(bf16)
    rsel_spec = pl.BlockSpec((chunk * HG_DK, rows), lambda bi, h, ti: (0, 0))
    in_specs = [seg(0), seg(1), seg(2), seg(3), vec, vec, rsel_spec] + ([st_spec] if has_state else [])
    args = [p3, p3, p3, p3, lb, gn, rsel] + ([s_in] if has_state else [])
    aliases = _threaded_state(args, in_specs, s_prev, 1)
    o, s_new = pl.pallas_call(
        functools.partial(_hgrn_kernel, nb=nb, tb=tb, hb=hb, chunk=chunk, has_state=has_state,
                          has_prev=s_prev is not None),
        grid=(bsz // nb, hgroups, tlen // tb),
        in_specs=in_specs,
        out_specs=[pl.BlockSpec((nb, tb, wb), lambda bi, h, ti: (bi, ti, h)), st_spec],
        out_shape=[jax.ShapeDtypeStruct((bsz, tlen, width), bf16),
                   jax.ShapeDtypeStruct((depth, bsz, heads, HG_DK, HG_DK), f32)],
        scratch_shapes=[pltpu.VMEM((nb, hb, HG_DK, HG_DK), f32)],
        input_output_aliases=aliases,
        compiler_params=_params("parallel", "parallel", "arbitrary"),
        name="hgrn2_scan",
    )(*args)
    return o.reshape(bsz * tlen, width), s_new


def _shift_rows(x, first, tb, by):
    rows = x.shape[0]
    t_idx = lax.broadcasted_iota(jnp.int32, (rows, 1), 0) % tb
    out = pltpu.roll(x, by, 0) if tb > by else x
    for j in range(by):
        out = jnp.where(t_idx == j, first[j], out)
    return out


def _bcast_rows(v3, nb, tb):
    w = v3.shape[-1]
    return jnp.broadcast_to(v3, (nb, tb, w)).reshape(nb * tb, w)


def _rw_prep_kernel(*refs, nb, tb, width, lora, has_state, has_prev):
    pm_ref, pl_ref = refs[:2]
    sh_ref = refs[2] if has_state else None
    mu_ref, w0_ref, w2_ref, a0_ref, a2_ref, g2_ref, kk_ref, ka_ref = refs[2 + has_state:10 + has_state]
    r_ref, w_ref, k_ref, v_ref, kn_ref, b_ref, g_ref, last_ref = refs[10 + has_state + has_prev:]
    ti = pl.program_id(1)
    rows = nb * tb
    tot = last_ref.shape[-1]
    x = jnp.concatenate([pm_ref[...], pl_ref[...]], axis=-1).reshape(rows, tot)

    @pl.when(ti == 0)
    def _():
        last_ref[...] = sh_ref[...] if has_state else jnp.zeros((nb, 1, tot), f32)

    prev = _shift_rows(x, [_bcast_rows(last_ref[...], nb, tb)], tb, 1)
    last_ref[...] = jnp.concatenate([pm_ref[:, tb - 1:tb, :], pl_ref[:, tb - 1:tb, :]], axis=-1)
    xs = x + (prev - x) * mu_ref[...]
    dl, al, gl = lora
    c = 3 * width
    r, kr, vr = xs[:, :width], xs[:, width:2 * width], xs[:, 2 * width:c]
    wd, ad, gd = xs[:, c:c + dl], xs[:, c + dl:c + dl + al], xs[:, c + dl + al:c + dl + al + gl]
    z = -(w0_ref[...] + _dot(jnp.tanh(wd).astype(bf16), w2_ref[...]))
    softplus = jnp.maximum(z, 0.0) + jnp.log(1.0 + jnp.exp(-jnp.abs(z)))
    w_log = -softplus - 0.5
    log_decay = -jnp.exp(w_log)
    a = _sigmoid(a0_ref[...] + _dot(ad.astype(bf16), a2_ref[...]))
    g = _dot(_sigmoid(gd).astype(bf16), g2_ref[...])
    kk = kr * kk_ref[...]
    ones_bd = _group_ones(V7X_LANES, RW_N)
    nrm = jnp.sqrt(_group_sum(kk * kk, ones_bd))
    kk = kk / jnp.maximum(nrm, 1e-12)
    k = kr * (1.0 + (a - 1.0) * ka_ref[...])
    shp = (nb, tb, width)
    r_ref[...] = r.reshape(shp).astype(r_ref.dtype)
    w_ref[...] = log_decay.reshape(shp)
    k_ref[...] = k.reshape(shp).astype(k_ref.dtype)
    v_ref[...] = vr.reshape(shp).astype(v_ref.dtype)
    kn_ref[...] = (-kk).reshape(shp).astype(kn_ref.dtype)
    b_ref[...] = (kk * a).reshape(shp).astype(b_ref.dtype)
    g_ref[...] = g.reshape(shp).astype(g_ref.dtype)


def _rw_prep(p_main, p_lora, shift_in, shift_prev, layer, depth, wts, bsz, tlen, seq_dtype):
    mu, w0, w2, a0, a2, g2, k_k, k_a = wts
    width = w0.shape[1]
    wl = p_lora.shape[1]
    tot = 3 * width + wl
    lora = (w2.shape[0], a2.shape[0], g2.shape[0])
    nb, tb = _seq_block(bsz, tlen)
    has_state = shift_in is not None
    blk = lambda w: pl.BlockSpec((nb, tb, w), lambda bi, ti: (bi, ti, 0))
    full = lambda arr: pl.BlockSpec(arr.shape, lambda bi, ti: (0,) * arr.ndim)
    last_spec = pl.BlockSpec((None, nb, 1, tot), lambda bi, ti: (layer, bi, 0, 0))
    small = [mu, w0, w2, a0, a2, g2, k_k, k_a]
    in_specs = [blk(3 * width), blk(wl)] + ([last_spec] if has_state else []) + [full(s) for s in small]
    args = ([p_main.reshape(bsz, tlen, 3 * width), p_lora.reshape(bsz, tlen, wl)]
            + ([shift_in] if has_state else []) + small)
    aliases = _threaded_state(args, in_specs, shift_prev, 7)
    seq = lambda i: jax.ShapeDtypeStruct((bsz, tlen, width), f32 if i == 1 else seq_dtype)
    outs = pl.pallas_call(
        functools.partial(_rw_prep_kernel, nb=nb, tb=tb, width=width, lora=lora, has_state=has_state,
                          has_prev=shift_prev is not None),
        grid=(bsz // nb, tlen // tb),
        in_specs=in_specs,
        out_specs=[blk(width)] * 7 + [last_spec],
        out_shape=[seq(i) for i in range(7)] + [jax.ShapeDtypeStruct((depth, bsz, 1, tot), f32)],
        input_output_aliases=aliases,
        compiler_params=_params("parallel", "arbitrary"),
        name="rwkv7_prep",
    )(*args)
    return outs[:7], outs[7]


def _rw_state_io(s0_ref, s_ref, st_scr, combos, has_state, ti, last):
    n = RW_N

    @pl.when(ti == 0)
    def _():
        for c, (b, p) in enumerate(combos):
            if has_state:
                st_scr[c] = jnp.concatenate([s0_ref[b, RW_PAIR * p + h] for h in range(RW_PAIR)], axis=-1)
            else:
                st_scr[c] = jnp.zeros((n, RW_PAIR * n), f32)

    def finish():
        @pl.when(ti == last)
        def _():
            for c, (b, p) in enumerate(combos):
                s = st_scr[c]
                for h in range(RW_PAIR):
                    s_ref[b, RW_PAIR * p + h] = s[:, h * n:(h + 1) * n]

    return finish


def _rw_scan_kernel(*refs, nb, tb, pairs, has_state, has_prev):
    r_ref, w_ref, k_ref, v_ref, kn_ref, b_ref = refs[:6]
    s0_ref = refs[6] if has_state else None
    o_ref, s_ref, st_scr, lhs_a, lhs_v, lhs_o = refs[6 + has_state + has_prev:]
    ti = pl.program_id(1)
    n = RW_N
    lanes = RW_PAIR * n
    sub = V7X_SUBLANES
    combos = [(b, p) for b in range(nb) for p in range(pairs)]
    finish = _rw_state_io(s0_ref, s_ref, st_scr, combos, has_state, ti, pl.num_programs(1) - 1)

    ones_bd = _group_ones(lanes, n)
    diag = (lax.broadcasted_iota(jnp.int32, (n, lanes), 0)
            == lax.broadcasted_iota(jnp.int32, (n, lanes), 1) % n)

    def token_group(grp, carry):
        base = pl.multiple_of(grp * sub, sub)

        def row(ref, b, p, j, decay=False):
            tile = ref[b, pl.ds(base, sub), pl.ds(p * lanes, lanes)]
            tile = jnp.exp(tile) if decay else tile
            return jnp.broadcast_to(tile[j:j + 1], (n, lanes))

        o_rows = [[] for _ in combos]
        for j in range(sub):
            for c, (b, p) in enumerate(combos):
                lhs_a[c * n:(c + 1) * n, :] = (st_scr[c] * row(kn_ref, b, p, j)).astype(bf16)
                lhs_v[c * n:(c + 1) * n, :] = jnp.where(diag, row(v_ref, b, p, j), 0.0).astype(bf16)
            sa_all = _dot(lhs_a[...], ones_bd)
            vb_all = _dot(lhs_v[...], ones_bd)
            for c, (b, p) in enumerate(combos):
                s_new = (st_scr[c] * row(w_ref, b, p, j, decay=True)
                         + sa_all[c * n:(c + 1) * n] * row(b_ref, b, p, j)
                         + vb_all[c * n:(c + 1) * n] * row(k_ref, b, p, j))
                st_scr[c] = s_new
                lhs_o[c * n:(c + 1) * n, :] = (s_new * row(r_ref, b, p, j)).astype(bf16)
            o_all = _dot(lhs_o[...], ones_bd)
            for c, (b, p) in enumerate(combos):
                o_rows[c].append(jnp.sum(jnp.where(diag, o_all[c * n:(c + 1) * n], 0.0),
                                         axis=0, keepdims=True))
        for c, (b, p) in enumerate(combos):
            o_ref[b, pl.ds(base, sub), pl.ds(p * lanes, lanes)] = jnp.concatenate(o_rows[c], axis=0)
        return carry

    lax.fori_loop(0, tb // sub, token_group, 0)
    finish()


def _rw_chunk_kernel(*refs, nb, tb, pairs, has_state, has_prev):
    r_ref, w_ref, k_ref, v_ref, kn_ref, b_ref = refs[:6]
    s0_ref = refs[6] if has_state else None
    o_ref, s_ref, st_scr, coef_scr, x_scr = refs[6 + has_state + has_prev:]
    ti = pl.program_id(1)
    n = RW_N
    lanes = RW_PAIR * n
    ln = RW_CHUNK
    combos = [(b, p) for b in range(nb) for p in range(pairs)]
    finish = _rw_state_io(s0_ref, s_ref, st_scr, combos, has_state, ti, pl.num_programs(1) - 1)

    ones_bd = _group_ones(lanes, n)
    t_idx = lax.broadcasted_iota(jnp.int32, (ln, lanes), 0)
    lane_idx = lax.broadcasted_iota(jnp.int32, (ln, lanes), 1)
    row_i = lax.broadcasted_iota(jnp.int32, (4 * ln, 2 * ln), 0)
    col_i = lax.broadcasted_iota(jnp.int32, (4 * ln, 2 * ln), 1)
    same_head = (row_i % (2 * ln)) // ln == col_i // ln
    strict = jnp.where(row_i < 2 * ln, 1, 0)
    lower = jnp.where(same_head & (col_i % ln + strict <= row_i % ln), 1.0, 0.0)
    nt_dims = (((1,), (1,)), ((), ()))
    tn_dims = (((0,), (0,)), ((), ()))

    def by_head(x):
        head0 = lane_idx < n
        return jnp.concatenate([jnp.where(head0, x, 0.0), jnp.where(head0, 0.0, x)], axis=0)

    def stack_heads(x):
        return jnp.concatenate([x[:, :n], x[:, n:]], axis=0)

    def pack_heads(xs):
        return jnp.concatenate([xs[:ln], xs[ln:]], axis=-1)

    def chunk(ci, carry):
        base = pl.multiple_of(ci * ln, ln)
        pre = []
        for c, (b, p) in enumerate(combos):
            tile = lambda ref: ref[b, pl.ds(base, ln), pl.ds(p * lanes, lanes)].astype(f32)
            lw = tile(w_ref)
            cum = lw
            step = 1
            while step < ln:
                cum = cum + jnp.where(t_idx >= step, pltpu.roll(cum, step, 0), 0.0)
                step *= 2
            c_last = jnp.broadcast_to(cum[ln - 1:ln], (ln, lanes))
            inv_c = jnp.exp(-cum)
            to_end = jnp.exp(c_last - cum)
            kn, bb, kk, vv = tile(kn_ref), tile(b_ref), tile(k_ref), tile(v_ref)
            n_dec = kn * jnp.exp(cum - lw)
            b_dec = bb * inv_c
            k_dec = kk * inv_c
            r_dec = tile(r_ref) * jnp.exp(cum)
            lhs_nr = jnp.concatenate([by_head(n_dec), by_head(r_dec)], axis=0).astype(bf16)
            sc_b = lower * lax.dot_general(lhs_nr, by_head(b_dec).astype(bf16), nt_dims,
                                           preferred_element_type=f32)
            sc_k = lower * lax.dot_general(lhs_nr, by_head(k_dec).astype(bf16), nt_dims,
                                           preferred_element_type=f32)
            lhs_c = jnp.concatenate(
                [jnp.where(t_idx > s, n_dec * jnp.broadcast_to(b_dec[s:s + 1], (ln, lanes)), 0.0)
                 for s in range(ln - 1)], axis=0).astype(bf16)
            coef_scr[c] = _dot(lhs_c, ones_bd)
            rhs_end = jnp.concatenate([by_head(bb * to_end), by_head(kk * to_end)], axis=0).astype(bf16)
            pre.append((lhs_nr, sc_k[:2 * ln].astype(bf16), sc_b[2 * ln:].astype(bf16),
                        sc_k[2 * ln:].astype(bf16), stack_heads(vv).astype(bf16), rhs_end,
                        jnp.exp(cum[ln - 1:ln])))

        gs = [lax.dot_general(pre[c][0], st_scr[c].astype(bf16), nt_dims, preferred_element_type=f32)
              for c in range(len(combos))]
        for c in range(len(combos)):
            x_scr[c] = pack_heads(gs[c][:2 * ln] + _dot(pre[c][1], pre[c][4]))
        for s in range(ln - 1):
            for c in range(len(combos)):
                x = x_scr[c]
                x_scr[c] = x + coef_scr[c, s * ln:(s + 1) * ln, :] * jnp.broadcast_to(x[s:s + 1], (ln, lanes))
        for c, (b, p) in enumerate(combos):
            _, _, sc_b_out, sc_k_out, vs_bf, rhs_end, d_last = pre[c]
            xs_bf = stack_heads(x_scr[c]).astype(bf16)
            o = gs[c][2 * ln:] + _dot(sc_b_out, xs_bf) + _dot(sc_k_out, vs_bf)
            o_ref[b, pl.ds(base, ln), pl.ds(p * lanes, lanes)] = pack_heads(o)
            upd = lax.dot_general(jnp.concatenate([xs_bf, vs_bf], axis=0), rhs_end, tn_dims,
                                  preferred_element_type=f32)
            st_scr[c] = st_scr[c] * d_last + upd
        return carry

    lax.fori_loop(0, tb // ln, chunk, 0)
    finish()


def _rw_lanes_kernel(*refs, tlen, has_prev):
    r_ref, w_ref, k_ref, v_ref, kn_ref, b_ref, s0_ref = refs[:7]
    o_ref, s_ref, vec_scr, o_scr = refs[7 + has_prev:]
    n = RW_N
    bsz = V7X_LANES
    sub = V7X_SUBLANES
    s_ref[...] = s0_ref[...]
    for t in range(tlen):
        for slot, ref in enumerate((kn_ref, w_ref, b_ref, k_ref, r_ref, v_ref)):
            x = ref[pl.ds(t, bsz, stride=tlen), :].T
            vec_scr[slot] = jnp.exp(x) if slot == 1 else x

        def value_rows(ig, carry):
            i0 = pl.multiple_of(ig * sub, sub)
            for h in range(RW_PAIR):
                kn_h, w_h, b_h, k_h, r_h = (vec_scr[slot, h * n:(h + 1) * n, :] for slot in range(5))
                v_tile = vec_scr[5, pl.ds(h * n + i0, sub), :]
                o_rows = []
                for ii in range(sub):
                    s_old = s_ref[h, i0 + ii]
                    sa = jnp.sum(s_old * kn_h, axis=0, keepdims=True)
                    s_new = s_old * w_h + sa * b_h + v_tile[ii:ii + 1] * k_h
                    s_ref[h, i0 + ii] = s_new
                    o_rows.append(jnp.sum(s_new * r_h, axis=0, keepdims=True))
                o_scr[pl.ds(h * n + i0, sub), :] = jnp.concatenate(o_rows, axis=0)
            return carry

        lax.fori_loop(0, n // sub, value_rows, 0)
        o_ref[pl.ds(t, bsz, stride=tlen), :] = o_scr[...].T


def _rw_lanes_scan(seqs, s_in, s_prev, layer, depth, bsz, tlen):
    r, w, k, v, kn, bb = seqs
    width = r.shape[-1]
    heads = width // RW_N
    lanes = RW_PAIR * RW_N
    rows = bsz * tlen
    blk = pl.BlockSpec((rows, lanes), lambda p: (0, p))
    st_spec = pl.BlockSpec((None, RW_PAIR, RW_N, RW_N, bsz), lambda p: (layer, p, 0, 0, 0))
    in_specs = [blk] * 6 + [st_spec]
    args = [a.reshape(rows, width) for a in (r, w, k, v, kn, bb)] + [s_in]
    aliases = _threaded_state(args, in_specs, s_prev, 1)
    o, s_new = pl.pallas_call(
        functools.partial(_rw_lanes_kernel, tlen=tlen, has_prev=s_prev is not None),
        grid=(heads // RW_PAIR,),
        in_specs=in_specs,
        out_specs=[blk, st_spec],
        out_shape=[jax.ShapeDtypeStruct((rows, width), f32),
                   jax.ShapeDtypeStruct((depth, heads, RW_N, RW_N, bsz), f32)],
        scratch_shapes=[pltpu.VMEM((6, lanes, bsz), f32), pltpu.VMEM((lanes, bsz), f32)],
        input_output_aliases=aliases,
        compiler_params=_params("parallel"),
        name="rwkv7_lanes_scan",
    )(*args)
    return o.reshape(bsz, tlen, width), s_new


def _rw_mode(bsz, tlen):
    if _tile(tlen, 64) % RW_CHUNK == 0:
        return "chunk"
    return "lanes" if bsz == V7X_LANES else "token"


def _rw_scan(seqs, s_in, s_prev, layer, depth, bsz, tlen):
    if _rw_mode(bsz, tlen) == "lanes" and s_in is not None:
        return _rw_lanes_scan(seqs, s_in, s_prev, layer, depth, bsz, tlen)
    r, w, k, v, kn, bb = seqs
    width = r.shape[-1]
    heads = width // RW_N
    pairs = heads // RW_PAIR
    nb = _tile(bsz, 4, align=1)
    tb = _tile(tlen, 64)
    chunked = tb % RW_CHUNK == 0
    has_state = s_in is not None
    blk = pl.BlockSpec((nb, tb, width), lambda bi, ti: (bi, ti, 0))
    st_spec = pl.BlockSpec((None, nb, heads, RW_N, RW_N), lambda bi, ti: (layer, bi, 0, 0, 0))
    m = nb * pairs * RW_N
    lanes = RW_PAIR * RW_N
    in_specs = [blk] * 6 + ([st_spec] if has_state else [])
    args = [r, w, k, v, kn, bb] + ([s_in] if has_state else [])
    aliases = _threaded_state(args, in_specs, s_prev, 1)
    scratch = [pltpu.VMEM((nb * pairs, RW_N, lanes), f32)]
    if chunked:
        scratch += [pltpu.VMEM((nb * pairs, (RW_CHUNK - 1) * RW_CHUNK, lanes), f32),
                    pltpu.VMEM((nb * pairs, RW_CHUNK, lanes), f32)]
    else:
        scratch += [pltpu.VMEM((m, lanes), bf16)] * 3
    o, s_new = pl.pallas_call(
        functools.partial(_rw_chunk_kernel if chunked else _rw_scan_kernel, nb=nb, tb=tb, pairs=pairs,
                          has_state=has_state, has_prev=s_prev is not None),
        grid=(bsz // nb, tlen // tb),
        in_specs=in_specs,
        out_specs=[blk, st_spec],
        out_shape=[jax.ShapeDtypeStruct((bsz, tlen, width), f32),
                   jax.ShapeDtypeStruct((depth, bsz, heads, RW_N, RW_N), f32)],
        scratch_shapes=scratch,
        input_output_aliases=aliases,
        compiler_params=_params("parallel", "arbitrary"),
        name="rwkv7_chunk_scan" if chunked else "rwkv7_scan",
    )(*args)
    return o, s_new


def _rw_post_kernel(o_ref, r_ref, k_ref, v_ref, g_ref, lw_ref, lbias_ref, rk_ref, y_ref):
    ones_bd = _group_ones(V7X_LANES, RW_N)
    o = o_ref[...]
    mu = _group_sum(o, ones_bd) * (1.0 / RW_N)
    d = o - mu
    var = _group_sum(d * d, ones_bd) * (1.0 / RW_N)
    on = d * lax.rsqrt(var + RW_GN_EPS) * lw_ref[...] + lbias_ref[...]
    rk = r_ref[...].astype(f32) * k_ref[...].astype(f32) * rk_ref[...]
    bonus = _group_sum(rk, ones_bd) * v_ref[...].astype(f32)
    y_ref[...] = ((on + bonus) * g_ref[...].astype(f32)).astype(y_ref.dtype)


def _rw_post(o, r, k, v, g, ln_w, ln_b, r_k):
    n, width = o.shape
    tm = _tile(n, 256)
    blk = pl.BlockSpec((tm, width), lambda i: (i, 0))
    vec = pl.BlockSpec((1, width), lambda i: (0, 0))
    return pl.pallas_call(
        _rw_post_kernel,
        grid=(n // tm,),
        in_specs=[blk] * 5 + [vec] * 3,
        out_specs=blk,
        out_shape=jax.ShapeDtypeStruct((n, width), bf16),
        compiler_params=_params("parallel"),
        name="rwkv7_post",
    )(o, r, k, v, g, ln_w, ln_b, r_k)


def _conv_kernel(*refs, nb, tb, has_state, has_prev):
    sb_ref, sc_ref, sh_ref = refs[:3]
    st_ref = refs[3] if has_state else None
    cw_ref = refs[3 + has_state]
    y_ref, new_ref = refs[4 + has_state + has_prev:]
    ti = pl.program_id(2)
    rows = nb * tb
    w = sb_ref.shape[-1]
    hist = SC_KSIZE - 1

    @pl.when(ti == 0)
    def _():
        new_ref[...] = st_ref[...] if has_state else jnp.zeros((nb, hist, w), f32)

    u = (sc_ref[...].astype(f32) * sh_ref[...].astype(f32)).reshape(rows, w)
    carry = new_ref[...]
    conv = cw_ref[hist:hist + 1, :] * u
    for back in range(1, hist + 1):
        first = [_bcast_rows(carry[:, hist - back + j:hist - back + j + 1, :], nb, tb) for j in range(back)]
        conv = conv + cw_ref[hist - back:hist - back + 1, :] * _shift_rows(u, first, tb, back)
    y_ref[...] = (sb_ref[...].astype(f32).reshape(rows, w) * conv).reshape(nb, tb, w).astype(y_ref.dtype)
    new_ref[...] = u.reshape(nb, tb, w)[:, tb - hist:tb, :]


def _conv(p_sc, st_in, st_prev, layer, depth, conv_w, bsz, tlen):
    width = conv_w.shape[1]
    cw = min(width, 256)
    ncol = width // cw
    nb, tb = _seq_block(bsz, tlen)
    assert tb >= SC_KSIZE - 1
    has_state = st_in is not None
    p3 = p_sc.reshape(bsz, tlen, 3 * width)
    seg = lambda s: pl.BlockSpec((nb, tb, cw), lambda bi, cj, ti, s=s: (bi, ti, s * ncol + cj))
    st_spec = pl.BlockSpec((None, nb, SC_KSIZE - 1, cw), lambda bi, cj, ti: (layer, bi, 0, cj))
    in_specs = ([seg(0), seg(1), seg(2)] + ([st_spec] if has_state else [])
                + [pl.BlockSpec((SC_KSIZE, cw), lambda bi, cj, ti: (0, cj))])
    args = [p3, p3, p3] + ([st_in] if has_state else []) + [conv_w]
    aliases = _threaded_state(args, in_specs, st_prev, 1)
    y, new = pl.pallas_call(
        functools.partial(_conv_kernel, nb=nb, tb=tb, has_state=has_state, has_prev=st_prev is not None),
        grid=(bsz // nb, ncol, tlen // tb),
        in_specs=in_specs,
        out_specs=[pl.BlockSpec((nb, tb, cw), lambda bi, cj, ti: (bi, ti, cj)), st_spec],
        out_shape=[jax.ShapeDtypeStruct((bsz, tlen, width), bf16),
                   jax.ShapeDtypeStruct((depth, bsz, SC_KSIZE - 1, width), f32)],
        input_output_aliases=aliases,
        compiler_params=_params("parallel", "parallel", "arbitrary"),
        name="short_conv",
    )(*args)
    return y.reshape(bsz * tlen, width), new


def _merge_kernel(a_ref, b_ref, c_ref, ga_ref, gb_ref, gc_ref, pa_ref, pb_ref, pc_ref, m_ref, w_scr):
    @pl.when(pl.program_id(1) == 0)
    def _():
        for s, w_ref in enumerate((pa_ref, pb_ref, pc_ref)):
            w_scr[s] = w_ref[...].astype(bf16)

    m = (_sigmoid(ga_ref[...].astype(f32)) * _dot(a_ref[...], w_scr[0])
         + _sigmoid(gb_ref[...].astype(f32)) * _dot(b_ref[...], w_scr[1])
         + _sigmoid(gc_ref[...].astype(f32)) * _dot(c_ref[...], w_scr[2]))
    m_ref[...] = m.astype(m_ref.dtype)


def _merge(ya, yb, yc, p_gate, w_pa, w_pb, w_pc, layer):
    n = ya.shape[0]
    width, d = w_pa.shape[1:]
    assert w_pb.shape[1] == width and w_pc.shape[1] == width
    tm = _tile(n, 512)
    tn = _tile(d, 512, align=V7X_LANES)
    nj = d // tn
    act = pl.BlockSpec((tm, width), lambda j, i: (i, 0))
    gate = lambda s: pl.BlockSpec((tm, tn), lambda j, i, s=s: (i, s * nj + j))
    wt = pl.BlockSpec((None, width, tn), lambda j, i: (layer, 0, j))
    return pl.pallas_call(
        _merge_kernel,
        grid=(nj, n // tm),
        in_specs=[act, act, act, gate(0), gate(1), gate(2), wt, wt, wt],
        out_specs=pl.BlockSpec((tm, tn), lambda j, i: (i, j)),
        out_shape=jax.ShapeDtypeStruct((n, d), bf16),
        scratch_shapes=[pltpu.VMEM((3, width, tn), bf16)],
        compiler_params=_params("arbitrary", "arbitrary"),
        name="gated_merge",
    )(ya, yb, yc, p_gate, p_gate, p_gate, w_pa, w_pb, w_pc)


def _oproj_kernel(m_ref, wo_ref, x_ref, g1_ref, g2_ref, x1_ref, h2_ref):
    mix = _dot(m_ref[...], wo_ref[...])
    x1 = x_ref[...] + _rms(mix, g1_ref[...])
    x1_ref[...] = x1
    h2_ref[...] = _rms(x1, g2_ref[...]).astype(h2_ref.dtype)


def _oproj(m, w_o, layer, x, g_post_mix, g_pre_mlp):
    n, d = x.shape
    tm = _tile(n, 256)
    blk = pl.BlockSpec((tm, d), lambda i: (i, 0))
    vec = pl.BlockSpec((1, d), lambda i: (0, 0))
    return pl.pallas_call(
        _oproj_kernel,
        grid=(n // tm,),
        in_specs=[blk, pl.BlockSpec((None, d, d), lambda i: (layer, 0, 0)), blk, vec, vec],
        out_specs=[blk, blk],
        out_shape=[jax.ShapeDtypeStruct((n, d), f32), jax.ShapeDtypeStruct((n, d), bf16)],
        compiler_params=_params("parallel"),
        name="out_proj_norms",
    )(m, w_o, x, g_post_mix, g_pre_mlp)


def _mlp_kernel(h_ref, w1_ref, w2_ref, ff_ref):
    kf = pl.program_id(1)

    @pl.when(kf == 0)
    def _():
        ff_ref[...] = jnp.zeros_like(ff_ref)

    t = jnp.maximum(_dot(h_ref[...], w1_ref[...].astype(bf16)), 0.0)
    ff_ref[...] += _dot((t * t).astype(bf16), w2_ref[...].astype(bf16))


def _mlp(h2, w1, w2, layer):
    n, d = h2.shape
    dff = w1.shape[2]
    tm = _tile(n, 1024)
    fc = _tile(dff, 512, align=V7X_LANES)
    once = pl.Buffered(1)
    return pl.pallas_call(
        _mlp_kernel,
        grid=(n // tm, dff // fc),
        in_specs=[pl.BlockSpec((tm, d), lambda i, kf: (i, 0), pipeline_mode=once),
                  pl.BlockSpec((None, d, fc), lambda i, kf: (layer, 0, kf)),
                  pl.BlockSpec((None, fc, d), lambda i, kf: (layer, kf, 0))],
        out_specs=pl.BlockSpec((tm, d), lambda i, kf: (i, 0), pipeline_mode=once),
        out_shape=jax.ShapeDtypeStruct((n, d), f32),
        compiler_params=_params("parallel", "arbitrary"),
        name="relu2_mlp",
    )(h2, w1, w2)


def _residual_kernel(*refs, has_next):
    if has_next:
        x1_ref, ff_ref, g_ref, gn_ref, y_ref, h_ref = refs
    else:
        x1_ref, ff_ref, g_ref, y_ref = refs
    y = x1_ref[...] + _rms(ff_ref[...], g_ref[...])
    y_ref[...] = y
    if has_next:
        h_ref[...] = _rms(y, gn_ref[...]).astype(h_ref.dtype)


def _residual(x1, ff, g_post_mlp, g_next):
    n, d = x1.shape
    tm = _tile(n, 512)
    blk = pl.BlockSpec((tm, d), lambda i: (i, 0))
    vec = pl.BlockSpec((1, d), lambda i: (0, 0))
    has_next = g_next is not None
    outs = pl.pallas_call(
        functools.partial(_residual_kernel, has_next=has_next),
        grid=(n // tm,),
        in_specs=[blk, blk, vec] + ([vec] if has_next else []),
        out_specs=[blk] + ([blk] if has_next else []),
        out_shape=[jax.ShapeDtypeStruct((n, d), f32)] + ([jax.ShapeDtypeStruct((n, d), bf16)] if has_next else []),
        compiler_params=_params("parallel"),
        name="mlp_residual_norm",
    )(*([x1, ff, g_post_mlp] + ([g_next] if has_next else [])))
    return (outs[0], outs[1]) if has_next else (outs[0], None)


def _layer(x, h, bsz, tlen, states_in, states_prev, layer, depth, wts, lw, g_next):
    hg_in, rw_in, shift_in, sc_in = states_in
    hg_prev, rw_prev, shift_prev, sc_prev = states_prev
    w_in, offs = wts["w_in"], wts["offs"]
    p_hg = _proj(h, w_in, layer, offs[0], offs[1] - offs[0], "in_proj_hgrn")
    p_rkv = _proj(h, w_in, layer, offs[1], offs[2] - offs[1], "in_proj_rwkv")
    p_lora = _proj(h, w_in, layer, offs[2], offs[3] - offs[2], "in_proj_rwkv_lora")
    p_sc = _proj(h, w_in, layer, offs[3], offs[4] - offs[3], "in_proj_conv", bf16)
    p_gate = _proj(h, w_in, layer, offs[4], offs[5] - offs[4], "in_proj_gate", bf16)

    ya, hg_new = _hgrn(p_hg, lw["lb"], lw["hg_norm"], hg_in, hg_prev, layer, depth, bsz, tlen)

    seq_dtype = bf16 if _rw_mode(bsz, tlen) == "chunk" else f32
    seqs, shift_new = _rw_prep(p_rkv, p_lora, shift_in, shift_prev, layer, depth, lw["rw_prep"], bsz, tlen,
                               seq_dtype)
    r, w, k, v, kn, bb, g = seqs
    o_rw, rw_new = _rw_scan((r, w, k, v, kn, bb), rw_in, rw_prev, layer, depth, bsz, tlen)
    flat = lambda t: t.reshape(bsz * tlen, t.shape[-1])
    yb = _rw_post(flat(o_rw), flat(r), flat(k), flat(v), flat(g), lw["rw_ln_w"], lw["rw_ln_b"], lw["rw_r_k"])

    yc, sc_new = _conv(p_sc, sc_in, sc_prev, layer, depth, lw["sc_conv_w"], bsz, tlen)

    m = _merge(ya, yb, yc, p_gate, wts["w_pa"], wts["w_pb"], wts["w_pc"], layer)
    x1, h2 = _oproj(m, wts["w_o"], layer, x, lw["g_post_mix"], lw["g_pre_mlp"])
    ff = _mlp(h2, wts["w_ff1"], wts["w_ff2"], layer)
    y, h_next = _residual(x1, ff, lw["g_post_mlp"], g_next)
    return y, h_next, (hg_new, rw_new, shift_new, sc_new)


def kernel(x_prompt, x_sample, state_hgrn, state_rwkv, state_rwkv_shift, state_conv, norm_pre_mix, norm_post_mix, norm_pre_mlp, norm_post_mlp, w_in, hg_lb_logits, hg_norm, w_pa, rw_mu, rw_w0, rw_w2, rw_a0, rw_a2, rw_g2, rw_k_k, rw_k_a, rw_r_k, rw_ln_w, rw_ln_b, w_pb, sc_conv_w, w_pc, w_o, w_ff1, w_ff2):
    depth = w_in.shape[0]
    d_model = x_prompt.shape[-1]
    hg_width = hg_norm.shape[1]
    rw_width = rw_w0.shape[1]
    rw_shift_width = rw_mu.shape[1]
    sc_width = sc_conv_w.shape[2]
    off_rw = 4 * hg_width
    off_lora = off_rw + 3 * rw_width
    off_sc = off_rw + rw_shift_width
    off_gate = off_sc + 3 * sc_width
    assert w_in.shape[2] == off_gate + 3 * d_model

    lb_all = _lower_bounds(hg_lb_logits.astype(f32))
    row = lambda a, l: a[l].reshape(1, -1).astype(f32)
    wts = {"w_in": w_in.astype(f32), "offs": (0, off_rw, off_lora, off_sc, off_gate, w_in.shape[2]),
           "w_pa": w_pa.astype(f32), "w_pb": w_pb.astype(f32), "w_pc": w_pc.astype(f32),
           "w_o": w_o.astype(bf16), "w_ff1": w_ff1.astype(f32), "w_ff2": w_ff2.astype(f32)}

    bp, tp, _ = x_prompt.shape
    bs, ts, _ = x_sample.shape
    yp = x_prompt.reshape(bp * tp, d_model).astype(f32)
    ys = x_sample.reshape(bs * ts, d_model).astype(f32)
    hp = _rmsnorm_cast(yp, row(norm_pre_mix, 0))
    hs = _rmsnorm_cast(ys, row(norm_pre_mix, 0))
    rw_lanes = _rw_mode(bs, ts) == "lanes"
    rw_state = state_rwkv.astype(f32)
    sample_in = (state_hgrn.astype(f32), jnp.transpose(rw_state, (0, 2, 3, 4, 1)) if rw_lanes else rw_state,
                 state_rwkv_shift.astype(f32).reshape(depth, bs, 1, rw_shift_width), state_conv.astype(f32))
    new_p = new_s = (None, None, None, None)
    for l in range(depth):
        lw = {
            "g_post_mix": row(norm_post_mix, l), "g_pre_mlp": row(norm_pre_mlp, l),
            "g_post_mlp": row(norm_post_mlp, l),
            "lb": lb_all[l:l + 1], "hg_norm": row(hg_norm, l),
            "rw_prep": (row(rw_mu, l), row(rw_w0, l), rw_w2[l].astype(bf16), row(rw_a0, l), rw_a2[l].astype(bf16),
                        rw_g2[l].astype(bf16), row(rw_k_k, l), row(rw_k_a, l)),
            "rw_ln_w": row(rw_ln_w, l), "rw_ln_b": row(rw_ln_b, l), "rw_r_k": row(rw_r_k, l),
            "sc_conv_w": sc_conv_w[l].astype(f32),
        }
        g_next = row(norm_pre_mix, l + 1) if l + 1 < depth else None
        yp, hp, new_p = _layer(yp, hp, bp, tp, (None, None, None, None), new_p, l, depth, wts, lw, g_next)
        ys, hs, new_s = _layer(ys, hs, bs, ts, sample_in, new_s, l, depth, wts, lw, g_next)
    shift = lambda s, b: s.reshape(depth, b, rw_shift_width)
    return (yp.reshape(bp, tp, d_model), ys.reshape(bs, ts, d_model),
            new_p[0], new_p[1], shift(new_p[2], bp), new_p[3],
            new_s[0], jnp.transpose(new_s[1], (0, 4, 1, 2, 3)) if rw_lanes else new_s[1],
            shift(new_s[2], bs), new_s[3])
```

```python
import functools

import jax
import jax.numpy as jnp
from jax import lax
from jax.experimental import pallas as pl
from jax.experimental.pallas import tpu as pltpu

HG_DK = 128
RW_N = 64
HG_F_MIN = 1e-30
RW_GN_EPS = 64e-5
NORM_EPS = 1e-6
SC_KSIZE = 3

V7X_LANES = 128
V7X_SUBLANES = 8
V7X_BF16_ROWS = 16
V7X_VMEM_LIMIT_BYTES = 48 * 1024 * 1024
HG_CHUNK = 16
HG_HEADS_PER_STEP = 4
HG_GROUP = 4
RW_PAIR = 2
RW_CHUNK = 16

f32 = jnp.float32
bf16 = jnp.bfloat16


def _tile(n, pref, align=V7X_SUBLANES):
    if n <= pref:
        return n
    for d in range(pref, 0, -1):
        if n % d == 0 and d % align == 0:
            return d
    return n


def _seq_block(bsz, tlen):
    if tlen >= 256:
        return 1, _tile(tlen, 256)
    return _tile(bsz, max(1, 128 // tlen), align=1), tlen


def _params(*sem):
    return pltpu.CompilerParams(dimension_semantics=sem, vmem_limit_bytes=V7X_VMEM_LIMIT_BYTES)


def _sigmoid(x):
    return 1.0 / (1.0 + jnp.exp(-x))


def _dot(a, b):
    return jnp.dot(a, b, preferred_element_type=f32)


def _rms(x, g):
    return x * lax.rsqrt(jnp.mean(x * x, axis=-1, keepdims=True) + NORM_EPS) * g


def _group_ones(width, group):
    r = lax.broadcasted_iota(jnp.int32, (width, width), 0) // group
    c = lax.broadcasted_iota(jnp.int32, (width, width), 1) // group
    return jnp.where(r == c, 1.0, 0.0).astype(bf16)


def _group_sum(z, ones_bd):
    outs = []
    for c in range(z.shape[-1] // V7X_LANES):
        zc = z[:, c * V7X_LANES:(c + 1) * V7X_LANES]
        hi = zc.astype(bf16)
        lo = (zc - hi.astype(f32)).astype(bf16)
        outs.append(_dot(hi, ones_bd) + _dot(lo, ones_bd))
    return outs[0] if len(outs) == 1 else jnp.concatenate(outs, axis=-1)


def _threaded_state(args, in_specs, prev_out, out_index):
    if prev_out is None:
        return {}
    args.append(prev_out)
    in_specs.append(pl.BlockSpec(memory_space=pl.ANY))
    return {len(args) - 1: out_index}


def _lb_kernel(logit_ref, lb_ref):
    z = logit_ref[...]
    depth = z.shape[0]
    m = jnp.max(z, axis=0, keepdims=True)
    e = jnp.exp(z - m)
    p = e / jnp.sum(e, axis=0, keepdims=True)
    acc = jnp.zeros_like(p[0:1])
    for l in range(depth):
        acc = acc + p[l:l + 1]
        lb_ref[l:l + 1, :] = jnp.clip(acc - p[0:1], 0.0, 1.0)


def _lower_bounds(logits):
    return pl.pallas_call(
        _lb_kernel, out_shape=jax.ShapeDtypeStruct(logits.shape, f32), name="hg_lower_bounds",
    )(logits)


def _rmsnorm_kernel(x_ref, g_ref, o_ref):
    o_ref[...] = _rms(x_ref[...], g_ref[...]).astype(o_ref.dtype)


def _rmsnorm_cast(x, g):
    n, d = x.shape
    tm = _tile(n, 512)
    return pl.pallas_call(
        _rmsnorm_kernel,
        grid=(n // tm,),
        in_specs=[pl.BlockSpec((tm, d), lambda i: (i, 0)), pl.BlockSpec((1, d), lambda i: (0, 0))],
        out_specs=pl.BlockSpec((tm, d), lambda i: (i, 0)),
        out_shape=jax.ShapeDtypeStruct((n, d), bf16),
        compiler_params=_params("parallel"),
        name="rmsnorm_cast",
    )(x, g)


def _proj_kernel(a_ref, w_ref, o_ref, wbf_scr):
    @pl.when(pl.program_id(1) == 0)
    def _():
        wbf_scr[...] = w_ref[0].astype(bf16)

    o_ref[...] = _dot(a_ref[...], wbf_scr[...]).astype(o_ref.dtype)


def _proj(h, w_stack, layer, off, width, name, out_dtype=f32):
    n, k = h.shape
    tm = _tile(n, 1024)
    tn = _tile(width, 1024, align=V7X_LANES)
    w_spec = pl.BlockSpec((pl.Element(1), pl.Element(k), pl.Element(tn)),
                          lambda j, i: (layer, 0, pl.multiple_of(off + j * tn, V7X_LANES)))
    return pl.pallas_call(
        _proj_kernel,
        grid=(width // tn, n // tm),
        in_specs=[pl.BlockSpec((tm, k), lambda j, i: (i, 0)), w_spec],
        out_specs=pl.BlockSpec((tm, tn), lambda j, i: (i, j)),
        out_shape=jax.ShapeDtypeStruct((n, width), out_dtype),
        scratch_shapes=[pltpu.VMEM((k, tn), bf16)],
        compiler_params=_params("arbitrary", "arbitrary"),
        name=name,
    )(h, w_stack)


def _hgrn_kernel(*refs, nb, tb, hb, chunk, has_state, has_prev):
    q_ref, f_ref, i_ref, og_ref, lb_ref, gn_ref, rsel_ref = refs[:7]
    s0_ref = refs[7] if has_state else None
    o_ref, s_ref, st_scr = refs[7 + has_state + has_prev:]
    ti = pl.program_id(2)
    rows = nb * tb
    dk = HG_DK
    nchunk = rows // chunk
    heads = range(hb)

    @pl.when(ti == 0)
    def _():
        for b in range(nb):
            for h in heads:
                st_scr[b, h] = s0_ref[b, h].T if has_state else jnp.zeros((dk, dk), f32)

    grp = HG_GROUP if (tb // chunk) % HG_GROUP == 0 else 1
    row_idx = lax.broadcasted_iota(jnp.int32, (rows, dk), 0)
    t_idx = row_idx % chunk
    pos_in_grp = (row_idx // chunk) % grp
    chunk_r = lax.broadcasted_iota(jnp.int32, (rows, rows), 0) // chunk
    chunk_c = lax.broadcasted_iota(jnp.int32, (rows, rows), 1) // chunk
    chunk_masks = [jnp.where((chunk_r - chunk_c == dist) & (chunk_r // grp == chunk_c // grp), 1.0, 0.0)
                   for dist in range(grp)]
    ngroup = nchunk // grp
    col_group = lax.broadcasted_iota(jnp.int32, (dk, rows), 1) // (grp * chunk)

    def chunk_row(a, s):
        a3 = a.reshape(nchunk, chunk, dk)
        return jnp.broadcast_to(a3[:, s:s + 1, :], (nchunk, chunk, dk)).reshape(rows, dk)

    def head_cols(ref, h):
        return ref[:, :, h * dk:(h + 1) * dk].reshape(rows, dk)

    sub = V7X_SUBLANES
    tpc = chunk // sub

    def tiles(a):
        return a.reshape(nchunk, tpc, sub, dk)

    glogs, q_decs, o_intras, u_alls = [], [], [], []
    for h in heads:
        lb = lb_ref[:, h * dk:(h + 1) * dk]
        hq = head_cols(q_ref, h)
        qq = hq * _sigmoid(hq)
        f = lb + (1.0 - lb) * _sigmoid(head_cols(f_ref, h))
        kk = 1.0 - f
        bcum = jnp.log(jnp.maximum(f, HG_F_MIN))
        step = 1
        while step < chunk:
            bcum = bcum + jnp.where(t_idx >= step, pltpu.roll(bcum, step, 0), 0.0)
            step *= 2
        b_last = chunk_row(bcum, chunk - 1)
        zs = []
        for s in range(chunk):
            t0 = s // sub

            def col_row(a):
                row = tiles(a)[:, t0:t0 + 1, s % sub:s % sub + 1, :]
                return jnp.broadcast_to(row, (nchunk, tpc - t0, sub, dk))

            dec = jnp.exp(jnp.minimum(tiles(bcum)[:, t0:] - col_row(bcum), 0.0))
            z = jnp.where(tiles(t_idx)[:, t0:] >= s, tiles(qq)[:, t0:] * col_row(kk) * dec, 0.0)
            if t0:
                z = jnp.concatenate([jnp.zeros((nchunk, t0, sub, dk), f32), z], axis=1)
            zs.append(z.reshape(rows, dk).astype(bf16))
        k_dec =(kk * jnp.exp(b_last - bcum)).astype(bf16)
        vv = head_cols(i_ref, h)
        a_rep = _dot(jnp.concatenate(zs, axis=-1), rsel_ref[...])
        scores = a_rep * chunk_masks[0]
        between = jnp.zeros_like(bcum)
        for dist in range(1, grp):
            q_far = (qq * jnp.exp(bcum + between)).astype(bf16)
            far = lax.dot_general(q_far, k_dec, (((1,), (1,)), ((), ())), preferred_element_type=f32)
            scores = scores + far * chunk_masks[dist]
            between = between + pltpu.roll(b_last, dist * chunk, 0)
        reach = jnp.zeros_like(bcum)
        tail = jnp.zeros_like(bcum)
        for dist in range(1, grp):
            reach = reach + jnp.where(pos_in_grp >= dist, pltpu.roll(b_last, dist * chunk, 0), 0.0)
            tail = tail + jnp.where(pos_in_grp + dist < grp, pltpu.roll(b_last, rows - dist * chunk, 0), 0.0)
        glogs.append(bcum + reach)
        q_decs.append((qq * jnp.exp(bcum + reach)).astype(bf16))
        o_intras.append(_dot(scores.astype(bf16), vv.astype(bf16)))
        k_grp = k_dec if grp == 1 else (kk * jnp.exp(b_last - bcum + tail)).astype(bf16)
        v_t = vv.T
        lhs = jnp.concatenate([jnp.where(col_group == g, v_t, 0.0).astype(bf16) for g in range(ngroup)], axis=0)
        u_alls.append(_dot(lhs, k_grp))

    outs = [[] for _ in heads]
    span = grp * chunk
    for b in range(nb):
        sts = [st_scr[b, h] for h in heads]
        for g in range(tb // span):
            gi = b * (tb // span) + g
            r0 = gi * span
            for h in heads:
                inter = lax.dot_general(q_decs[h][r0:r0 + span], sts[h].astype(bf16),
                                        (((1,), (1,)), ((), ())), preferred_element_type=f32)
                outs[h].append(inter + o_intras[h][r0:r0 + span])
                sts[h] = jnp.exp(glogs[h][r0 + span - 1:r0 + span]) * sts[h] + u_alls[h][gi * dk:(gi + 1) * dk]
        for h in heads:
            st_scr[b, h] = sts[h]
    for h in heads:
        o = jnp.concatenate(outs[h], axis=0)
        o = o * lax.rsqrt(jnp.mean(o * o, axis=-1, keepdims=True) + NORM_EPS)
        hog = head_cols(og_ref, h)
        o = o * gn_ref[:, h * dk:(h + 1) * dk] * (hog * _sigmoid(hog))
        o_ref[:, :, h * dk:(h + 1) * dk] = o.reshape(nb, tb, dk).astype(o_ref.dtype)

    @pl.when(ti == pl.num_programs(2) - 1)
    def _():
        for b in range(nb):
            for h in heads:
                s_ref[b, h] = st_scr[b, h].T


def _hgrn(p_hg, lb, gn, s_in, s_prev, layer, depth, bsz, tlen):
    width = p_hg.shape[1] // 4
    heads = width // HG_DK
    chunk = min(HG_CHUNK, tlen)
    nb, tb = _seq_block(bsz, tlen)
    assert tb % chunk == 0
    p3 = p_hg.reshape(bsz, tlen, 4 * width)
    hb = HG_HEADS_PER_STEP if heads % HG_HEADS_PER_STEP == 0 else 1
    hgroups = heads // hb
    wb = hb * HG_DK
    seg = lambda s: pl.BlockSpec((nb, tb, wb), lambda bi, h, ti, s=s: (bi, ti, s * hgroups + h))
    vec = pl.BlockSpec((1, wb), lambda bi, h, ti: (0, h))
    st_spec = pl.BlockSpec((None, nb, hb, HG_DK, HG_DK), lambda bi, h, ti: (layer, bi, h, 0, 0))
    has_state = s_in is not None
    rows = nb * tb
    rsel = (jnp.arange(chunk * HG_DK)[:, None] // HG_DK == jnp.arange(rows)[None, :] % chunk).astype(bf16)
    rsel_spec = pl.BlockSpec((chunk * HG_DK, rows), lambda bi, h, ti: (0, 0))
    in_specs = [seg(0), seg(1), seg(2), seg(3), vec, vec, rsel_spec] + ([st_spec] if has_state else [])
    args = [p3, p3, p3, p3, lb, gn, rsel] + ([s_in] if has_state else [])
    aliases = _threaded_state(args, in_specs, s_prev, 1)
    o, s_new = pl.pallas_call(
        functools.partial(_hgrn_kernel, nb=nb, tb=tb, hb=hb, chunk=chunk, has_state=has_state,
                          has_prev=s_prev is not None),
        grid=(bsz // nb, hgroups, tlen // tb),
        in_specs=in_specs,
        out_specs=[pl.BlockSpec((nb, tb, wb), lambda bi, h, ti: (bi, ti, h)), st_spec],
        out_shape=[jax.ShapeDtypeStruct((bsz, tlen, width), bf16),
                   jax.ShapeDtypeStruct((depth, bsz, heads, HG_DK, HG_DK), f32)],
        scratch_shapes=[pltpu.VMEM((nb, hb, HG_DK, HG_DK), f32)],
        input_output_aliases=aliases,
        compiler_params=_params("parallel", "parallel", "arbitrary"),
        name="hgrn2_scan",
    )(*args)
    return o.reshape(bsz * tlen, width), s_new


def _shift_rows(x, first, tb, by):
    rows = x.shape[0]
    t_idx = lax.broadcasted_iota(jnp.int32, (rows, 1), 0) % tb
    out = pltpu.roll(x, by, 0) if tb > by else x
    for j in range(by):
        out = jnp.where(t_idx == j, first[j], out)
    return out


def _bcast_rows(v3, nb, tb):
    w = v3.shape[-1]
    return jnp.broadcast_to(v3, (nb, tb, w)).reshape(nb * tb, w)


def _rw_prep_kernel(*refs, nb, tb, width, lora, has_state, has_prev):
    pm_ref, pl_ref = refs[:2]
    sh_ref = refs[2] if has_state else None
    mu_ref, w0_ref, w2_ref, a0_ref, a2_ref, g2_ref, kk_ref, ka_ref = refs[2 + has_state:10 + has_state]
    r_ref, w_ref, k_ref, v_ref, kn_ref, b_ref, g_ref, last_ref = refs[10 + has_state + has_prev:]
    ti = pl.program_id(1)
    rows = nb * tb
    tot = last_ref.shape[-1]
    x = jnp.concatenate([pm_ref[...], pl_ref[...]], axis=-1).reshape(rows, tot)

    @pl.when(ti == 0)
    def _():
        last_ref[...] = sh_ref[...] if has_state else jnp.zeros((nb, 1, tot), f32)

    prev = _shift_rows(x, [_bcast_rows(last_ref[...], nb, tb)], tb, 1)
    last_ref[...] = jnp.concatenate([pm_ref[:, tb - 1:tb, :], pl_ref[:, tb - 1:tb, :]], axis=-1)
    xs = x + (prev - x) * mu_ref[...]
    dl, al, gl = lora
    c = 3 * width
    r, kr, vr = xs[:, :width], xs[:, width:2 * width], xs[:, 2 * width:c]
    wd, ad, gd = xs[:, c:c + dl], xs[:, c + dl:c + dl + al], xs[:, c + dl + al:c + dl + al + gl]
    z = -(w0_ref[...] + _dot(jnp.tanh(wd).astype(bf16), w2_ref[...]))
    softplus = jnp.maximum(z, 0.0) + jnp.log(1.0 + jnp.exp(-jnp.abs(z)))
    w_log = -softplus - 0.5
    log_decay = -jnp.exp(w_log)
    a = _sigmoid(a0_ref[...] + _dot(ad.astype(bf16), a2_ref[...]))
    g = _dot(_sigmoid(gd).astype(bf16), g2_ref[...])
    kk = kr * kk_ref[...]
    ones_bd = _group_ones(V7X_LANES, RW_N)
    nrm = jnp.sqrt(_group_sum(kk * kk, ones_bd))
    kk = kk / jnp.maximum(nrm, 1e-12)
    k = kr * (1.0 + (a - 1.0) * ka_ref[...])
    shp = (nb, tb, width)
    r_ref[...] = r.reshape(shp).astype(r_ref.dtype)
    w_ref[...] = log_decay.reshape(shp)
    k_ref[...] = k.reshape(shp).astype(k_ref.dtype)
    v_ref[...] = vr.reshape(shp).astype(v_ref.dtype)
    kn_ref[...] = (-kk).reshape(shp).astype(kn_ref.dtype)
    b_ref[...] = (kk * a).reshape(shp).astype(b_ref.dtype)
    g_ref[...] = g.reshape(shp).astype(g_ref.dtype)


def _rw_prep(p_main, p_lora, shift_in, shift_prev, layer, depth, wts, bsz, tlen, seq_dtype):
    mu, w0, w2, a0, a2, g2, k_k, k_a = wts
    width = w0.shape[1]
    wl = p_lora.shape[1]
    tot = 3 * width + wl
    lora = (w2.shape[0], a2.shape[0], g2.shape[0])
    nb, tb = _seq_block(bsz, tlen)
    has_state = shift_in is not None
    blk = lambda w: pl.BlockSpec((nb, tb, w), lambda bi, ti: (bi, ti, 0))
    full = lambda arr: pl.BlockSpec(arr.shape, lambda bi, ti: (0,) * arr.ndim)
    last_spec = pl.BlockSpec((None, nb, 1, tot), lambda bi, ti: (layer, bi, 0, 0))
    small = [mu, w0, w2, a0, a2, g2, k_k, k_a]
    in_specs = [blk(3 * width), blk(wl)] + ([last_spec] if has_state else []) + [full(s) for s in small]
    args = ([p_main.reshape(bsz, tlen, 3 * width), p_lora.reshape(bsz, tlen, wl)]
            + ([shift_in] if has_state else []) + small)
    aliases = _threaded_state(args, in_specs, shift_prev, 7)
    seq = lambda i: jax.ShapeDtypeStruct((bsz, tlen, width), f32 if i == 1 else seq_dtype)
    outs = pl.pallas_call(
        functools.partial(_rw_prep_kernel, nb=nb, tb=tb, width=width, lora=lora, has_state=has_state,
                          has_prev=shift_prev is not None),
        grid=(bsz // nb, tlen // tb),
        in_specs=in_specs,
        out_specs=[blk(width)] * 7 + [last_spec],
        out_shape=[seq(i) for i in range(7)] + [jax.ShapeDtypeStruct((depth, bsz, 1, tot), f32)],
        input_output_aliases=aliases,
        compiler_params=_params("parallel", "arbitrary"),
        name="rwkv7_prep",
    )(*args)
    return outs[:7], outs[7]


def _rw_state_io(s0_ref, s_ref, st_scr, combos, has_state, ti, last):
    n = RW_N

    @pl.when(ti == 0)
    def _():
        for c, (b, p) in enumerate(combos):
            if has_state:
                st_scr[c] = jnp.concatenate([s0_ref[b, RW_PAIR * p + h] for h in range(RW_PAIR)], axis=-1)
            else:
                st_scr[c] = jnp.zeros((n, RW_PAIR * n), f32)

    def finish():
        @pl.when(ti == last)
        def _():
            for c, (b, p) in enumerate(combos):
                s = st_scr[c]
                for h in range(RW_PAIR):
                    s_ref[b, RW_PAIR * p + h] = s[:, h * n:(h + 1) * n]

    return finish


def _rw_scan_kernel(*refs, nb, tb, pairs, has_state, has_prev):
    r_ref, w_ref, k_ref, v_ref, kn_ref, b_ref = refs[:6]
    s0_ref = refs[6] if has_state else None
    o_ref, s_ref, st_scr, lhs_a, lhs_v, lhs_o = refs[6 + has_state + has_prev:]
    ti = pl.program_id(1)
    n = RW_N
    lanes = RW_PAIR * n
    sub = V7X_SUBLANES
    combos = [(b, p) for b in range(nb) for p in range(pairs)]
    finish = _rw_state_io(s0_ref, s_ref, st_scr, combos, has_state, ti, pl.num_programs(1) - 1)

    ones_bd = _group_ones(lanes, n)
    diag = (lax.broadcasted_iota(jnp.int32, (n, lanes), 0)
            == lax.broadcasted_iota(jnp.int32, (n, lanes), 1) % n)

    def token_group(grp, carry):
        base = pl.multiple_of(grp * sub, sub)

        def row(ref, b, p, j, decay=False):
            tile = ref[b, pl.ds(base, sub), pl.ds(p * lanes, lanes)]
            tile = jnp.exp(tile) if decay else tile
            return jnp.broadcast_to(tile[j:j + 1], (n, lanes))

        o_rows = [[] for _ in combos]
        for j in range(sub):
            for c, (b, p) in enumerate(combos):
                lhs_a[c * n:(c + 1) * n, :] = (st_scr[c] * row(kn_ref, b, p, j)).astype(bf16)
                lhs_v[c * n:(c + 1) * n, :] = jnp.where(diag, row(v_ref, b, p, j), 0.0).astype(bf16)
            sa_all = _dot(lhs_a[...], ones_bd)
            vb_all = _dot(lhs_v[...], ones_bd)
            for c, (b, p) in enumerate(combos):
                s_new = (st_scr[c] * row(w_ref, b, p, j, decay=True)
                         + sa_all[c * n:(c + 1) * n] * row(b_ref, b, p, j)
                         + vb_all[c * n:(c + 1) * n] * row(k_ref, b, p, j))
                st_scr[c] = s_new
                lhs_o[c * n:(c + 1) * n, :] = (s_new * row(r_ref, b, p, j)).astype(bf16)
            o_all = _dot(lhs_o[...], ones_bd)
            for c, (b, p) in enumerate(combos):
                o_rows[c].append(jnp.sum(jnp.where(diag, o_all[c * n:(c + 1) * n], 0.0),
                                         axis=0, keepdims=True))
        for c, (b, p) in enumerate(combos):
            o_ref[b, pl.ds(base, sub), pl.ds(p * lanes, lanes)] = jnp.concatenate(o_rows[c], axis=0)
        return carry

    lax.fori_loop(0, tb // sub, token_group, 0)
    finish()


def _rw_chunk_kernel(*refs, nb, tb, pairs, has_state, has_prev):
    r_ref, w_ref, k_ref, v_ref, kn_ref, b_ref = refs[:6]
    s0_ref = refs[6] if has_state else None
    o_ref, s_ref, st_scr, coef_scr, x_scr = refs[6 + has_state + has_prev:]
    ti = pl.program_id(1)
    n = RW_N
    lanes = RW_PAIR * n
    ln = RW_CHUNK
    combos = [(b, p) for b in range(nb) for p in range(pairs)]
    finish = _rw_state_io(s0_ref, s_ref, st_scr, combos, has_state, ti, pl.num_programs(1) - 1)

    ones_bd = _group_ones(lanes, n)
    t_idx = lax.broadcasted_iota(jnp.int32, (ln, lanes), 0)
    lane_idx = lax.broadcasted_iota(jnp.int32, (ln, lanes), 1)
    row_i = lax.broadcasted_iota(jnp.int32, (4 * ln, 2 * ln), 0)
    col_i = lax.broadcasted_iota(jnp.int32, (4 * ln, 2 * ln), 1)
    same_head = (row_i % (2 * ln)) // ln == col_i // ln
    strict = jnp.where(row_i < 2 * ln, 1, 0)
    lower = jnp.where(same_head & (col_i % ln + strict <= row_i % ln), 1.0, 0.0)
    nt_dims = (((1,), (1,)), ((), ()))

    def by_head(x):
        head0 = lane_idx < n
        return jnp.concatenate([jnp.where(head0, x, 0.0), jnp.where(head0, 0.0, x)], axis=0)

    def stack_heads(x):
        return jnp.concatenate([x[:, :n], x[:, n:]], axis=0)

    def pack_heads(xs):
        return jnp.concatenate([xs[:ln], xs[ln:]], axis=-1)

    half = ln // 2
    solve_cols = list(range(half - 1)) + list(range(half, ln - 1))

    def chunk(ci, carry):
        base = pl.multiple_of(ci * ln, ln)
        pre = []
        for c, (b, p) in enumerate(combos):
            tile = lambda ref: ref[b, pl.ds(base, ln), pl.ds(p * lanes, lanes)].astype(f32)
            lw = tile(w_ref)
            cum = lw
            step = 1
            while step < ln:
                cum = cum + jnp.where(t_idx >= step, pltpu.roll(cum, step, 0), 0.0)
                step *= 2
            c_last = jnp.broadcast_to(cum[ln - 1:ln], (ln, lanes))
            inv_c = jnp.exp(-cum)
            to_end = jnp.exp(c_last - cum)
            kn, bb, kk, vv = tile(kn_ref), tile(b_ref), tile(k_ref), tile(v_ref)
            n_dec = kn * jnp.exp(cum - lw)
            b_dec = bb * inv_c
            k_dec = kk * inv_c
            r_dec = tile(r_ref) * jnp.exp(cum)
            lhs_nr = jnp.concatenate([by_head(n_dec), by_head(r_dec)], axis=0).astype(bf16)
            sc_b = lower * lax.dot_general(lhs_nr, by_head(b_dec).astype(bf16), nt_dims,
                                           preferred_element_type=f32)
            sc_k = lower * lax.dot_general(lhs_nr, by_head(k_dec).astype(bf16), nt_dims,
                                           preferred_element_type=f32)
            blocks = []
            for s in solve_cols:
                lo = 0 if s < half else half
                blocks.append(jnp.where(t_idx[lo:lo + half] > s,
                                        n_dec[lo:lo + half] * jnp.broadcast_to(b_dec[s:s + 1], (half, lanes)), 0.0))
            coef_scr[c] = _dot(jnp.concatenate(blocks, axis=0).astype(bf16), ones_bd)
            sc_cross = jnp.concatenate([sc_b[half:ln], sc_b[ln + half:2 * ln]], axis=0).astype(bf16)
            vs = stack_heads(vv)
            upd_v = _dot(vs.T.astype(bf16), by_head(kk * to_end).astype(bf16))
            pre.append((lhs_nr, sc_k[:2 * ln].astype(bf16), sc_b[2 * ln:].astype(bf16),
                        sc_k[2 * ln:].astype(bf16), vs.astype(bf16), by_head(bb * to_end).astype(bf16),
                        jnp.exp(cum[ln - 1:ln]), sc_cross, upd_v))

        gs = [lax.dot_general(pre[c][0], st_scr[c].astype(bf16), nt_dims, preferred_element_type=f32)
              for c in range(len(combos))]
        for c in range(len(combos)):
            x_scr[c] = pack_heads(gs[c][:2 * ln] + _dot(pre[c][1], pre[c][4]))

        def solve_half(lo, first_block):
            xh = [x_scr[c, lo:lo + half, :] for c in range(len(combos))]
            for si in range(half - 1):
                for c in range(len(combos)):
                    coef = coef_scr[c, (first_block + si) * half:(first_block + si + 1) * half, :]
                    xh[c] = xh[c] + coef * jnp.broadcast_to(xh[c][si:si + 1], (half, lanes))
            for c in range(len(combos)):
                x_scr[c, lo:lo + half, :] = xh[c]

        solve_half(0, 0)
        for c in range(len(combos)):
            x_top = jnp.concatenate([x_scr[c, :half, :], jnp.zeros((half, lanes), f32)], axis=0)
            cross = _dot(pre[c][7], stack_heads(x_top).astype(bf16))
            x_scr[c, half:, :] = x_scr[c, half:, :] + jnp.concatenate([cross[:half], cross[half:]], axis=-1)
        solve_half(half, half - 1)
        xss = [stack_heads(x_scr[c]) for c in range(len(combos))]
        xs_ts = [xs.T.astype(bf16) for xs in xss]
        for c, (b, p) in enumerate(combos):
            _, _, sc_b_out, sc_k_out, vs_bf, rhs_b_end, d_last, _, upd_v = pre[c]
            o = gs[c][2 * ln:] + _dot(sc_b_out, xss[c].astype(bf16)) + _dot(sc_k_out, vs_bf)
            o_ref[b, pl.ds(base, ln), pl.ds(p * lanes, lanes)] = pack_heads(o)
            st_scr[c] = st_scr[c] * d_last + _dot(xs_ts[c], rhs_b_end) + upd_v
        return carry

    lax.fori_loop(0, tb // ln, chunk, 0)
    finish()


def _rw_lanes_kernel(*refs, tlen, has_prev):
    r_ref, w_ref, k_ref, v_ref, kn_ref, b_ref, s0_ref = refs[:7]
    o_ref, s_ref, vec_scr, o_scr = refs[7 + has_prev:]
    n = RW_N
    bsz = V7X_LANES
    sub = V7X_SUBLANES
    s_ref[...] = s0_ref[...]
    for t in range(tlen):
        for slot, ref in enumerate((kn_ref, w_ref, b_ref, k_ref, r_ref, v_ref)):
            x = ref[pl.ds(t, bsz, stride=tlen), :].T
            vec_scr[slot] = jnp.exp(x) if slot == 1 else x

        def value_rows(ig, carry):
            i0 = pl.multiple_of(ig * sub, sub)
            for h in range(RW_PAIR):
                kn_h, w_h, b_h, k_h, r_h = (vec_scr[slot, h * n:(h + 1) * n, :] for slot in range(5))
                v_tile = vec_scr[5, pl.ds(h * n + i0, sub), :]
                o_rows = []
                for ii in range(sub):
                    s_old = s_ref[h, i0 + ii]
                    sa = jnp.sum(s_old * kn_h, axis=0, keepdims=True)
                    s_new = s_old * w_h + sa * b_h + v_tile[ii:ii + 1] * k_h
                    s_ref[h, i0 + ii] = s_new
                    o_rows.append(jnp.sum(s_new * r_h, axis=0, keepdims=True))
                o_scr[pl.ds(h * n + i0, sub), :] = jnp.concatenate(o_rows, axis=0)
            return carry

        lax.fori_loop(0, n // sub, value_rows, 0)
        o_ref[pl.ds(t, bsz, stride=tlen), :] = o_scr[...].T


def _rw_lanes_scan(seqs, s_in, s_prev, layer, depth, bsz, tlen):
    r, w, k, v, kn, bb = seqs
    width = r.shape[-1]
    heads = width // RW_N
    lanes = RW_PAIR * RW_N
    rows = bsz * tlen
    blk = pl.BlockSpec((rows, lanes), lambda p: (0, p))
    st_spec = pl.BlockSpec((None, RW_PAIR, RW_N, RW_N, bsz), lambda p: (layer, p, 0, 0, 0))
    in_specs = [blk] * 6 + [st_spec]
    args = [a.reshape(rows, width) for a in (r, w, k, v, kn, bb)] + [s_in]
    aliases = _threaded_state(args, in_specs, s_prev, 1)
    o, s_new = pl.pallas_call(
        functools.partial(_rw_lanes_kernel, tlen=tlen, has_prev=s_prev is not None),
        grid=(heads // RW_PAIR,),
        in_specs=in_specs,
        out_specs=[blk, st_spec],
        out_shape=[jax.ShapeDtypeStruct((rows, width), f32),
                   jax.ShapeDtypeStruct((depth, heads, RW_N, RW_N, bsz), f32)],
        scratch_shapes=[pltpu.VMEM((6, lanes, bsz), f32), pltpu.VMEM((lanes, bsz), f32)],
        input_output_aliases=aliases,
        compiler_params=_params("parallel"),
        name="rwkv7_lanes_scan",
    )(*args)
    return o.reshape(bsz, tlen, width), s_new


def _rw_mode(bsz, tlen):
    if _tile(tlen, 64) % RW_CHUNK == 0:
        return "chunk"
    return "lanes" if bsz == V7X_LANES else "token"


def _rw_scan(seqs, s_in, s_prev, layer, depth, bsz, tlen):
    if _rw_mode(bsz, tlen) == "lanes" and s_in is not None:
        return _rw_lanes_scan(seqs, s_in, s_prev, layer, depth, bsz, tlen)
    r, w, k, v, kn, bb = seqs
    width = r.shape[-1]
    heads = width // RW_N
    pairs = heads // RW_PAIR
    nb = _tile(bsz, 4, align=1)
    tb = _tile(tlen, 64)
    chunked = tb % RW_CHUNK == 0
    has_state = s_in is not None
    blk = pl.BlockSpec((nb, tb, width), lambda bi, ti: (bi, ti, 0))
    st_spec = pl.BlockSpec((None, nb, heads, RW_N, RW_N), lambda bi, ti: (layer, bi, 0, 0, 0))
    m = nb * pairs * RW_N
    lanes = RW_PAIR * RW_N
    in_specs = [blk] * 6 + ([st_spec] if has_state else [])
    args = [r, w, k, v, kn, bb] + ([s_in] if has_state else [])
    aliases = _threaded_state(args, in_specs, s_prev, 1)
    scratch = [pltpu.VMEM((nb * pairs, RW_N, lanes), f32)]
    if chunked:
        scratch += [pltpu.VMEM((nb * pairs, (RW_CHUNK - 2) * (RW_CHUNK // 2), lanes), f32),
                    pltpu.VMEM((nb * pairs, RW_CHUNK, lanes), f32)]
    else:
        scratch += [pltpu.VMEM((m, lanes), bf16)] * 3
    o, s_new = pl.pallas_call(
        functools.partial(_rw_chunk_kernel if chunked else _rw_scan_kernel, nb=nb, tb=tb, pairs=pairs,
                          has_state=has_state, has_prev=s_prev is not None),
        grid=(bsz // nb, tlen // tb),
        in_specs=in_specs,
        out_specs=[blk, st_spec],
        out_shape=[jax.ShapeDtypeStruct((bsz, tlen, width), f32),
                   jax.ShapeDtypeStruct((depth, bsz, heads, RW_N, RW_N), f32)],
        scratch_shapes=scratch,
        input_output_aliases=aliases,
        compiler_params=_params("parallel", "arbitrary"),
        name="rwkv7_chunk_scan" if chunked else "rwkv7_scan",
    )(*args)
    return o, s_new


def _rw_post_kernel(o_ref, r_ref, k_ref, v_ref, g_ref, lw_ref, lbias_ref, rk_ref, y_ref):
    ones_bd = _group_ones(V7X_LANES, RW_N)
    o = o_ref[...]
    mu = _group_sum(o, ones_bd) * (1.0 / RW_N)
    d = o - mu
    var = _group_sum(d * d, ones_bd) * (1.0 / RW_N)
    on = d * lax.rsqrt(var + RW_GN_EPS) * lw_ref[...] + lbias_ref[...]
    rk = r_ref[...].astype(f32) * k_ref[...].astype(f32) * rk_ref[...]
    bonus = _group_sum(rk, ones_bd) * v_ref[...].astype(f32)
    y_ref[...] = ((on + bonus) * g_ref[...].astype(f32)).astype(y_ref.dtype)


def _rw_post(o, r, k, v, g, ln_w, ln_b, r_k):
    n, width = o.shape
    tm = _tile(n, 512)
    blk = pl.BlockSpec((tm, width), lambda i: (i, 0))
    vec = pl.BlockSpec((1, width), lambda i: (0, 0))
    return pl.pallas_call(
        _rw_post_kernel,
        grid=(n // tm,),
        in_specs=[blk] * 5 + [vec] * 3,
        out_specs=blk,
        out_shape=jax.ShapeDtypeStruct((n, width), bf16),
        compiler_params=_params("parallel"),
        name="rwkv7_post",
    )(o, r, k, v, g, ln_w, ln_b, r_k)


def _conv_kernel(*refs, nb, tb, has_state, has_prev):
    sb_ref, sc_ref, sh_ref = refs[:3]
    st_ref = refs[3] if has_state else None
    cw_ref = refs[3 + has_state]
    y_ref, new_ref = refs[4 + has_state + has_prev:]
    ti = pl.program_id(2)
    rows = nb * tb
    w = sb_ref.shape[-1]
    hist = SC_KSIZE - 1

    @pl.when(ti == 0)
    def _():
        new_ref[...] = st_ref[...] if has_state else jnp.zeros((nb, hist, w), f32)

    u = (sc_ref[...].astype(f32) * sh_ref[...].astype(f32)).reshape(rows, w)
    carry = new_ref[...]
    conv = cw_ref[hist:hist + 1, :] * u
    for back in range(1, hist + 1):
        first = [_bcast_rows(carry[:, hist - back + j:hist - back + j + 1, :], nb, tb) for j in range(back)]
        conv = conv + cw_ref[hist - back:hist - back + 1, :] * _shift_rows(u, first, tb, back)
    y_ref[...] = (sb_ref[...].astype(f32).reshape(rows, w) * conv).reshape(nb, tb, w).astype(y_ref.dtype)
    new_ref[...] = u.reshape(nb, tb, w)[:, tb - hist:tb, :]


def _conv(p_sc, st_in, st_prev, layer, depth, conv_w, bsz, tlen):
    width = conv_w.shape[1]
    cw = _tile(width, 1024, align=V7X_LANES)
    ncol = width // cw
    nb, tb = _seq_block(bsz, tlen)
    assert tb >= SC_KSIZE - 1
    has_state = st_in is not None
    p3 = p_sc.reshape(bsz, tlen, 3 * width)
    seg = lambda s: pl.BlockSpec((nb, tb, cw), lambda bi, cj, ti, s=s: (bi, ti, s * ncol + cj))
    st_spec = pl.BlockSpec((None, nb, SC_KSIZE - 1, cw), lambda bi, cj, ti: (layer, bi, 0, cj))
    in_specs = ([seg(0), seg(1), seg(2)] + ([st_spec] if has_state else [])
                + [pl.BlockSpec((SC_KSIZE, cw), lambda bi, cj, ti: (0, cj))])
    args = [p3, p3, p3] + ([st_in] if has_state else []) + [conv_w]
    aliases = _threaded_state(args, in_specs, st_prev, 1)
    y, new = pl.pallas_call(
        functools.partial(_conv_kernel, nb=nb, tb=tb, has_state=has_state, has_prev=st_prev is not None),
        grid=(bsz // nb, ncol, tlen // tb),
        in_specs=in_specs,
        out_specs=[pl.BlockSpec((nb, tb, cw), lambda bi, cj, ti: (bi, ti, cj)), st_spec],
        out_shape=[jax.ShapeDtypeStruct((bsz, tlen, width), bf16),
                   jax.ShapeDtypeStruct((depth, bsz, SC_KSIZE - 1, width), f32)],
        input_output_aliases=aliases,
        compiler_params=_params("parallel", "parallel", "arbitrary"),
        name="short_conv",
    )(*args)
    return y.reshape(bsz * tlen, width), new


def _merge_kernel(a_ref, b_ref, c_ref, ga_ref, gb_ref, gc_ref, pa_ref, pb_ref, pc_ref, m_ref, w_scr):
    @pl.when(pl.program_id(1) == 0)
    def _():
        for s, w_ref in enumerate((pa_ref, pb_ref, pc_ref)):
            w_scr[s] = w_ref[...].astype(bf16)

    m = (_sigmoid(ga_ref[...].astype(f32)) * _dot(a_ref[...], w_scr[0])
         + _sigmoid(gb_ref[...].astype(f32)) * _dot(b_ref[...], w_scr[1])
         + _sigmoid(gc_ref[...].astype(f32)) * _dot(c_ref[...], w_scr[2]))
    m_ref[...] = m.astype(m_ref.dtype)


def _merge(ya, yb, yc, p_gate, w_pa, w_pb, w_pc, layer):
    n = ya.shape[0]
    width, d = w_pa.shape[1:]
    assert w_pb.shape[1] == width and w_pc.shape[1] == width
    tm = _tile(n, 1024)
    tn = _tile(d, 512, align=V7X_LANES)
    nj = d // tn
    act = pl.BlockSpec((tm, width), lambda j, i: (i, 0))
    gate = lambda s: pl.BlockSpec((tm, tn), lambda j, i, s=s: (i, s * nj + j))
    wt = pl.BlockSpec((None, width, tn), lambda j, i: (layer, 0, j))
    return pl.pallas_call(
        _merge_kernel,
        grid=(nj, n // tm),
        in_specs=[act, act, act, gate(0), gate(1), gate(2), wt, wt, wt],
        out_specs=pl.BlockSpec((tm, tn), lambda j, i: (i, j)),
        out_shape=jax.ShapeDtypeStruct((n, d), bf16),
        scratch_shapes=[pltpu.VMEM((3, width, tn), bf16)],
        compiler_params=_params("arbitrary", "arbitrary"),
        name="gated_merge",
    )(ya, yb, yc, p_gate, p_gate, p_gate, w_pa, w_pb, w_pc)


def _oproj_kernel(m_ref, wo_ref, x_ref, g1_ref, g2_ref, x1_ref, h2_ref):
    mix = _dot(m_ref[...], wo_ref[...])
    x1 = x_ref[...] + _rms(mix, g1_ref[...])
    x1_ref[...] = x1
    h2_ref[...] = _rms(x1, g2_ref[...]).astype(h2_ref.dtype)


def _oproj(m, w_o, layer, x, g_post_mix, g_pre_mlp):
    n, d = x.shape
    tm = _tile(n, 512)
    blk = pl.BlockSpec((tm, d), lambda i: (i, 0))
    vec = pl.BlockSpec((1, d), lambda i: (0, 0))
    return pl.pallas_call(
        _oproj_kernel,
        grid=(n // tm,),
        in_specs=[blk, pl.BlockSpec((None, d, d), lambda i: (layer, 0, 0)), blk, vec, vec],
        out_specs=[blk, blk],
        out_shape=[jax.ShapeDtypeStruct((n, d), f32), jax.ShapeDtypeStruct((n, d), bf16)],
        compiler_params=_params("parallel"),
        name="out_proj_norms",
    )(m, w_o, x, g_post_mix, g_pre_mlp)


def _mlp_kernel(h_ref, w1_ref, w2_ref, ff_ref):
    kf = pl.program_id(1)

    @pl.when(kf == 0)
    def _():
        ff_ref[...] = jnp.zeros_like(ff_ref)

    t = jnp.maximum(_dot(h_ref[...], w1_ref[...].astype(bf16)), 0.0)
    ff_ref[...] += _dot((t * t).astype(bf16), w2_ref[...].astype(bf16))


def _mlp(h2, w1, w2, layer):
    n, d = h2.shape
    dff = w1.shape[2]
    tm = _tile(n, 1024)
    fc = _tile(dff, 512, align=V7X_LANES)
    once = pl.Buffered(1)
    return pl.pallas_call(
        _mlp_kernel,
        grid=(n // tm, dff // fc),
        in_specs=[pl.BlockSpec((tm, d), lambda i, kf: (i, 0), pipeline_mode=once),
                  pl.BlockSpec((None, d, fc), lambda i, kf: (layer, 0, kf)),
                  pl.BlockSpec((None, fc, d), lambda i, kf: (layer, kf, 0))],
        out_specs=pl.BlockSpec((tm, d), lambda i, kf: (i, 0), pipeline_mode=once),
        out_shape=jax.ShapeDtypeStruct((n, d), f32),
        compiler_params=_params("parallel", "arbitrary"),
        name="relu2_mlp",
    )(h2, w1, w2)


def _residual_kernel(*refs, has_next):
    if has_next:
        x1_ref, ff_ref, g_ref, gn_ref, y_ref, h_ref = refs
    else:
        x1_ref, ff_ref, g_ref, y_ref = refs
    y = x1_ref[...] + _rms(ff_ref[...], g_ref[...])
    y_ref[...] = y
    if has_next:
        h_ref[...] = _rms(y, gn_ref[...]).astype(h_ref.dtype)


def _residual(x1, ff, g_post_mlp, g_next):
    n, d = x1.shape
    tm = _tile(n, 512)
    blk = pl.BlockSpec((tm, d), lambda i: (i, 0))
    vec = pl.BlockSpec((1, d), lambda i: (0, 0))
    has_next = g_next is not None
    outs = pl.pallas_call(
        functools.partial(_residual_kernel, has_next=has_next),
        grid=(n // tm,),
        in_specs=[blk, blk, vec] + ([vec] if has_next else []),
        out_specs=[blk] + ([blk] if has_next else []),
        out_shape=[jax.ShapeDtypeStruct((n, d), f32)] + ([jax.ShapeDtypeStruct((n, d), bf16)] if has_next else []),
        compiler_params=_params("parallel"),
        name="mlp_residual_norm",
    )(*([x1, ff, g_post_mlp] + ([g_next] if has_next else [])))
    return (outs[0], outs[1]) if has_next else (outs[0], None)


def _layer(x, h, bsz, tlen, states_in, states_prev, layer, depth, wts, lw, g_next):
    hg_in, rw_in, shift_in, sc_in = states_in
    hg_prev, rw_prev, shift_prev, sc_prev = states_prev
    w_in, offs = wts["w_in"], wts["offs"]
    p_hg = _proj(h, w_in, layer, offs[0], offs[1] - offs[0], "in_proj_hgrn")
    p_rkv = _proj(h, w_in, layer, offs[1], offs[2] - offs[1], "in_proj_rwkv")
    p_lora = _proj(h, w_in, layer, offs[2], offs[3] - offs[2], "in_proj_rwkv_lora")
    p_sc = _proj(h, w_in, layer, offs[3], offs[4] - offs[3], "in_proj_conv", bf16)
    p_gate = _proj(h, w_in, layer, offs[4], offs[5] - offs[4], "in_proj_gate", bf16)

    ya, hg_new = _hgrn(p_hg, lw["lb"], lw["hg_norm"], hg_in, hg_prev, layer, depth, bsz, tlen)

    seq_dtype = bf16 if _rw_mode(bsz, tlen) == "chunk" else f32
    seqs, shift_new = _rw_prep(p_rkv, p_lora, shift_in, shift_prev, layer, depth, lw["rw_prep"], bsz, tlen,
                               seq_dtype)
    r, w, k, v, kn, bb, g = seqs
    o_rw, rw_new = _rw_scan((r, w, k, v, kn, bb), rw_in, rw_prev, layer, depth, bsz, tlen)
    flat = lambda t: t.reshape(bsz * tlen, t.shape[-1])
    yb = _rw_post(flat(o_rw), flat(r), flat(k), flat(v), flat(g), lw["rw_ln_w"], lw["rw_ln_b"], lw["rw_r_k"])

    yc, sc_new = _conv(p_sc, sc_in, sc_prev, layer, depth, lw["sc_conv_w"], bsz, tlen)

    m = _merge(ya, yb, yc, p_gate, wts["w_pa"], wts["w_pb"], wts["w_pc"], layer)
    x1, h2 = _oproj(m, wts["w_o"], layer, x, lw["g_post_mix"], lw["g_pre_mlp"])
    ff = _mlp(h2, wts["w_ff1"], wts["w_ff2"], layer)
    y, h_next = _residual(x1, ff, lw["g_post_mlp"], g_next)
    return y, h_next, (hg_new, rw_new, shift_new, sc_new)


def kernel(x_prompt, x_sample, state_hgrn, state_rwkv, state_rwkv_shift, state_conv, norm_pre_mix, norm_post_mix, norm_pre_mlp, norm_post_mlp, w_in, hg_lb_logits, hg_norm, w_pa, rw_mu, rw_w0, rw_w2, rw_a0, rw_a2, rw_g2, rw_k_k, rw_k_a, rw_r_k, rw_ln_w, rw_ln_b, w_pb, sc_conv_w, w_pc, w_o, w_ff1, w_ff2):
    depth = w_in.shape[0]
    d_model = x_prompt.shape[-1]
    hg_width = hg_norm.shape[1]
    rw_width = rw_w0.shape[1]
    rw_shift_width = rw_mu.shape[1]
    sc_width = sc_conv_w.shape[2]
    off_rw = 4 * hg_width
    off_lora = off_rw + 3 * rw_width
    off_sc = off_rw + rw_shift_width
    off_gate = off_sc + 3 * sc_width
    assert w_in.shape[2] == off_gate + 3 * d_model

    lb_all = _lower_bounds(hg_lb_logits.astype(f32))
    row = lambda a, l: a[l].reshape(1, -1).astype(f32)
    wts = {"w_in": w_in.astype(f32), "offs": (0, off_rw, off_lora, off_sc, off_gate, w_in.shape[2]),
           "w_pa": w_pa.astype(f32), "w_pb": w_pb.astype(f32), "w_pc": w_pc.astype(f32),
           "w_o": w_o.astype(bf16), "w_ff1": w_ff1.astype(f32), "w_ff2": w_ff2.astype(f32)}

    bp, tp, _ = x_prompt.shape
    bs, ts, _ = x_sample.shape
    yp = x_prompt.reshape(bp * tp, d_model).astype(f32)
    ys = x_sample.reshape(bs * ts, d_model).astype(f32)
    hp = _rmsnorm_cast(yp, row(norm_pre_mix, 0))
    hs = _rmsnorm_cast(ys, row(norm_pre_mix, 0))
    rw_lanes = _rw_mode(bs, ts) == "lanes"
    rw_state = state_rwkv.astype(f32)
    sample_in = (state_hgrn.astype(f32), jnp.transpose(rw_state, (0, 2, 3, 4, 1)) if rw_lanes else rw_state,
                 state_rwkv_shift.astype(f32).reshape(depth, bs, 1, rw_shift_width), state_conv.astype(f32))
    new_p = new_s = (None, None, None, None)
    for l in range(depth):
        lw = {
            "g_post_mix": row(norm_post_mix, l), "g_pre_mlp": row(norm_pre_mlp, l),
            "g_post_mlp": row(norm_post_mlp, l),
            "lb": lb_all[l:l + 1], "hg_norm": row(hg_norm, l),
            "rw_prep": (row(rw_mu, l), row(rw_w0, l), rw_w2[l].astype(bf16), row(rw_a0, l), rw_a2[l].astype(bf16),
                        rw_g2[l].astype(bf16), row(rw_k_k, l), row(rw_k_a, l)),
            "rw_ln_w": row(rw_ln_w, l), "rw_ln_b": row(rw_ln_b, l), "rw_r_k": row(rw_r_k, l),
            "sc_conv_w": sc_conv_w[l].astype(f32),
        }
        g_next = row(norm_pre_mix, l + 1) if l + 1 < depth else None
        yp, hp, new_p = _layer(yp, hp, bp, tp, (None, None, None, None), new_p, l, depth, wts, lw, g_next)
        ys, hs, new_s = _layer(ys, hs, bs, ts, sample_in, new_s, l, depth, wts, lw, g_next)
    shift = lambda s, b: s.reshape(depth, b, rw_shift_width)
    return (yp.reshape(bp, tp, d_model), ys.reshape(bs, ts, d_model),
            new_p[0], new_p[1], shift(new_p[2], bp), new_p[3],
            new_s[0], jnp.transpose(new_s[1], (0, 4, 1, 2, 3)) if rw_lanes else new_s[1],
            shift(new_s[2], bs), new_s[3])
```

```python
import functools

import jax
import jax.numpy as jnp
from jax import lax
from jax.experimental import pallas as pl
from jax.experimental.pallas import tpu as pltpu

HG_DK = 128
RW_N = 64
HG_F_MIN = 1e-30
LOG2_E = 1.4426950408889634
RW_GN_EPS = 64e-5
NORM_EPS = 1e-6
SC_KSIZE = 3

V7X_LANES = 128
V7X_SUBLANES = 8
V7X_BF16_ROWS = 16
V7X_VMEM_LIMIT_BYTES = 48 * 1024 * 1024
V7X_VMEM_BIG_TILE_BYTES = 56 * 1024 * 1024
HG_CHUNK = 16
HG_HEADS_PER_STEP = 4
HG_GROUP = 4
RW_PAIR = 2
RW_CHUNK = 16

f32 = jnp.float32
bf16 = jnp.bfloat16


def _tile(n, pref, align=V7X_SUBLANES):
    if n <= pref:
        return n
    for d in range(pref, 0, -1):
        if n % d == 0 and d % align == 0:
            return d
    return n


def _seq_block(bsz, tlen):
    if tlen >= 256:
        return 1, _tile(tlen, 256)
    return _tile(bsz, max(1, 128 // tlen), align=1), tlen


def _params(*sem, vmem=V7X_VMEM_LIMIT_BYTES):
    return pltpu.CompilerParams(dimension_semantics=sem, vmem_limit_bytes=vmem)


def _sigmoid(x):
    return 1.0 / (1.0 + jnp.exp(-x))


def _dot(a, b):
    return jnp.dot(a, b, preferred_element_type=f32)


def _rms(x, g):
    return x * lax.rsqrt(jnp.mean(x * x, axis=-1, keepdims=True) + NORM_EPS) * g


def _group_ones(width, group):
    r = lax.broadcasted_iota(jnp.int32, (width, width), 0) // group
    c = lax.broadcasted_iota(jnp.int32, (width, width), 1) // group
    return jnp.where(r == c, 1.0, 0.0).astype(bf16)


def _group_sum(z, ones_bd):
    outs = []
    for c in range(z.shape[-1] // V7X_LANES):
        zc = z[:, c * V7X_LANES:(c + 1) * V7X_LANES]
        hi = zc.astype(bf16)
        lo = (zc - hi.astype(f32)).astype(bf16)
        outs.append(_dot(hi, ones_bd) + _dot(lo, ones_bd))
    return outs[0] if len(outs) == 1 else jnp.concatenate(outs, axis=-1)


def _threaded_state(args, in_specs, prev_out, out_index):
    if prev_out is None:
        return {}
    args.append(prev_out)
    in_specs.append(pl.BlockSpec(memory_space=pl.ANY))
    return {len(args) - 1: out_index}


def _lb_kernel(logit_ref, lb_ref):
    z = logit_ref[...]
    depth = z.shape[0]
    m = jnp.max(z, axis=0, keepdims=True)
    e = jnp.exp(z - m)
    p = e / jnp.sum(e, axis=0, keepdims=True)
    acc = jnp.zeros_like(p[0:1])
    for l in range(depth):
        acc = acc + p[l:l + 1]
        lb_ref[l:l + 1, :] = jnp.clip(acc - p[0:1], 0.0, 1.0)


def _lower_bounds(logits):
    return pl.pallas_call(
        _lb_kernel, out_shape=jax.ShapeDtypeStruct(logits.shape, f32), name="hg_lower_bounds",
    )(logits)


def _rmsnorm_kernel(x_ref, g_ref, o_ref):
    o_ref[...] = _rms(x_ref[...], g_ref[...]).astype(o_ref.dtype)


def _rmsnorm_cast(x, g):
    n, d = x.shape
    tm = _tile(n, 512)
    return pl.pallas_call(
        _rmsnorm_kernel,
        grid=(n // tm,),
        in_specs=[pl.BlockSpec((tm, d), lambda i: (i, 0)), pl.BlockSpec((1, d), lambda i: (0, 0))],
        out_specs=pl.BlockSpec((tm, d), lambda i: (i, 0)),
        out_shape=jax.ShapeDtypeStruct((n, d), bf16),
        compiler_params=_params("parallel"),
        name="rmsnorm_cast",
    )(x, g)


def _proj_kernel(a_ref, w_ref, o_ref, wbf_scr):
    @pl.when(pl.program_id(1) == 0)
    def _():
        wbf_scr[...] = w_ref[0].astype(bf16)

    o_ref[...] = _dot(a_ref[...], wbf_scr[...]).astype(o_ref.dtype)


def _proj(h, w_stack, layer, off, width, name, out_dtype=f32):
    n, k = h.shape
    big = out_dtype == bf16
    tm = _tile(n, 2048 if big else 1024)
    tn = _tile(width, 1024, align=V7X_LANES)
    w_spec = pl.BlockSpec((pl.Element(1), pl.Element(k), pl.Element(tn)),
                          lambda j, i: (layer, 0, pl.multiple_of(off + j * tn, V7X_LANES)))
    return pl.pallas_call(
        _proj_kernel,
        grid=(width // tn, n // tm),
        in_specs=[pl.BlockSpec((tm, k), lambda j, i: (i, 0)), w_spec],
        out_specs=pl.BlockSpec((tm, tn), lambda j, i: (i, j)),
        out_shape=jax.ShapeDtypeStruct((n, width), out_dtype),
        scratch_shapes=[pltpu.VMEM((k, tn), bf16)],
        compiler_params=_params("arbitrary", "arbitrary",
                                vmem=V7X_VMEM_BIG_TILE_BYTES if big else V7X_VMEM_LIMIT_BYTES),
        name=name,
    )(h, w_stack)


def _hgrn_kernel(*refs, nb, tb, hb, chunk, has_state, has_prev):
    q_ref, f_ref, i_ref, og_ref, lb_ref, gn_ref, rsel_ref = refs[:7]
    s0_ref = refs[7] if has_state else None
    o_ref, s_ref, st_scr = refs[7 + has_state + has_prev:]
    ti = pl.program_id(2)
    rows = nb * tb
    dk = HG_DK
    nchunk = rows // chunk
    heads = range(hb)

    @pl.when(ti == 0)
    def _():
        for b in range(nb):
            for h in heads:
                st_scr[b, h] = s0_ref[b, h].T if has_state else jnp.zeros((dk, dk), f32)

    grp = HG_GROUP if (tb // chunk) % HG_GROUP == 0 else 1
    row_idx = lax.broadcasted_iota(jnp.int32, (rows, dk), 0)
    t_idx = row_idx % chunk
    pos_in_grp = (row_idx // chunk) % grp
    chunk_r = lax.broadcasted_iota(jnp.int32, (rows, rows), 0) // chunk
    chunk_c = lax.broadcasted_iota(jnp.int32, (rows, rows), 1) // chunk
    chunk_masks = [jnp.where((chunk_r - chunk_c == dist) & (chunk_r // grp == chunk_c // grp), 1.0, 0.0)
                   for dist in range(grp)]
    ngroup = nchunk // grp
    col_group = lax.broadcasted_iota(jnp.int32, (dk, rows), 1) // (grp * chunk)

    def chunk_row(a, s):
        a3 = a.reshape(nchunk, chunk, dk)
        return jnp.broadcast_to(a3[:, s:s + 1, :], (nchunk, chunk, dk)).reshape(rows, dk)

    def head_cols(ref, h):
        return ref[:, :, h * dk:(h + 1) * dk].reshape(rows, dk)

    sub = V7X_SUBLANES
    tpc = chunk // sub

    def tiles(a):
        return a.reshape(nchunk, tpc, sub, dk)

    glogs, q_decs, o_intras, u_alls = [], [], [], []
    for h in heads:
        lb = lb_ref[:, h * dk:(h + 1) * dk]
        hq = head_cols(q_ref, h)
        qq = hq * _sigmoid(hq)
        f = lb + (1.0 - lb) * _sigmoid(head_cols(f_ref, h))
        kk = 1.0 - f
        bcum = jnp.log(jnp.maximum(f, HG_F_MIN))
        step = 1
        while step < chunk:
            bcum = bcum + jnp.where(t_idx >= step, pltpu.roll(bcum, step, 0), 0.0)
            step *= 2
        b_last = chunk_row(bcum, chunk - 1)
        zs = []
        b2 = bcum * LOG2_E
        for s in range(chunk):
            t0 = s // sub

            def col_row(a):
                return jnp.broadcast_to(tiles(a)[:, t0:t0 + 1, s % sub:s % sub + 1, :], (nchunk, 1, sub, dk))

            b2_s, k_s = col_row(b2), col_row(kk)
            parts = [jnp.zeros((nchunk, t0, sub, dk), f32)] if t0 else []
            for tt in range(t0, tpc):
                diff = tiles(b2)[:, tt:tt + 1] - b2_s
                qk = tiles(qq)[:, tt:tt + 1] * k_s
                if tt == t0:
                    z = jnp.where(tiles(t_idx)[:, tt:tt + 1] >= s, qk * jnp.exp2(jnp.minimum(diff, 0.0)), 0.0)
                else:
                    z = qk * jnp.exp2(diff)
                parts.append(z)
            z = parts[0] if len(parts) == 1 else jnp.concatenate(parts, axis=1)
            zs.append(z.reshape(rows, dk).astype(bf16))
        k_dec =(kk * jnp.exp(b_last - bcum)).astype(bf16)
        vv = head_cols(i_ref, h)
        a_rep = _dot(jnp.concatenate(zs, axis=-1), rsel_ref[...])
        scores = a_rep * chunk_masks[0]
        between = jnp.zeros_like(bcum)
        for dist in range(1, grp):
            q_far = (qq * jnp.exp(bcum + between)).astype(bf16)
            far = lax.dot_general(q_far, k_dec, (((1,), (1,)), ((), ())), preferred_element_type=f32)
            scores = scores + far * chunk_masks[dist]
            between = between + pltpu.roll(b_last, dist * chunk, 0)
        reach = jnp.zeros_like(bcum)
        tail = jnp.zeros_like(bcum)
        for dist in range(1, grp):
            reach = reach + jnp.where(pos_in_grp >= dist, pltpu.roll(b_last, dist * chunk, 0), 0.0)
            tail = tail + jnp.where(pos_in_grp + dist < grp, pltpu.roll(b_last, rows - dist * chunk, 0), 0.0)
        glogs.append(bcum + reach)
        q_decs.append((qq * jnp.exp(bcum + reach)).astype(bf16))
        o_intras.append(_dot(scores.astype(bf16), vv.astype(bf16)))
        k_grp = k_dec if grp == 1 else (kk * jnp.exp(b_last - bcum + tail)).astype(bf16)
        v_t = vv.T
        lhs = jnp.concatenate([jnp.where(col_group == g, v_t, 0.0).astype(bf16) for g in range(ngroup)], axis=0)
        u_alls.append(_dot(lhs, k_grp))

    outs = [[] for _ in heads]
    span = grp * chunk
    for b in range(nb):
        sts = [st_scr[b, h] for h in heads]
        for g in range(tb // span):
            gi = b * (tb // span) + g
            r0 = gi * span
            for h in heads:
                inter = lax.dot_general(q_decs[h][r0:r0 + span], sts[h].astype(bf16),
                                        (((1,), (1,)), ((), ())), preferred_element_type=f32)
                outs[h].append(inter + o_intras[h][r0:r0 + span])
                sts[h] = jnp.exp(glogs[h][r0 + span - 1:r0 + span]) * sts[h] + u_alls[h][gi * dk:(gi + 1) * dk]
        for h in heads:
            st_scr[b, h] = sts[h]
    for h in heads:
        o = jnp.concatenate(outs[h], axis=0)
        o = o * lax.rsqrt(jnp.mean(o * o, axis=-1, keepdims=True) + NORM_EPS)
        hog = head_cols(og_ref, h)
        o = o * gn_ref[:, h * dk:(h + 1) * dk] * (hog * _sigmoid(hog))
        o_ref[:, :, h * dk:(h + 1) * dk] = o.reshape(nb, tb, dk).astype(o_ref.dtype)

    @pl.when(ti == pl.num_programs(2) - 1)
    def _():
        for b in range(nb):
            for h in heads:
                s_ref[b, h] = st_scr[b, h].T


def _hgrn(p_hg, lb, gn, s_in, s_prev, layer, depth, bsz, tlen):
    width = p_hg.shape[1] // 4
    heads = width // HG_DK
    chunk = min(HG_CHUNK, tlen)
    nb, tb = _seq_block(bsz, tlen)
    assert tb % chunk == 0
    p3 = p_hg.reshape(bsz, tlen, 4 * width)
    hb = HG_HEADS_PER_STEP if heads % HG_HEADS_PER_STEP == 0 else 1
    hgroups = heads // hb
    wb = hb * HG_DK
    seg = lambda s: pl.BlockSpec((nb, tb, wb), lambda bi, h, ti, s=s: (bi, ti, s * hgroups + h))
    vec = pl.BlockSpec((1, wb), lambda bi, h, ti: (0, h))
    st_spec = pl.BlockSpec((None, nb, hb, HG_DK, HG_DK), lambda bi, h, ti: (layer, bi, h, 0, 0))
    has_state = s_in is not None
    rows = nb * tb
    rsel = (jnp.arange(chunk * HG_DK)[:, None] // HG_DK == jnp.arange(rows)[None, :] % chunk).astype(bf16)
    rsel_spec = pl.BlockSpec((chunk * HG_DK, rows), lambda bi, h, ti: (0, 0))
    in_specs = [seg(0), seg(1), seg(2), seg(3), vec, vec, rsel_spec] + ([st_spec] if has_state else [])
    args = [p3, p3, p3, p3, lb, gn, rsel] + ([s_in] if has_state else [])
    aliases = _threaded_state(args, in_specs, s_prev, 1)
    o, s_new = pl.pallas_call(
        functools.partial(_hgrn_kernel, nb=nb, tb=tb, hb=hb, chunk=chunk, has_state=has_state,
                          has_prev=s_prev is not None),
        grid=(bsz // nb, hgroups, tlen // tb),
        in_specs=in_specs,
        out_specs=[pl.BlockSpec((nb, tb, wb), lambda bi, h, ti: (bi, ti, h)), st_spec],
        out_shape=[jax.ShapeDtypeStruct((bsz, tlen, width), bf16),
                   jax.ShapeDtypeStruct((depth, bsz, heads, HG_DK, HG_DK), f32)],
        scratch_shapes=[pltpu.VMEM((nb, hb, HG_DK, HG_DK), f32)],
        input_output_aliases=aliases,
        compiler_params=_params("parallel", "parallel", "arbitrary"),
        name="hgrn2_scan",
    )(*args)
    return o.reshape(bsz * tlen, width), s_new


def _shift_rows(x, first, tb, by):
    rows = x.shape[0]
    t_idx = lax.broadcasted_iota(jnp.int32, (rows, 1), 0) % tb
    out = pltpu.roll(x, by, 0) if tb > by else x
    for j in range(by):
        out = jnp.where(t_idx == j, first[j], out)
    return out


def _bcast_rows(v3, nb, tb):
    w = v3.shape[-1]
    return jnp.broadcast_to(v3, (nb, tb, w)).reshape(nb * tb, w)


def _rw_prep_kernel(*refs, nb, tb, width, lora, has_state, has_prev):
    pm_ref, pl_ref = refs[:2]
    sh_ref = refs[2] if has_state else None
    mu_ref, w0_ref, w2_ref, a0_ref, a2_ref, g2_ref, kk_ref, ka_ref = refs[2 + has_state:10 + has_state]
    r_ref, w_ref, k_ref, v_ref, kn_ref, b_ref, g_ref, last_ref = refs[10 + has_state + has_prev:]
    ti = pl.program_id(1)
    rows = nb * tb
    tot = last_ref.shape[-1]
    x = jnp.concatenate([pm_ref[...], pl_ref[...]], axis=-1).reshape(rows, tot)

    @pl.when(ti == 0)
    def _():
        last_ref[...] = sh_ref[...] if has_state else jnp.zeros((nb, 1, tot), f32)

    prev = _shift_rows(x, [_bcast_rows(last_ref[...], nb, tb)], tb, 1)
    last_ref[...] = jnp.concatenate([pm_ref[:, tb - 1:tb, :], pl_ref[:, tb - 1:tb, :]], axis=-1)
    xs = x + (prev - x) * mu_ref[...]
    dl, al, gl = lora
    c = 3 * width
    r, kr, vr = xs[:, :width], xs[:, width:2 * width], xs[:, 2 * width:c]
    wd, ad, gd = xs[:, c:c + dl], xs[:, c + dl:c + dl + al], xs[:, c + dl + al:c + dl + al + gl]
    z = -(w0_ref[...] + _dot(jnp.tanh(wd).astype(bf16), w2_ref[...]))
    softplus = jnp.maximum(z, 0.0) + jnp.log(1.0 + jnp.exp(-jnp.abs(z)))
    w_log = -softplus - 0.5
    log_decay = -jnp.exp(w_log)
    a = _sigmoid(a0_ref[...] + _dot(ad.astype(bf16), a2_ref[...]))
    g = _dot(_sigmoid(gd).astype(bf16), g2_ref[...])
    kk = kr * kk_ref[...]
    ones_bd = _group_ones(V7X_LANES, RW_N)
    nrm = jnp.sqrt(_group_sum(kk * kk, ones_bd))
    kk = kk / jnp.maximum(nrm, 1e-12)
    k = kr * (1.0 + (a - 1.0) * ka_ref[...])
    shp = (nb, tb, width)
    r_ref[...] = r.reshape(shp).astype(r_ref.dtype)
    w_ref[...] = log_decay.reshape(shp)
    k_ref[...] = k.reshape(shp).astype(k_ref.dtype)
    v_ref[...] = vr.reshape(shp).astype(v_ref.dtype)
    kn_ref[...] = (-kk).reshape(shp).astype(kn_ref.dtype)
    b_ref[...] = (kk * a).reshape(shp).astype(b_ref.dtype)
    g_ref[...] = g.reshape(shp).astype(g_ref.dtype)


def _rw_prep(p_main, p_lora, shift_in, shift_prev, layer, depth, wts, bsz, tlen, seq_dtype):
    mu, w0, w2, a0, a2, g2, k_k, k_a = wts
    width = w0.shape[1]
    wl = p_lora.shape[1]
    tot = 3 * width + wl
    lora = (w2.shape[0], a2.shape[0], g2.shape[0])
    nb, tb = _seq_block(bsz, tlen)
    has_state = shift_in is not None
    blk = lambda w: pl.BlockSpec((nb, tb, w), lambda bi, ti: (bi, ti, 0))
    full = lambda arr: pl.BlockSpec(arr.shape, lambda bi, ti: (0,) * arr.ndim)
    last_spec = pl.BlockSpec((None, nb, 1, tot), lambda bi, ti: (layer, bi, 0, 0))
    small = [mu, w0, w2, a0, a2, g2, k_k, k_a]
    in_specs = [blk(3 * width), blk(wl)] + ([last_spec] if has_state else []) + [full(s) for s in small]
    args = ([p_main.reshape(bsz, tlen, 3 * width), p_lora.reshape(bsz, tlen, wl)]
            + ([shift_in] if has_state else []) + small)
    aliases = _threaded_state(args, in_specs, shift_prev, 7)
    seq = lambda i: jax.ShapeDtypeStruct((bsz, tlen, width), f32 if i == 1 else seq_dtype)
    outs = pl.pallas_call(
        functools.partial(_rw_prep_kernel, nb=nb, tb=tb, width=width, lora=lora, has_state=has_state,
                          has_prev=shift_prev is not None),
        grid=(bsz // nb, tlen // tb),
        in_specs=in_specs,
        out_specs=[blk(width)] * 7 + [last_spec],
        out_shape=[seq(i) for i in range(7)] + [jax.ShapeDtypeStruct((depth, bsz, 1, tot), f32)],
        input_output_aliases=aliases,
        compiler_params=_params("parallel", "arbitrary"),
        name="rwkv7_prep",
    )(*args)
    return outs[:7], outs[7]


def _rw_state_io(s0_ref, s_ref, st_scr, combos, has_state, ti, last):
    n = RW_N

    @pl.when(ti == 0)
    def _():
        for c, (b, p) in enumerate(combos):
            if has_state:
                st_scr[c] = jnp.concatenate([s0_ref[b, RW_PAIR * p + h] for h in range(RW_PAIR)], axis=-1)
            else:
                st_scr[c] = jnp.zeros((n, RW_PAIR * n), f32)

    def finish():
        @pl.when(ti == last)
        def _():
            for c, (b, p) in enumerate(combos):
                s = st_scr[c]
                for h in range(RW_PAIR):
                    s_ref[b, RW_PAIR * p + h] = s[:, h * n:(h + 1) * n]

    return finish


def _rw_scan_kernel(*refs, nb, tb, pairs, has_state, has_prev):
    r_ref, w_ref, k_ref, v_ref, kn_ref, b_ref = refs[:6]
    s0_ref = refs[6] if has_state else None
    o_ref, s_ref, st_scr, lhs_a, lhs_v, lhs_o = refs[6 + has_state + has_prev:]
    ti = pl.program_id(1)
    n = RW_N
    lanes = RW_PAIR * n
    sub = V7X_SUBLANES
    combos = [(b, p) for b in range(nb) for p in range(pairs)]
    finish = _rw_state_io(s0_ref, s_ref, st_scr, combos, has_state, ti, pl.num_programs(1) - 1)

    ones_bd = _group_ones(lanes, n)
    diag = (lax.broadcasted_iota(jnp.int32, (n, lanes), 0)
            == lax.broadcasted_iota(jnp.int32, (n, lanes), 1) % n)

    def token_group(grp, carry):
        base = pl.multiple_of(grp * sub, sub)

        def row(ref, b, p, j, decay=False):
            tile = ref[b, pl.ds(base, sub), pl.ds(p * lanes, lanes)]
            tile = jnp.exp(tile) if decay else tile
            return jnp.broadcast_to(tile[j:j + 1], (n, lanes))

        o_rows = [[] for _ in combos]
        for j in range(sub):
            for c, (b, p) in enumerate(combos):
                lhs_a[c * n:(c + 1) * n, :] = (st_scr[c] * row(kn_ref, b, p, j)).astype(bf16)
                lhs_v[c * n:(c + 1) * n, :] = jnp.where(diag, row(v_ref, b, p, j), 0.0).astype(bf16)
            sa_all = _dot(lhs_a[...], ones_bd)
            vb_all = _dot(lhs_v[...], ones_bd)
            for c, (b, p) in enumerate(combos):
                s_new = (st_scr[c] * row(w_ref, b, p, j, decay=True)
                         + sa_all[c * n:(c + 1) * n] * row(b_ref, b, p, j)
                         + vb_all[c * n:(c + 1) * n] * row(k_ref, b, p, j))
                st_scr[c] = s_new
                lhs_o[c * n:(c + 1) * n, :] = (s_new * row(r_ref, b, p, j)).astype(bf16)
            o_all = _dot(lhs_o[...], ones_bd)
            for c, (b, p) in enumerate(combos):
                o_rows[c].append(jnp.sum(jnp.where(diag, o_all[c * n:(c + 1) * n], 0.0),
                                         axis=0, keepdims=True))
        for c, (b, p) in enumerate(combos):
            o_ref[b, pl.ds(base, sub), pl.ds(p * lanes, lanes)] = jnp.concatenate(o_rows[c], axis=0)
        return carry

    lax.fori_loop(0, tb // sub, token_group, 0)
    finish()


def _rw_chunk_kernel(*refs, nb, tb, pairs, has_state, has_prev):
    r_ref, w_ref, k_ref, v_ref, kn_ref, b_ref = refs[:6]
    s0_ref = refs[6] if has_state else None
    o_ref, s_ref, st_scr, coef_scr, x_scr = refs[6 + has_state + has_prev:]
    ti = pl.program_id(1)
    n = RW_N
    lanes = RW_PAIR * n
    ln = RW_CHUNK
    combos = [(b, p) for b in range(nb) for p in range(pairs)]
    finish = _rw_state_io(s0_ref, s_ref, st_scr, combos, has_state, ti, pl.num_programs(1) - 1)

    ones_bd = _group_ones(lanes, n)
    t_idx = lax.broadcasted_iota(jnp.int32, (ln, lanes), 0)
    lane_idx = lax.broadcasted_iota(jnp.int32, (ln, lanes), 1)
    row_i = lax.broadcasted_iota(jnp.int32, (4 * ln, 4 * ln), 0)
    col_i = lax.broadcasted_iota(jnp.int32, (4 * ln, 4 * ln), 1)
    same_head = (row_i % (2 * ln)) // ln == (col_i % (2 * ln)) // ln
    strict = jnp.where(row_i < 2 * ln, 1, 0)
    lower2 = jnp.where(same_head & (col_i % ln + strict <= row_i % ln), 1.0, 0.0)
    nt_dims = (((1,), (1,)), ((), ()))

    def by_head(x):
        head0 = lane_idx < n
        return jnp.concatenate([jnp.where(head0, x, 0.0), jnp.where(head0, 0.0, x)], axis=0)

    def stack_heads(x):
        return jnp.concatenate([x[:, :n], x[:, n:]], axis=0)

    def pack_heads(xs):
        return jnp.concatenate([xs[:ln], xs[ln:]], axis=-1)

    half = ln // 2
    solve_cols = list(range(half - 1)) + list(range(half, ln - 1))

    def chunk(ci, carry):
        base = pl.multiple_of(ci * ln, ln)
        pre = []
        for c, (b, p) in enumerate(combos):
            tile = lambda ref: ref[b, pl.ds(base, ln), pl.ds(p * lanes, lanes)].astype(f32)
            lw = tile(w_ref)
            cum = lw
            step = 1
            while step < ln:
                cum = cum + jnp.where(t_idx >= step, pltpu.roll(cum, step, 0), 0.0)
                step *= 2
            c_last = jnp.broadcast_to(cum[ln - 1:ln], (ln, lanes))
            inv_c = jnp.exp(-cum)
            to_end = jnp.exp(c_last - cum)
            kn, bb, kk, vv = tile(kn_ref), tile(b_ref), tile(k_ref), tile(v_ref)
            n_dec = kn * jnp.exp(cum - lw)
            b_dec = bb * inv_c
            k_dec = kk * inv_c
            r_dec = tile(r_ref) * jnp.exp(cum)
            lhs_nr = jnp.concatenate([by_head(n_dec), by_head(r_dec)], axis=0).astype(bf16)
            rhs_bk = jnp.concatenate([by_head(b_dec), by_head(k_dec)], axis=0).astype(bf16)
            sc = (lower2 * lax.dot_general(lhs_nr, rhs_bk, nt_dims, preferred_element_type=f32)).astype(bf16)
            blocks = []
            for s in solve_cols:
                lo = 0 if s < half else half
                blocks.append(jnp.where(t_idx[lo:lo + half] > s,
                                        n_dec[lo:lo + half] * jnp.broadcast_to(b_dec[s:s + 1], (half, lanes)), 0.0))
            coef_scr[c] = _dot(jnp.concatenate(blocks, axis=0).astype(bf16), ones_bd)
            sc_cross = jnp.concatenate([sc[half:ln], sc[ln + half:2 * ln]], axis=0)
            vs = stack_heads(vv)
            upd_v = _dot(vs.T.astype(bf16), by_head(kk * to_end).astype(bf16))
            pre.append((lhs_nr, sc[:2 * ln], sc[2 * ln:], vs.astype(bf16), by_head(bb * to_end).astype(bf16),
                        jnp.exp(cum[ln - 1:ln]), sc_cross, upd_v))

        pad = jnp.zeros((2 * ln, n), bf16)
        gs = [lax.dot_general(pre[c][0], st_scr[c].astype(bf16), nt_dims, preferred_element_type=f32)
              for c in range(len(combos))]
        for c in range(len(combos)):
            x_scr[c] = pack_heads(gs[c][:2 * ln] + _dot(pre[c][1], jnp.concatenate([pad, pre[c][3]], axis=0)))

        def solve_half(lo, first_block):
            xh = [x_scr[c, lo:lo + half, :] for c in range(len(combos))]
            for si in range(half - 1):
                for c in range(len(combos)):
                    coef = coef_scr[c, (first_block + si) * half:(first_block + si + 1) * half, :]
                    xh[c] = xh[c] + coef * jnp.broadcast_to(xh[c][si:si + 1], (half, lanes))
            for c in range(len(combos)):
                x_scr[c, lo:lo + half, :] = xh[c]

        solve_half(0, 0)
        for c in range(len(combos)):
            x_top = jnp.concatenate([x_scr[c, :half, :], jnp.zeros((half, lanes), f32)], axis=0)
            cross = _dot(pre[c][6], jnp.concatenate([stack_heads(x_top).astype(bf16), pad], axis=0))
            x_scr[c, half:, :] = x_scr[c, half:, :] + jnp.concatenate([cross[:half], cross[half:]], axis=-1)
        solve_half(half, half - 1)
        xss = [stack_heads(x_scr[c]) for c in range(len(combos))]
        xs_ts = [xs.T.astype(bf16) for xs in xss]
        for c, (b, p) in enumerate(combos):
            _, _, sc_out, vs_bf, rhs_b_end, d_last, _, upd_v = pre[c]
            o = gs[c][2 * ln:] + _dot(sc_out, jnp.concatenate([xss[c].astype(bf16), vs_bf], axis=0))
            o_ref[b, pl.ds(base, ln), pl.ds(p * lanes, lanes)] = pack_heads(o)
            st_scr[c] = st_scr[c] * d_last + _dot(xs_ts[c], rhs_b_end) + upd_v
        return carry

    lax.fori_loop(0, tb // ln, chunk, 0)
    finish()


def _rw_lanes_kernel(*refs, tlen, has_prev):
    r_ref, w_ref, k_ref, v_ref, kn_ref, b_ref, s0_ref = refs[:7]
    o_ref, s_ref, vec_scr, o_scr = refs[7 + has_prev:]
    n = RW_N
    bsz = V7X_LANES
    sub = V7X_SUBLANES
    s_ref[...] = s0_ref[...]
    for t in range(tlen):
        for slot, ref in enumerate((kn_ref, w_ref, b_ref, k_ref, r_ref, v_ref)):
            x = ref[pl.ds(t, bsz, stride=tlen), :].T
            vec_scr[slot] = jnp.exp(x) if slot == 1 else x

        def value_rows(ig, carry):
            i0 = pl.multiple_of(ig * sub, sub)
            for h in range(RW_PAIR):
                kn_h, w_h, b_h, k_h, r_h = (vec_scr[slot, h * n:(h + 1) * n, :] for slot in range(5))
                v_tile = vec_scr[5, pl.ds(h * n + i0, sub), :]
                o_rows = []
                for ii in range(sub):
                    s_old = s_ref[h, i0 + ii]
                    sa = jnp.sum(s_old * kn_h, axis=0, keepdims=True)
                    s_new = s_old * w_h + sa * b_h + v_tile[ii:ii + 1] * k_h
                    s_ref[h, i0 + ii] = s_new
                    o_rows.append(jnp.sum(s_new * r_h, axis=0, keepdims=True))
                o_scr[pl.ds(h * n + i0, sub), :] = jnp.concatenate(o_rows, axis=0)
            return carry

        lax.fori_loop(0, n // sub, value_rows, 0)
        o_ref[pl.ds(t, bsz, stride=tlen), :] = o_scr[...].T


def _rw_lanes_scan(seqs, s_in, s_prev, layer, depth, bsz, tlen):
    r, w, k, v, kn, bb = seqs
    width = r.shape[-1]
    heads = width // RW_N
    lanes = RW_PAIR * RW_N
    rows = bsz * tlen
    blk = pl.BlockSpec((rows, lanes), lambda p: (0, p))
    st_spec = pl.BlockSpec((None, RW_PAIR, RW_N, RW_N, bsz), lambda p: (layer, p, 0, 0, 0))
    in_specs = [blk] * 6 + [st_spec]
    args = [a.reshape(rows, width) for a in (r, w, k, v, kn, bb)] + [s_in]
    aliases = _threaded_state(args, in_specs, s_prev, 1)
    o, s_new = pl.pallas_call(
        functools.partial(_rw_lanes_kernel, tlen=tlen, has_prev=s_prev is not None),
        grid=(heads // RW_PAIR,),
        in_specs=in_specs,
        out_specs=[blk, st_spec],
        out_shape=[jax.ShapeDtypeStruct((rows, width), f32),
                   jax.ShapeDtypeStruct((depth, heads, RW_N, RW_N, bsz), f32)],
        scratch_shapes=[pltpu.VMEM((6, lanes, bsz), f32), pltpu.VMEM((lanes, bsz), f32)],
        input_output_aliases=aliases,
        compiler_params=_params("parallel"),
        name="rwkv7_lanes_scan",
    )(*args)
    return o.reshape(bsz, tlen, width), s_new


def _rw_mode(bsz, tlen):
    if _tile(tlen, 64) % RW_CHUNK == 0:
        return "chunk"
    return "lanes" if bsz == V7X_LANES else "token"


def _rw_scan(seqs, s_in, s_prev, layer, depth, bsz, tlen):
    if _rw_mode(bsz, tlen) == "lanes" and s_in is not None:
        return _rw_lanes_scan(seqs, s_in, s_prev, layer, depth, bsz, tlen)
    r, w, k, v, kn, bb = seqs
    width = r.shape[-1]
    heads = width // RW_N
    pairs = heads // RW_PAIR
    nb = _tile(bsz, 4, align=1)
    tb = _tile(tlen, 64)
    chunked = tb % RW_CHUNK == 0
    has_state = s_in is not None
    blk = pl.BlockSpec((nb, tb, width), lambda bi, ti: (bi, ti, 0))
    st_spec = pl.BlockSpec((None, nb, heads, RW_N, RW_N), lambda bi, ti: (layer, bi, 0, 0, 0))
    m = nb * pairs * RW_N
    lanes = RW_PAIR * RW_N
    in_specs = [blk] * 6 + ([st_spec] if has_state else [])
    args = [r, w, k, v, kn, bb] + ([s_in] if has_state else [])
    aliases = _threaded_state(args, in_specs, s_prev, 1)
    scratch = [pltpu.VMEM((nb * pairs, RW_N, lanes), f32)]
    if chunked:
        scratch += [pltpu.VMEM((nb * pairs, (RW_CHUNK - 2) * (RW_CHUNK // 2), lanes), f32),
                    pltpu.VMEM((nb * pairs, RW_CHUNK, lanes), f32)]
    else:
        scratch += [pltpu.VMEM((m, lanes), bf16)] * 3
    o, s_new = pl.pallas_call(
        functools.partial(_rw_chunk_kernel if chunked else _rw_scan_kernel, nb=nb, tb=tb, pairs=pairs,
                          has_state=has_state, has_prev=s_prev is not None),
        grid=(bsz // nb, tlen // tb),
        in_specs=in_specs,
        out_specs=[blk, st_spec],
        out_shape=[jax.ShapeDtypeStruct((bsz, tlen, width), f32),
                   jax.ShapeDtypeStruct((depth, bsz, heads, RW_N, RW_N), f32)],
        scratch_shapes=scratch,
        input_output_aliases=aliases,
        compiler_params=_params("parallel", "arbitrary"),
        name="rwkv7_chunk_scan" if chunked else "rwkv7_scan",
    )(*args)
    return o, s_new


def _rw_post_kernel(o_ref, r_ref, k_ref, v_ref, g_ref, lw_ref, lbias_ref, rk_ref, y_ref):
    ones_bd = _group_ones(V7X_LANES, RW_N)
    o = o_ref[...]
    mu = _group_sum(o, ones_bd) * (1.0 / RW_N)
    d = o - mu
    var = _group_sum(d * d, ones_bd) * (1.0 / RW_N)
    on = d * lax.rsqrt(var + RW_GN_EPS) * lw_ref[...] + lbias_ref[...]
    rk = r_ref[...].astype(f32) * k_ref[...].astype(f32) * rk_ref[...]
    bonus = _group_sum(rk, ones_bd) * v_ref[...].astype(f32)
    y_ref[...] = ((on + bonus) * g_ref[...].astype(f32)).astype(y_ref.dtype)


def _rw_post(o, r, k, v, g, ln_w, ln_b, r_k):
    n, width = o.shape
    tm = _tile(n, 512)
    blk = pl.BlockSpec((tm, width), lambda i: (i, 0))
    vec = pl.BlockSpec((1, width), lambda i: (0, 0))
    return pl.pallas_call(
        _rw_post_kernel,
        grid=(n // tm,),
        in_specs=[blk] * 5 + [vec] * 3,
        out_specs=blk,
        out_shape=jax.ShapeDtypeStruct((n, width), bf16),
        compiler_params=_params("parallel"),
        name="rwkv7_post",
    )(o, r, k, v, g, ln_w, ln_b, r_k)


def _conv_kernel(*refs, nb, tb, has_state, has_prev):
    sb_ref, sc_ref, sh_ref = refs[:3]
    st_ref = refs[3] if has_state else None
    cw_ref = refs[3 + has_state]
    y_ref, new_ref = refs[4 + has_state + has_prev:]
    ti = pl.program_id(2)
    rows = nb * tb
    w = sb_ref.shape[-1]
    hist = SC_KSIZE - 1

    @pl.when(ti == 0)
    def _():
        new_ref[...] = st_ref[...] if has_state else jnp.zeros((nb, hist, w), f32)

    u = (sc_ref[...].astype(f32) * sh_ref[...].astype(f32)).reshape(rows, w)
    carry = new_ref[...]
    conv = cw_ref[hist:hist + 1, :] * u
    for back in range(1, hist + 1):
        first = [_bcast_rows(carry[:, hist - back + j:hist - back + j + 1, :], nb, tb) for j in range(back)]
        conv = conv + cw_ref[hist - back:hist - back + 1, :] * _shift_rows(u, first, tb, back)
    y_ref[...] = (sb_ref[...].astype(f32).reshape(rows, w) * conv).reshape(nb, tb, w).astype(y_ref.dtype)
    new_ref[...] = u.reshape(nb, tb, w)[:, tb - hist:tb, :]


def _conv(p_sc, st_in, st_prev, layer, depth, conv_w, bsz, tlen):
    width = conv_w.shape[1]
    cw = _tile(width, 1024, align=V7X_LANES)
    ncol = width // cw
    nb, tb = _seq_block(bsz, tlen)
    assert tb >= SC_KSIZE - 1
    has_state = st_in is not None
    p3 = p_sc.reshape(bsz, tlen, 3 * width)
    seg = lambda s: pl.BlockSpec((nb, tb, cw), lambda bi, cj, ti, s=s: (bi, ti, s * ncol + cj))
    st_spec = pl.BlockSpec((None, nb, SC_KSIZE - 1, cw), lambda bi, cj, ti: (layer, bi, 0, cj))
    in_specs = ([seg(0), seg(1), seg(2)] + ([st_spec] if has_state else [])
                + [pl.BlockSpec((SC_KSIZE, cw), lambda bi, cj, ti: (0, cj))])
    args = [p3, p3, p3] + ([st_in] if has_state else []) + [conv_w]
    aliases = _threaded_state(args, in_specs, st_prev, 1)
    y, new = pl.pallas_call(
        functools.partial(_conv_kernel, nb=nb, tb=tb, has_state=has_state, has_prev=st_prev is not None),
        grid=(bsz // nb, ncol, tlen // tb),
        in_specs=in_specs,
        out_specs=[pl.BlockSpec((nb, tb, cw), lambda bi, cj, ti: (bi, ti, cj)), st_spec],
        out_shape=[jax.ShapeDtypeStruct((bsz, tlen, width), bf16),
                   jax.ShapeDtypeStruct((depth, bsz, SC_KSIZE - 1, width), f32)],
        input_output_aliases=aliases,
        compiler_params=_params("parallel", "parallel", "arbitrary"),
        name="short_conv",
    )(*args)
    return y.reshape(bsz * tlen, width), new


def _merge_kernel(a_ref, b_ref, c_ref, ga_ref, gb_ref, gc_ref, pa_ref, pb_ref, pc_ref, m_ref, w_scr):
    @pl.when(pl.program_id(1) == 0)
    def _():
        for s, w_ref in enumerate((pa_ref, pb_ref, pc_ref)):
            w_scr[s] = w_ref[...].astype(bf16)

    m = (_sigmoid(ga_ref[...].astype(f32)) * _dot(a_ref[...], w_scr[0])
         + _sigmoid(gb_ref[...].astype(f32)) * _dot(b_ref[...], w_scr[1])
         + _sigmoid(gc_ref[...].astype(f32)) * _dot(c_ref[...], w_scr[2]))
    m_ref[...] = m.astype(m_ref.dtype)


def _merge(ya, yb, yc, p_gate, w_pa, w_pb, w_pc, layer):
    n = ya.shape[0]
    width, d = w_pa.shape[1:]
    assert w_pb.shape[1] == width and w_pc.shape[1] == width
    tm = _tile(n, 1024)
    tn = _tile(d, 512, align=V7X_LANES)
    nj = d // tn
    act = pl.BlockSpec((tm, width), lambda j, i: (i, 0))
    gate = lambda s: pl.BlockSpec((tm, tn), lambda j, i, s=s: (i, s * nj + j))
    wt = pl.BlockSpec((None, width, tn), lambda j, i: (layer, 0, j))
    return pl.pallas_call(
        _merge_kernel,
        grid=(nj, n // tm),
        in_specs=[act, act, act, gate(0), gate(1), gate(2), wt, wt, wt],
        out_specs=pl.BlockSpec((tm, tn), lambda j, i: (i, j)),
        out_shape=jax.ShapeDtypeStruct((n, d), bf16),
        scratch_shapes=[pltpu.VMEM((3, width, tn), bf16)],
        compiler_params=_params("arbitrary", "arbitrary"),
        name="gated_merge",
    )(ya, yb, yc, p_gate, p_gate, p_gate, w_pa, w_pb, w_pc)


def _oproj_kernel(m_ref, wo_ref, x_ref, g1_ref, g2_ref, x1_ref, h2_ref):
    mix = _dot(m_ref[...], wo_ref[...])
    x1 = x_ref[...] + _rms(mix, g1_ref[...])
    x1_ref[...] = x1
    h2_ref[...] = _rms(x1, g2_ref[...]).astype(h2_ref.dtype)


def _oproj(m, w_o, layer, x, g_post_mix, g_pre_mlp):
    n, d = x.shape
    tm = _tile(n, 512)
    blk = pl.BlockSpec((tm, d), lambda i: (i, 0))
    vec = pl.BlockSpec((1, d), lambda i: (0, 0))
    return pl.pallas_call(
        _oproj_kernel,
        grid=(n // tm,),
        in_specs=[blk, pl.BlockSpec((None, d, d), lambda i: (layer, 0, 0)), blk, vec, vec],
        out_specs=[blk, blk],
        out_shape=[jax.ShapeDtypeStruct((n, d), f32), jax.ShapeDtypeStruct((n, d), bf16)],
        compiler_params=_params("parallel"),
        name="out_proj_norms",
    )(m, w_o, x, g_post_mix, g_pre_mlp)


def _mlp_kernel(h_ref, w1_ref, w2_ref, ff_ref):
    kf = pl.program_id(1)

    @pl.when(kf == 0)
    def _():
        ff_ref[...] = jnp.zeros_like(ff_ref)

    t = jnp.maximum(_dot(h_ref[...], w1_ref[...].astype(bf16)), 0.0)
    ff_ref[...] += _dot((t * t).astype(bf16), w2_ref[...].astype(bf16))


def _mlp(h2, w1, w2, layer):
    n, d = h2.shape
    dff = w1.shape[2]
    tm = _tile(n, 2048)
    fc = _tile(dff, 512, align=V7X_LANES)
    once = pl.Buffered(1)
    return pl.pallas_call(
        _mlp_kernel,
        grid=(n // tm, dff // fc),
        in_specs=[pl.BlockSpec((tm, d), lambda i, kf: (i, 0), pipeline_mode=once),
                  pl.BlockSpec((None, d, fc), lambda i, kf: (layer, 0, kf)),
                  pl.BlockSpec((None, fc, d), lambda i, kf: (layer, kf, 0))],
        out_specs=pl.BlockSpec((tm, d), lambda i, kf: (i, 0), pipeline_mode=once),
        out_shape=jax.ShapeDtypeStruct((n, d), f32),
        compiler_params=_params("parallel", "arbitrary", vmem=V7X_VMEM_BIG_TILE_BYTES),
        name="relu2_mlp",
    )(h2, w1, w2)


def _residual_kernel(*refs, has_next):
    if has_next:
        x1_ref, ff_ref, g_ref, gn_ref, y_ref, h_ref = refs
    else:
        x1_ref, ff_ref, g_ref, y_ref = refs
    y = x1_ref[...] + _rms(ff_ref[...], g_ref[...])
    y_ref[...] = y
    if has_next:
        h_ref[...] = _rms(y, gn_ref[...]).astype(h_ref.dtype)


def _residual(x1, ff, g_post_mlp, g_next):
    n, d = x1.shape
    tm = _tile(n, 512)
    blk = pl.BlockSpec((tm, d), lambda i: (i, 0))
    vec = pl.BlockSpec((1, d), lambda i: (0, 0))
    has_next = g_next is not None
    outs = pl.pallas_call(
        functools.partial(_residual_kernel, has_next=has_next),
        grid=(n // tm,),
        in_specs=[blk, blk, vec] + ([vec] if has_next else []),
        out_specs=[blk] + ([blk] if has_next else []),
        out_shape=[jax.ShapeDtypeStruct((n, d), f32)] + ([jax.ShapeDtypeStruct((n, d), bf16)] if has_next else []),
        compiler_params=_params("parallel"),
        name="mlp_residual_norm",
    )(*([x1, ff, g_post_mlp] + ([g_next] if has_next else [])))
    return (outs[0], outs[1]) if has_next else (outs[0], None)


def _layer(x, h, bsz, tlen, states_in, states_prev, layer, depth, wts, lw, g_next):
    hg_in, rw_in, shift_in, sc_in = states_in
    hg_prev, rw_prev, shift_prev, sc_prev = states_prev
    w_in, offs = wts["w_in"], wts["offs"]
    p_hg = _proj(h, w_in, layer, offs[0], offs[1] - offs[0], "in_proj_hgrn")
    p_rkv = _proj(h, w_in, layer, offs[1], offs[2] - offs[1], "in_proj_rwkv")
    p_lora = _proj(h, w_in, layer, offs[2], offs[3] - offs[2], "in_proj_rwkv_lora")
    p_sc = _proj(h, w_in, layer, offs[3], offs[4] - offs[3], "in_proj_conv", bf16)
    p_gate = _proj(h, w_in, layer, offs[4], offs[5] - offs[4], "in_proj_gate", bf16)

    ya, hg_new = _hgrn(p_hg, lw["lb"], lw["hg_norm"], hg_in, hg_prev, layer, depth, bsz, tlen)

    seq_dtype = bf16 if _rw_mode(bsz, tlen) == "chunk" else f32
    seqs, shift_new = _rw_prep(p_rkv, p_lora, shift_in, shift_prev, layer, depth, lw["rw_prep"], bsz, tlen,
                               seq_dtype)
    r, w, k, v, kn, bb, g = seqs
    o_rw, rw_new = _rw_scan((r, w, k, v, kn, bb), rw_in, rw_prev, layer, depth, bsz, tlen)
    flat = lambda t: t.reshape(bsz * tlen, t.shape[-1])
    yb = _rw_post(flat(o_rw), flat(r), flat(k), flat(v), flat(g), lw["rw_ln_w"], lw["rw_ln_b"], lw["rw_r_k"])

    yc, sc_new = _conv(p_sc, sc_in, sc_prev, layer, depth, lw["sc_conv_w"], bsz, tlen)

    m = _merge(ya, yb, yc, p_gate, wts["w_pa"], wts["w_pb"], wts["w_pc"], layer)
    x1, h2 = _oproj(m, wts["w_o"], layer, x, lw["g_post_mix"], lw["g_pre_mlp"])
    ff = _mlp(h2, wts["w_ff1"], wts["w_ff2"], layer)
    y, h_next = _residual(x1, ff, lw["g_post_mlp"], g_next)
    return y, h_next, (hg_new, rw_new, shift_new, sc_new)


def kernel(x_prompt, x_sample, state_hgrn, state_rwkv, state_rwkv_shift, state_conv, norm_pre_mix, norm_post_mix, norm_pre_mlp, norm_post_mlp, w_in, hg_lb_logits, hg_norm, w_pa, rw_mu, rw_w0, rw_w2, rw_a0, rw_a2, rw_g2, rw_k_k, rw_k_a, rw_r_k, rw_ln_w, rw_ln_b, w_pb, sc_conv_w, w_pc, w_o, w_ff1, w_ff2):
    depth = w_in.shape[0]
    d_model = x_prompt.shape[-1]
    hg_width = hg_norm.shape[1]
    rw_width = rw_w0.shape[1]
    rw_shift_width = rw_mu.shape[1]
    sc_width = sc_conv_w.shape[2]
    off_rw = 4 * hg_width
    off_lora = off_rw + 3 * rw_width
    off_sc = off_rw + rw_shift_width
    off_gate = off_sc + 3 * sc_width
    assert w_in.shape[2] == off_gate + 3 * d_model

    lb_all = _lower_bounds(hg_lb_logits.astype(f32))
    row = lambda a, l: a[l].reshape(1, -1).astype(f32)
    wts = {"w_in": w_in.astype(f32), "offs": (0, off_rw, off_lora, off_sc, off_gate, w_in.shape[2]),
           "w_pa": w_pa.astype(f32), "w_pb": w_pb.astype(f32), "w_pc": w_pc.astype(f32),
           "w_o": w_o.astype(bf16), "w_ff1": w_ff1.astype(f32), "w_ff2": w_ff2.astype(f32)}

    bp, tp, _ = x_prompt.shape
    bs, ts, _ = x_sample.shape
    yp = x_prompt.reshape(bp * tp, d_model).astype(f32)
    ys = x_sample.reshape(bs * ts, d_model).astype(f32)
    hp = _rmsnorm_cast(yp, row(norm_pre_mix, 0))
    hs = _rmsnorm_cast(ys, row(norm_pre_mix, 0))
    rw_lanes = _rw_mode(bs, ts) == "lanes"
    rw_state = state_rwkv.astype(f32)
    sample_in = (state_hgrn.astype(f32), jnp.transpose(rw_state, (0, 2, 3, 4, 1)) if rw_lanes else rw_state,
                 state_rwkv_shift.astype(f32).reshape(depth, bs, 1, rw_shift_width), state_conv.astype(f32))
    new_p = new_s = (None, None, None, None)
    for l in range(depth):
        lw = {
            "g_post_mix": row(norm_post_mix, l), "g_pre_mlp": row(norm_pre_mlp, l),
            "g_post_mlp": row(norm_post_mlp, l),
            "lb": lb_all[l:l + 1], "hg_norm": row(hg_norm, l),
            "rw_prep": (row(rw_mu, l), row(rw_w0, l), rw_w2[l].astype(bf16), row(rw_a0, l), rw_a2[l].astype(bf16),
                        rw_g2[l].astype(bf16), row(rw_k_k, l), row(rw_k_a, l)),
            "rw_ln_w": row(rw_ln_w, l), "rw_ln_b": row(rw_ln_b, l), "rw_r_k": row(rw_r_k, l),
            "sc_conv_w": sc_conv_w[l].astype(f32),
        }
        g_next = row(norm_pre_mix, l + 1) if l + 1 < depth else None
        yp, hp, new_p = _layer(yp, hp, bp, tp, (None, None, None, None), new_p, l, depth, wts, lw, g_next)
        ys, hs, new_s = _layer(ys, hs, bs, ts, sample_in, new_s, l, depth, wts, lw, g_next)
    shift = lambda s, b: s.reshape(depth, b, rw_shift_width)
    return (yp.reshape(bp, tp, d_model), ys.reshape(bs, ts, d_model),
            new_p[0], new_p[1], shift(new_p[2], bp), new_p[3],
            new_s[0], jnp.transpose(new_s[1], (0, 4, 1, 2, 3)) if rw_lanes else new_s[1],
            shift(new_s[2], bs), new_s[3])
```

```python
import functools

import jax
import jax.numpy as jnp
from jax import lax
from jax.experimental import pallas as pl
from jax.experimental.pallas import tpu as pltpu

HG_DK = 128
RW_N = 64
HG_F_MIN = 1e-30
LOG2_E = 1.4426950408889634
RW_GN_EPS = 64e-5
NORM_EPS = 1e-6
SC_KSIZE = 3

V7X_LANES = 128
V7X_SUBLANES = 8
V7X_BF16_ROWS = 16
V7X_VMEM_LIMIT_BYTES = 48 * 1024 * 1024
V7X_VMEM_BIG_TILE_BYTES = 56 * 1024 * 1024
HG_CHUNK = 16
HG_HEADS_PER_STEP = 4
HG_GROUP = 4
RW_PAIR = 2
RW_CHUNK = 16

f32 = jnp.float32
bf16 = jnp.bfloat16


def _tile(n, pref, align=V7X_SUBLANES):
    if n <= pref:
        return n
    for d in range(pref, 0, -1):
        if n % d == 0 and d % align == 0:
            return d
    return n


def _seq_block(bsz, tlen):
    if tlen >= 256:
        return 1, _tile(tlen, 256)
    return _tile(bsz, max(1, 128 // tlen), align=1), tlen


def _params(*sem, vmem=V7X_VMEM_LIMIT_BYTES):
    return pltpu.CompilerParams(dimension_semantics=sem, vmem_limit_bytes=vmem)


def _sigmoid(x):
    return 1.0 / (1.0 + jnp.exp(-x))


def _gate_sigmoid(x):
    return 0.5 * jnp.tanh(0.5 * x) + 0.5


def _dot(a, b):
    return jnp.dot(a, b, preferred_element_type=f32)


def _rms(x, g):
    return x * lax.rsqrt(jnp.mean(x * x, axis=-1, keepdims=True) + NORM_EPS) * g


def _group_ones(width, group):
    r = lax.broadcasted_iota(jnp.int32, (width, width), 0) // group
    c = lax.broadcasted_iota(jnp.int32, (width, width), 1) // group
    return jnp.where(r == c, 1.0, 0.0).astype(bf16)


def _group_sum(z, ones_bd):
    outs = []
    for c in range(z.shape[-1] // V7X_LANES):
        zc = z[:, c * V7X_LANES:(c + 1) * V7X_LANES]
        hi = zc.astype(bf16)
        lo = (zc - hi.astype(f32)).astype(bf16)
        outs.append(_dot(hi, ones_bd) + _dot(lo, ones_bd))
    return outs[0] if len(outs) == 1 else jnp.concatenate(outs, axis=-1)


def _threaded_state(args, in_specs, prev_out, out_index):
    if prev_out is None:
        return {}
    args.append(prev_out)
    in_specs.append(pl.BlockSpec(memory_space=pl.ANY))
    return {len(args) - 1: out_index}


def _lb_kernel(logit_ref, lb_ref):
    z = logit_ref[...]
    depth = z.shape[0]
    m = jnp.max(z, axis=0, keepdims=True)
    e = jnp.exp(z - m)
    p = e / jnp.sum(e, axis=0, keepdims=True)
    acc = jnp.zeros_like(p[0:1])
    for l in range(depth):
        acc = acc + p[l:l + 1]
        lb_ref[l:l + 1, :] = jnp.clip(acc - p[0:1], 0.0, 1.0)


def _lower_bounds(logits):
    return pl.pallas_call(
        _lb_kernel, out_shape=jax.ShapeDtypeStruct(logits.shape, f32), name="hg_lower_bounds",
    )(logits)


def _rmsnorm_kernel(x_ref, g_ref, o_ref):
    o_ref[...] = _rms(x_ref[...], g_ref[...]).astype(o_ref.dtype)


def _rmsnorm_cast(x, g):
    n, d = x.shape
    tm = _tile(n, 512)
    return pl.pallas_call(
        _rmsnorm_kernel,
        grid=(n // tm,),
        in_specs=[pl.BlockSpec((tm, d), lambda i: (i, 0)), pl.BlockSpec((1, d), lambda i: (0, 0))],
        out_specs=pl.BlockSpec((tm, d), lambda i: (i, 0)),
        out_shape=jax.ShapeDtypeStruct((n, d), bf16),
        compiler_params=_params("parallel"),
        name="rmsnorm_cast",
    )(x, g)


def _proj_kernel(a_ref, w_ref, o_ref, wbf_scr):
    @pl.when(pl.program_id(1) == 0)
    def _():
        wbf_scr[...] = w_ref[0].astype(bf16)

    o_ref[...] = _dot(a_ref[...], wbf_scr[...]).astype(o_ref.dtype)


def _proj(h, w_stack, layer, off, width, name, out_dtype=f32):
    n, k = h.shape
    big = out_dtype == bf16
    tm = _tile(n, 2048 if big else 1024)
    tn = _tile(width, 1024, align=V7X_LANES)
    w_spec = pl.BlockSpec((pl.Element(1), pl.Element(k), pl.Element(tn)),
                          lambda j, i: (layer, 0, pl.multiple_of(off + j * tn, V7X_LANES)))
    return pl.pallas_call(
        _proj_kernel,
        grid=(width // tn, n // tm),
        in_specs=[pl.BlockSpec((tm, k), lambda j, i: (i, 0)), w_spec],
        out_specs=pl.BlockSpec((tm, tn), lambda j, i: (i, j)),
        out_shape=jax.ShapeDtypeStruct((n, width), out_dtype),
        scratch_shapes=[pltpu.VMEM((k, tn), bf16)],
        compiler_params=_params("arbitrary", "arbitrary",
                                vmem=V7X_VMEM_BIG_TILE_BYTES if big else V7X_VMEM_LIMIT_BYTES),
        name=name,
    )(h, w_stack)


def _hgrn_kernel(*refs, nb, tb, hb, chunk, has_state, has_prev):
    q_ref, f_ref, i_ref, og_ref, lb_ref, gn_ref, rsel_ref = refs[:7]
    s0_ref = refs[7] if has_state else None
    o_ref, s_ref, st_scr = refs[7 + has_state + has_prev:]
    ti = pl.program_id(2)
    rows = nb * tb
    dk = HG_DK
    nchunk = rows // chunk
    heads = range(hb)

    @pl.when(ti == 0)
    def _():
        for b in range(nb):
            for h in heads:
                st_scr[b, h] = s0_ref[b, h].T if has_state else jnp.zeros((dk, dk), f32)

    grp = HG_GROUP if (tb // chunk) % HG_GROUP == 0 else 1
    row_idx = lax.broadcasted_iota(jnp.int32, (rows, dk), 0)
    t_idx = row_idx % chunk
    pos_in_grp = (row_idx // chunk) % grp
    chunk_r = lax.broadcasted_iota(jnp.int32, (rows, rows), 0) // chunk
    chunk_c = lax.broadcasted_iota(jnp.int32, (rows, rows), 1) // chunk
    chunk_masks = [jnp.where((chunk_r - chunk_c == dist) & (chunk_r // grp == chunk_c // grp), 1.0, 0.0)
                   for dist in range(grp)]
    ngroup = nchunk // grp
    col_group = lax.broadcasted_iota(jnp.int32, (dk, rows), 1) // (grp * chunk)

    def chunk_row(a, s):
        a3 = a.reshape(nchunk, chunk, dk)
        return jnp.broadcast_to(a3[:, s:s + 1, :], (nchunk, chunk, dk)).reshape(rows, dk)

    def head_cols(ref, h):
        return ref[:, :, h * dk:(h + 1) * dk].reshape(rows, dk)

    sub = V7X_SUBLANES
    tpc = chunk // sub

    def tiles(a):
        return a.reshape(nchunk, tpc, sub, dk)

    glogs, q_decs, o_intras, u_alls = [], [], [], []
    for h in heads:
        lb = lb_ref[:, h * dk:(h + 1) * dk]
        hq = head_cols(q_ref, h)
        qq = hq * _gate_sigmoid(hq)
        f = lb + (1.0 - lb) * _sigmoid(head_cols(f_ref, h))
        kk = 1.0 - f
        bcum = jnp.log(jnp.maximum(f, HG_F_MIN))
        step = 1
        while step < chunk:
            bcum = bcum + jnp.where(t_idx >= step, pltpu.roll(bcum, step, 0), 0.0)
            step *= 2
        b_last = chunk_row(bcum, chunk - 1)
        zs = []
        b2 = bcum * LOG2_E
        for s in range(chunk):
            t0 = s // sub

            def col_row(a):
                return jnp.broadcast_to(tiles(a)[:, t0:t0 + 1, s % sub:s % sub + 1, :], (nchunk, 1, sub, dk))

            b2_s, k_s = col_row(b2), col_row(kk)
            parts = [jnp.zeros((nchunk, t0, sub, dk), f32)] if t0 else []
            for tt in range(t0, tpc):
                diff = tiles(b2)[:, tt:tt + 1] - b2_s
                qk = tiles(qq)[:, tt:tt + 1] * k_s
                if tt == t0:
                    z = jnp.where(tiles(t_idx)[:, tt:tt + 1] >= s, qk * jnp.exp2(jnp.minimum(diff, 0.0)), 0.0)
                else:
                    z = qk * jnp.exp2(diff)
                parts.append(z)
            z = parts[0] if len(parts) == 1 else jnp.concatenate(parts, axis=1)
            zs.append(z.reshape(rows, dk).astype(bf16))
        k_dec =(kk * jnp.exp(b_last - bcum)).astype(bf16)
        vv = head_cols(i_ref, h)
        a_rep = _dot(jnp.concatenate(zs, axis=-1), rsel_ref[...])
        scores = a_rep * chunk_masks[0]
        between = jnp.zeros_like(bcum)
        for dist in range(1, grp):
            q_far = (qq * jnp.exp(bcum + between)).astype(bf16)
            far = lax.dot_general(q_far, k_dec, (((1,), (1,)), ((), ())), preferred_element_type=f32)
            scores = scores + far * chunk_masks[dist]
            between = between + pltpu.roll(b_last, dist * chunk, 0)
        reach = jnp.zeros_like(bcum)
        tail = jnp.zeros_like(bcum)
        for dist in range(1, grp):
            reach = reach + jnp.where(pos_in_grp >= dist, pltpu.roll(b_last, dist * chunk, 0), 0.0)
            tail = tail + jnp.where(pos_in_grp + dist < grp, pltpu.roll(b_last, rows - dist * chunk, 0), 0.0)
        glogs.append(bcum + reach)
        q_decs.append((qq * jnp.exp(bcum + reach)).astype(bf16))
        o_intras.append(_dot(scores.astype(bf16), vv.astype(bf16)))
        k_grp = k_dec if grp == 1 else (kk * jnp.exp(b_last - bcum + tail)).astype(bf16)
        v_t = vv.T
        lhs = jnp.concatenate([jnp.where(col_group == g, v_t, 0.0).astype(bf16) for g in range(ngroup)], axis=0)
        u_alls.append(_dot(lhs, k_grp))

    outs = [[] for _ in heads]
    span = grp * chunk
    for b in range(nb):
        sts = [st_scr[b, h] for h in heads]
        for g in range(tb // span):
            gi = b * (tb // span) + g
            r0 = gi * span
            for h in heads:
                inter = lax.dot_general(q_decs[h][r0:r0 + span], sts[h].astype(bf16),
                                        (((1,), (1,)), ((), ())), preferred_element_type=f32)
                outs[h].append(inter + o_intras[h][r0:r0 + span])
                sts[h] = jnp.exp(glogs[h][r0 + span - 1:r0 + span]) * sts[h] + u_alls[h][gi * dk:(gi + 1) * dk]
        for h in heads:
            st_scr[b, h] = sts[h]
    for h in heads:
        o = jnp.concatenate(outs[h], axis=0)
        o = o * lax.rsqrt(jnp.mean(o * o, axis=-1, keepdims=True) + NORM_EPS)
        hog = head_cols(og_ref, h)
        o = o * gn_ref[:, h * dk:(h + 1) * dk] * (hog * _gate_sigmoid(hog))
        o_ref[:, :, h * dk:(h + 1) * dk] = o.reshape(nb, tb, dk).astype(o_ref.dtype)

    @pl.when(ti == pl.num_programs(2) - 1)
    def _():
        for b in range(nb):
            for h in heads:
                s_ref[b, h] = st_scr[b, h].T


def _hgrn(p_hg, lb, gn, s_in, s_prev, layer, depth, bsz, tlen):
    width = p_hg.shape[1] // 4
    heads = width // HG_DK
    chunk = min(HG_CHUNK, tlen)
    nb, tb = _seq_block(bsz, tlen)
    assert tb % chunk == 0
    p3 = p_hg.reshape(bsz, tlen, 4 * width)
    hb = HG_HEADS_PER_STEP if heads % HG_HEADS_PER_STEP == 0 else 1
    hgroups = heads // hb
    wb = hb * HG_DK
    seg = lambda s: pl.BlockSpec((nb, tb, wb), lambda bi, h, ti, s=s: (bi, ti, s * hgroups + h))
    vec = pl.BlockSpec((1, wb), lambda bi, h, ti: (0, h))
    st_spec = pl.BlockSpec((None, nb, hb, HG_DK, HG_DK), lambda bi, h, ti: (layer, bi, h, 0, 0))
    has_state = s_in is not None
    rows = nb * tb
    rsel = (jnp.arange(chunk * HG_DK)[:, None] // HG_DK == jnp.arange(rows)[None, :] % chunk).astype(bf16)
    rsel_spec = pl.BlockSpec((chunk * HG_DK, rows), lambda bi, h, ti: (0, 0))
    in_specs = [seg(0), seg(1), seg(2), seg(3), vec, vec, rsel_spec] + ([st_spec] if has_state else [])
    args = [p3, p3, p3, p3, lb, gn, rsel] + ([s_in] if has_state else [])
    aliases = _threaded_state(args, in_specs, s_prev, 1)
    o, s_new = pl.pallas_call(
        functools.partial(_hgrn_kernel, nb=nb, tb=tb, hb=hb, chunk=chunk, has_state=has_state,
                          has_prev=s_prev is not None),
        grid=(bsz // nb, hgroups, tlen // tb),
        in_specs=in_specs,
        out_specs=[pl.BlockSpec((nb, tb, wb), lambda bi, h, ti: (bi, ti, h)), st_spec],
        out_shape=[jax.ShapeDtypeStruct((bsz, tlen, width), bf16),
                   jax.ShapeDtypeStruct((depth, bsz, heads, HG_DK, HG_DK), f32)],
        scratch_shapes=[pltpu.VMEM((nb, hb, HG_DK, HG_DK), f32)],
        input_output_aliases=aliases,
        compiler_params=_params("parallel", "parallel", "arbitrary"),
        name="hgrn2_scan",
    )(*args)
    return o.reshape(bsz * tlen, width), s_new


def _shift_rows(x, first, tb, by):
    rows = x.shape[0]
    t_idx = lax.broadcasted_iota(jnp.int32, (rows, 1), 0) % tb
    out = pltpu.roll(x, by, 0) if tb > by else x
    for j in range(by):
        out = jnp.where(t_idx == j, first[j], out)
    return out


def _bcast_rows(v3, nb, tb):
    w = v3.shape[-1]
    return jnp.broadcast_to(v3, (nb, tb, w)).reshape(nb * tb, w)


def _rw_prep_kernel(*refs, nb, tb, width, lora, has_state, has_prev):
    pm_ref, pl_ref = refs[:2]
    sh_ref = refs[2] if has_state else None
    mu_ref, w0_ref, w2_ref, a0_ref, a2_ref, g2_ref, kk_ref, ka_ref = refs[2 + has_state:10 + has_state]
    r_ref, w_ref, k_ref, v_ref, kn_ref, b_ref, g_ref, last_ref = refs[10 + has_state + has_prev:]
    ti = pl.program_id(1)
    rows = nb * tb
    tot = last_ref.shape[-1]
    x = jnp.concatenate([pm_ref[...], pl_ref[...]], axis=-1).reshape(rows, tot)

    @pl.when(ti == 0)
    def _():
        last_ref[...] = sh_ref[...] if has_state else jnp.zeros((nb, 1, tot), f32)

    prev = _shift_rows(x, [_bcast_rows(last_ref[...], nb, tb)], tb, 1)
    last_ref[...] = jnp.concatenate([pm_ref[:, tb - 1:tb, :], pl_ref[:, tb - 1:tb, :]], axis=-1)
    xs = x + (prev - x) * mu_ref[...]
    dl, al, gl = lora
    c = 3 * width
    r, kr, vr = xs[:, :width], xs[:, width:2 * width], xs[:, 2 * width:c]
    wd, ad, gd = xs[:, c:c + dl], xs[:, c + dl:c + dl + al], xs[:, c + dl + al:c + dl + al + gl]
    z = -(w0_ref[...] + _dot(jnp.tanh(wd).astype(bf16), w2_ref[...]))
    softplus = jnp.maximum(z, 0.0) + jnp.log(1.0 + jnp.exp(-jnp.abs(z)))
    w_log = -softplus - 0.5
    log_decay = -jnp.exp(w_log)
    a = _gate_sigmoid(a0_ref[...] + _dot(ad.astype(bf16), a2_ref[...]))
    g = _dot(_gate_sigmoid(gd).astype(bf16), g2_ref[...])
    kk = kr * kk_ref[...]
    ones_bd = _group_ones(V7X_LANES, RW_N)
    nrm = jnp.sqrt(_group_sum(kk * kk, ones_bd))
    kk = kk / jnp.maximum(nrm, 1e-12)
    k = kr * (1.0 + (a - 1.0) * ka_ref[...])
    shp = (nb, tb, width)
    r_ref[...] = r.reshape(shp).astype(r_ref.dtype)
    w_ref[...] = log_decay.reshape(shp)
    k_ref[...] = k.reshape(shp).astype(k_ref.dtype)
    v_ref[...] = vr.reshape(shp).astype(v_ref.dtype)
    kn_ref[...] = (-kk).reshape(shp).astype(kn_ref.dtype)
    b_ref[...] = (kk * a).reshape(shp).astype(b_ref.dtype)
    g_ref[...] = g.reshape(shp).astype(g_ref.dtype)


def _rw_prep(p_main, p_lora, shift_in, shift_prev, layer, depth, wts, bsz, tlen, seq_dtype):
    mu, w0, w2, a0, a2, g2, k_k, k_a = wts
    width = w0.shape[1]
    wl = p_lora.shape[1]
    tot = 3 * width + wl
    lora = (w2.shape[0], a2.shape[0], g2.shape[0])
    nb, tb = _seq_block(bsz, tlen)
    has_state = shift_in is not None
    blk = lambda w: pl.BlockSpec((nb, tb, w), lambda bi, ti: (bi, ti, 0))
    full = lambda arr: pl.BlockSpec(arr.shape, lambda bi, ti: (0,) * arr.ndim)
    last_spec = pl.BlockSpec((None, nb, 1, tot), lambda bi, ti: (layer, bi, 0, 0))
    small = [mu, w0, w2, a0, a2, g2, k_k, k_a]
    in_specs = [blk(3 * width), blk(wl)] + ([last_spec] if has_state else []) + [full(s) for s in small]
    args = ([p_main.reshape(bsz, tlen, 3 * width), p_lora.reshape(bsz, tlen, wl)]
            + ([shift_in] if has_state else []) + small)
    aliases = _threaded_state(args, in_specs, shift_prev, 7)
    seq = lambda i: jax.ShapeDtypeStruct((bsz, tlen, width), f32 if i == 1 else seq_dtype)
    outs = pl.pallas_call(
        functools.partial(_rw_prep_kernel, nb=nb, tb=tb, width=width, lora=lora, has_state=has_state,
                          has_prev=shift_prev is not None),
        grid=(bsz // nb, tlen // tb),
        in_specs=in_specs,
        out_specs=[blk(width)] * 7 + [last_spec],
        out_shape=[seq(i) for i in range(7)] + [jax.ShapeDtypeStruct((depth, bsz, 1, tot), f32)],
        input_output_aliases=aliases,
        compiler_params=_params("parallel", "arbitrary"),
        name="rwkv7_prep",
    )(*args)
    return outs[:7], outs[7]


def _rw_state_io(s0_ref, s_ref, st_scr, combos, has_state, ti, last):
    n = RW_N

    @pl.when(ti == 0)
    def _():
        for c, (b, p) in enumerate(combos):
            if has_state:
                st_scr[c] = jnp.concatenate([s0_ref[b, RW_PAIR * p + h] for h in range(RW_PAIR)], axis=-1)
            else:
                st_scr[c] = jnp.zeros((n, RW_PAIR * n), f32)

    def finish():
        @pl.when(ti == last)
        def _():
            for c, (b, p) in enumerate(combos):
                s = st_scr[c]
                for h in range(RW_PAIR):
                    s_ref[b, RW_PAIR * p + h] = s[:, h * n:(h + 1) * n]

    return finish


def _rw_scan_kernel(*refs, nb, tb, pairs, has_state, has_prev):
    r_ref, w_ref, k_ref, v_ref, kn_ref, b_ref = refs[:6]
    s0_ref = refs[6] if has_state else None
    o_ref, s_ref, st_scr, lhs_a, lhs_v, lhs_o = refs[6 + has_state + has_prev:]
    ti = pl.program_id(1)
    n = RW_N
    lanes = RW_PAIR * n
    sub = V7X_SUBLANES
    combos = [(b, p) for b in range(nb) for p in range(pairs)]
    finish = _rw_state_io(s0_ref, s_ref, st_scr, combos, has_state, ti, pl.num_programs(1) - 1)

    ones_bd = _group_ones(lanes, n)
    diag = (lax.broadcasted_iota(jnp.int32, (n, lanes), 0)
            == lax.broadcasted_iota(jnp.int32, (n, lanes), 1) % n)

    def token_group(grp, carry):
        base = pl.multiple_of(grp * sub, sub)

        def row(ref, b, p, j, decay=False):
            tile = ref[b, pl.ds(base, sub), pl.ds(p * lanes, lanes)]
            tile = jnp.exp(tile) if decay else tile
            return jnp.broadcast_to(tile[j:j + 1], (n, lanes))

        o_rows = [[] for _ in combos]
        for j in range(sub):
            for c, (b, p) in enumerate(combos):
                lhs_a[c * n:(c + 1) * n, :] = (st_scr[c] * row(kn_ref, b, p, j)).astype(bf16)
                lhs_v[c * n:(c + 1) * n, :] = jnp.where(diag, row(v_ref, b, p, j), 0.0).astype(bf16)
            sa_all = _dot(lhs_a[...], ones_bd)
            vb_all = _dot(lhs_v[...], ones_bd)
            for c, (b, p) in enumerate(combos):
                s_new = (st_scr[c] * row(w_ref, b, p, j, decay=True)
                         + sa_all[c * n:(c + 1) * n] * row(b_ref, b, p, j)
                         + vb_all[c * n:(c + 1) * n] * row(k_ref, b, p, j))
                st_scr[c] = s_new
                lhs_o[c * n:(c + 1) * n, :] = (s_new * row(r_ref, b, p, j)).astype(bf16)
            o_all = _dot(lhs_o[...], ones_bd)
            for c, (b, p) in enumerate(combos):
                o_rows[c].append(jnp.sum(jnp.where(diag, o_all[c * n:(c + 1) * n], 0.0),
                                         axis=0, keepdims=True))
        for c, (b, p) in enumerate(combos):
            o_ref[b, pl.ds(base, sub), pl.ds(p * lanes, lanes)] = jnp.concatenate(o_rows[c], axis=0)
        return carry

    lax.fori_loop(0, tb // sub, token_group, 0)
    finish()


def _rw_chunk_kernel(*refs, nb, tb, pairs, has_state, has_prev):
    r_ref, w_ref, k_ref, v_ref, kn_ref, b_ref = refs[:6]
    s0_ref = refs[6] if has_state else None
    o_ref, s_ref, st_scr, coef_scr, x_scr = refs[6 + has_state + has_prev:]
    ti = pl.program_id(1)
    n = RW_N
    lanes = RW_PAIR * n
    ln = RW_CHUNK
    combos = [(b, p) for b in range(nb) for p in range(pairs)]
    finish = _rw_state_io(s0_ref, s_ref, st_scr, combos, has_state, ti, pl.num_programs(1) - 1)

    ones_bd = _group_ones(lanes, n)
    t_idx = lax.broadcasted_iota(jnp.int32, (ln, lanes), 0)
    lane_idx = lax.broadcasted_iota(jnp.int32, (ln, lanes), 1)
    row_i = lax.broadcasted_iota(jnp.int32, (4 * ln, 4 * ln), 0)
    col_i = lax.broadcasted_iota(jnp.int32, (4 * ln, 4 * ln), 1)
    same_head = (row_i % (2 * ln)) // ln == (col_i % (2 * ln)) // ln
    strict = jnp.where(row_i < 2 * ln, 1, 0)
    lower2 = jnp.where(same_head & (col_i % ln + strict <= row_i % ln), 1.0, 0.0)
    nt_dims = (((1,), (1,)), ((), ()))

    def by_head(x):
        head0 = lane_idx < n
        return jnp.concatenate([jnp.where(head0, x, 0.0), jnp.where(head0, 0.0, x)], axis=0)

    def stack_heads(x):
        return jnp.concatenate([x[:, :n], x[:, n:]], axis=0)

    def pack_heads(xs):
        return jnp.concatenate([xs[:ln], xs[ln:]], axis=-1)

    half = ln // 2
    solve_cols = list(range(half - 1)) + list(range(half, ln - 1))

    def chunk(ci, carry):
        base = pl.multiple_of(ci * ln, ln)
        pre = []
        for c, (b, p) in enumerate(combos):
            tile = lambda ref: ref[b, pl.ds(base, ln), pl.ds(p * lanes, lanes)].astype(f32)
            lw = tile(w_ref)
            cum = lw
            step = 1
            while step < ln:
                cum = cum + jnp.where(t_idx >= step, pltpu.roll(cum, step, 0), 0.0)
                step *= 2
            c_last = jnp.broadcast_to(cum[ln - 1:ln], (ln, lanes))
            inv_c = jnp.exp(-cum)
            to_end = jnp.exp(c_last - cum)
            kn, bb, kk, vv = tile(kn_ref), tile(b_ref), tile(k_ref), tile(v_ref)
            n_dec = kn * jnp.exp(cum - lw)
            b_dec = bb * inv_c
            k_dec = kk * inv_c
            r_dec = tile(r_ref) * jnp.exp(cum)
            lhs_nr = jnp.concatenate([by_head(n_dec), by_head(r_dec)], axis=0).astype(bf16)
            rhs_bk = jnp.concatenate([by_head(b_dec), by_head(k_dec)], axis=0).astype(bf16)
            sc = (lower2 * lax.dot_general(lhs_nr, rhs_bk, nt_dims, preferred_element_type=f32)).astype(bf16)
            blocks = []
            for s in solve_cols:
                lo = 0 if s < half else half
                blocks.append(jnp.where(t_idx[lo:lo + half] > s,
                                        n_dec[lo:lo + half] * jnp.broadcast_to(b_dec[s:s + 1], (half, lanes)), 0.0))
            coef_scr[c] = _dot(jnp.concatenate(blocks, axis=0).astype(bf16), ones_bd)
            sc_cross = jnp.concatenate([sc[half:ln], sc[ln + half:2 * ln]], axis=0)
            vs = stack_heads(vv)
            upd_v = _dot(vs.T.astype(bf16), by_head(kk * to_end).astype(bf16))
            pre.append((lhs_nr, sc[:2 * ln], sc[2 * ln:], vs.astype(bf16), by_head(bb * to_end).astype(bf16),
                        jnp.exp(cum[ln - 1:ln]), sc_cross, upd_v))

        pad = jnp.zeros((2 * ln, n), bf16)
        gs = [lax.dot_general(pre[c][0], st_scr[c].astype(bf16), nt_dims, preferred_element_type=f32)
              for c in range(len(combos))]
        for c in range(len(combos)):
            x_scr[c] = pack_heads(gs[c][:2 * ln] + _dot(pre[c][1], jnp.concatenate([pad, pre[c][3]], axis=0)))

        def solve_half(lo, first_block):
            xh = [x_scr[c, lo:lo + half, :] for c in range(len(combos))]
            for si in range(half - 1):
                for c in range(len(combos)):
                    coef = coef_scr[c, (first_block + si) * half:(first_block + si + 1) * half, :]
                    xh[c] = xh[c] + coef * jnp.broadcast_to(xh[c][si:si + 1], (half, lanes))
            for c in range(len(combos)):
                x_scr[c, lo:lo + half, :] = xh[c]

        solve_half(0, 0)
        for c in range(len(combos)):
            x_top = jnp.concatenate([x_scr[c, :half, :], jnp.zeros((half, lanes), f32)], axis=0)
            cross = _dot(pre[c][6], jnp.concatenate([stack_heads(x_top).astype(bf16), pad], axis=0))
            x_scr[c, half:, :] = x_scr[c, half:, :] + jnp.concatenate([cross[:half], cross[half:]], axis=-1)
        solve_half(half, half - 1)
        xss = [stack_heads(x_scr[c]) for c in range(len(combos))]
        xs_ts = [xs.T.astype(bf16) for xs in xss]
        for c, (b, p) in enumerate(combos):
            _, _, sc_out, vs_bf, rhs_b_end, d_last, _, upd_v = pre[c]
            o = gs[c][2 * ln:] + _dot(sc_out, jnp.concatenate([xss[c].astype(bf16), vs_bf], axis=0))
            o_ref[b, pl.ds(base, ln), pl.ds(p * lanes, lanes)] = pack_heads(o)
            st_scr[c] = st_scr[c] * d_last + _dot(xs_ts[c], rhs_b_end) + upd_v
        return carry

    lax.fori_loop(0, tb // ln, chunk, 0)
    finish()


def _rw_lanes_kernel(*refs, tlen, has_prev):
    r_ref, w_ref, k_ref, v_ref, kn_ref, b_ref, s0_ref = refs[:7]
    o_ref, s_ref, vec_scr, o_scr = refs[7 + has_prev:]
    n = RW_N
    bsz = V7X_LANES
    sub = V7X_SUBLANES
    s_ref[...] = s0_ref[...]
    for t in range(tlen):
        for slot, ref in enumerate((kn_ref, w_ref, b_ref, k_ref, r_ref, v_ref)):
            x = ref[pl.ds(t, bsz, stride=tlen), :].T
            vec_scr[slot] = jnp.exp(x) if slot == 1 else x

        def value_rows(ig, carry):
            i0 = pl.multiple_of(ig * sub, sub)
            for h in range(RW_PAIR):
                kn_h, w_h, b_h, k_h, r_h = (vec_scr[slot, h * n:(h + 1) * n, :] for slot in range(5))
                v_tile = vec_scr[5, pl.ds(h * n + i0, sub), :]
                o_rows = []
                for ii in range(sub):
                    s_old = s_ref[h, i0 + ii]
                    sa = jnp.sum(s_old * kn_h, axis=0, keepdims=True)
                    s_new = s_old * w_h + sa * b_h + v_tile[ii:ii + 1] * k_h
                    s_ref[h, i0 + ii] = s_new
                    o_rows.append(jnp.sum(s_new * r_h, axis=0, keepdims=True))
                o_scr[pl.ds(h * n + i0, sub), :] = jnp.concatenate(o_rows, axis=0)
            return carry

        lax.fori_loop(0, n // sub, value_rows, 0)
        o_ref[pl.ds(t, bsz, stride=tlen), :] = o_scr[...].T


def _rw_lanes_scan(seqs, s_in, s_prev, layer, depth, bsz, tlen):
    r, w, k, v, kn, bb = seqs
    width = r.shape[-1]
    heads = width // RW_N
    lanes = RW_PAIR * RW_N
    rows = bsz * tlen
    blk = pl.BlockSpec((rows, lanes), lambda p: (0, p))
    st_spec = pl.BlockSpec((None, RW_PAIR, RW_N, RW_N, bsz), lambda p: (layer, p, 0, 0, 0))
    in_specs = [blk] * 6 + [st_spec]
    args = [a.reshape(rows, width) for a in (r, w, k, v, kn, bb)] + [s_in]
    aliases = _threaded_state(args, in_specs, s_prev, 1)
    o, s_new = pl.pallas_call(
        functools.partial(_rw_lanes_kernel, tlen=tlen, has_prev=s_prev is not None),
        grid=(heads // RW_PAIR,),
        in_specs=in_specs,
        out_specs=[blk, st_spec],
        out_shape=[jax.ShapeDtypeStruct((rows, width), f32),
                   jax.ShapeDtypeStruct((depth, heads, RW_N, RW_N, bsz), f32)],
        scratch_shapes=[pltpu.VMEM((6, lanes, bsz), f32), pltpu.VMEM((lanes, bsz), f32)],
        input_output_aliases=aliases,
        compiler_params=_params("parallel"),
        name="rwkv7_lanes_scan",
    )(*args)
    return o.reshape(bsz, tlen, width), s_new


def _rw_mode(bsz, tlen):
    if _tile(tlen, 64) % RW_CHUNK == 0:
        return "chunk"
    return "lanes" if bsz == V7X_LANES else "token"


def _rw_scan(seqs, s_in, s_prev, layer, depth, bsz, tlen):
    if _rw_mode(bsz, tlen) == "lanes" and s_in is not None:
        return _rw_lanes_scan(seqs, s_in, s_prev, layer, depth, bsz, tlen)
    r, w, k, v, kn, bb = seqs
    width = r.shape[-1]
    heads = width // RW_N
    pairs = heads // RW_PAIR
    nb = _tile(bsz, 4, align=1)
    tb = _tile(tlen, 64)
    chunked = tb % RW_CHUNK == 0
    has_state = s_in is not None
    blk = pl.BlockSpec((nb, tb, width), lambda bi, ti: (bi, ti, 0))
    st_spec = pl.BlockSpec((None, nb, heads, RW_N, RW_N), lambda bi, ti: (layer, bi, 0, 0, 0))
    m = nb * pairs * RW_N
    lanes = RW_PAIR * RW_N
    in_specs = [blk] * 6 + ([st_spec] if has_state else [])
    args = [r, w, k, v, kn, bb] + ([s_in] if has_state else [])
    aliases = _threaded_state(args, in_specs, s_prev, 1)
    scratch = [pltpu.VMEM((nb * pairs, RW_N, lanes), f32)]
    if chunked:
        scratch += [pltpu.VMEM((nb * pairs, (RW_CHUNK - 2) * (RW_CHUNK // 2), lanes), f32),
                    pltpu.VMEM((nb * pairs, RW_CHUNK, lanes), f32)]
    else:
        scratch += [pltpu.VMEM((m, lanes), bf16)] * 3
    o, s_new = pl.pallas_call(
        functools.partial(_rw_chunk_kernel if chunked else _rw_scan_kernel, nb=nb, tb=tb, pairs=pairs,
                          has_state=has_state, has_prev=s_prev is not None),
        grid=(bsz // nb, tlen // tb),
        in_specs=in_specs,
        out_specs=[blk, st_spec],
        out_shape=[jax.ShapeDtypeStruct((bsz, tlen, width), f32),
                   jax.ShapeDtypeStruct((depth, bsz, heads, RW_N, RW_N), f32)],
        scratch_shapes=scratch,
        input_output_aliases=aliases,
        compiler_params=_params("parallel", "arbitrary"),
        name="rwkv7_chunk_scan" if chunked else "rwkv7_scan",
    )(*args)
    return o, s_new


def _rw_post_kernel(o_ref, r_ref, k_ref, v_ref, g_ref, lw_ref, lbias_ref, rk_ref, y_ref):
    ones_bd = _group_ones(V7X_LANES, RW_N)
    o = o_ref[...]
    mu = _group_sum(o, ones_bd) * (1.0 / RW_N)
    d = o - mu
    var = _group_sum(d * d, ones_bd) * (1.0 / RW_N)
    on = d * lax.rsqrt(var + RW_GN_EPS) * lw_ref[...] + lbias_ref[...]
    rk = r_ref[...].astype(f32) * k_ref[...].astype(f32) * rk_ref[...]
    bonus = _group_sum(rk, ones_bd) * v_ref[...].astype(f32)
    y_ref[...] = ((on + bonus) * g_ref[...].astype(f32)).astype(y_ref.dtype)


def _rw_post(o, r, k, v, g, ln_w, ln_b, r_k):
    n, width = o.shape
    tm = _tile(n, 512)
    blk = pl.BlockSpec((tm, width), lambda i: (i, 0))
    vec = pl.BlockSpec((1, width), lambda i: (0, 0))
    return pl.pallas_call(
        _rw_post_kernel,
        grid=(n // tm,),
        in_specs=[blk] * 5 + [vec] * 3,
        out_specs=blk,
        out_shape=jax.ShapeDtypeStruct((n, width), bf16),
        compiler_params=_params("parallel"),
        name="rwkv7_post",
    )(o, r, k, v, g, ln_w, ln_b, r_k)


def _conv_kernel(*refs, nb, tb, has_state, has_prev):
    sb_ref, sc_ref, sh_ref = refs[:3]
    st_ref = refs[3] if has_state else None
    cw_ref = refs[3 + has_state]
    y_ref, new_ref = refs[4 + has_state + has_prev:]
    ti = pl.program_id(2)
    rows = nb * tb
    w = sb_ref.shape[-1]
    hist = SC_KSIZE - 1

    @pl.when(ti == 0)
    def _():
        new_ref[...] = st_ref[...] if has_state else jnp.zeros((nb, hist, w), f32)

    u = (sc_ref[...].astype(f32) * sh_ref[...].astype(f32)).reshape(rows, w)
    carry = new_ref[...]
    conv = cw_ref[hist:hist + 1, :] * u
    for back in range(1, hist + 1):
        first = [_bcast_rows(carry[:, hist - back + j:hist - back + j + 1, :], nb, tb) for j in range(back)]
        conv = conv + cw_ref[hist - back:hist - back + 1, :] * _shift_rows(u, first, tb, back)
    y_ref[...] = (sb_ref[...].astype(f32).reshape(rows, w) * conv).reshape(nb, tb, w).astype(y_ref.dtype)
    new_ref[...] = u.reshape(nb, tb, w)[:, tb - hist:tb, :]


def _conv(p_sc, st_in, st_prev, layer, depth, conv_w, bsz, tlen):
    width = conv_w.shape[1]
    cw = _tile(width, 1024, align=V7X_LANES)
    ncol = width // cw
    nb, tb = _seq_block(bsz, tlen)
    assert tb >= SC_KSIZE - 1
    has_state = st_in is not None
    p3 = p_sc.reshape(bsz, tlen, 3 * width)
    seg = lambda s: pl.BlockSpec((nb, tb, cw), lambda bi, cj, ti, s=s: (bi, ti, s * ncol + cj))
    st_spec = pl.BlockSpec((None, nb, SC_KSIZE - 1, cw), lambda bi, cj, ti: (layer, bi, 0, cj))
    in_specs = ([seg(0), seg(1), seg(2)] + ([st_spec] if has_state else [])
                + [pl.BlockSpec((SC_KSIZE, cw), lambda bi, cj, ti: (0, cj))])
    args = [p3, p3, p3] + ([st_in] if has_state else []) + [conv_w]
    aliases = _threaded_state(args, in_specs, st_prev, 1)
    y, new = pl.pallas_call(
        functools.partial(_conv_kernel, nb=nb, tb=tb, has_state=has_state, has_prev=st_prev is not None),
        grid=(bsz // nb, ncol, tlen // tb),
        in_specs=in_specs,
        out_specs=[pl.BlockSpec((nb, tb, cw), lambda bi, cj, ti: (bi, ti, cj)), st_spec],
        out_shape=[jax.ShapeDtypeStruct((bsz, tlen, width), bf16),
                   jax.ShapeDtypeStruct((depth, bsz, SC_KSIZE - 1, width), f32)],
        input_output_aliases=aliases,
        compiler_params=_params("parallel", "parallel", "arbitrary"),
        name="short_conv",
    )(*args)
    return y.reshape(bsz * tlen, width), new


def _merge_kernel(a_ref, b_ref, c_ref, ga_ref, gb_ref, gc_ref, pa_ref, pb_ref, pc_ref, m_ref, w_scr):
    @pl.when(pl.program_id(1) == 0)
    def _():
        for s, w_ref in enumerate((pa_ref, pb_ref, pc_ref)):
            w_scr[s] = w_ref[...].astype(bf16)

    m = (_gate_sigmoid(ga_ref[...].astype(f32)) * _dot(a_ref[...], w_scr[0])
         + _gate_sigmoid(gb_ref[...].astype(f32)) * _dot(b_ref[...], w_scr[1])
         + _gate_sigmoid(gc_ref[...].astype(f32)) * _dot(c_ref[...], w_scr[2]))
    m_ref[...] = m.astype(m_ref.dtype)


def _merge(ya, yb, yc, p_gate, w_pa, w_pb, w_pc, layer):
    n = ya.shape[0]
    width, d = w_pa.shape[1:]
    assert w_pb.shape[1] == width and w_pc.shape[1] == width
    tm = _tile(n, 1024)
    tn = _tile(d, 512, align=V7X_LANES)
    nj = d // tn
    act = pl.BlockSpec((tm, width), lambda j, i: (i, 0))
    gate = lambda s: pl.BlockSpec((tm, tn), lambda j, i, s=s: (i, s * nj + j))
    wt = pl.BlockSpec((None, width, tn), lambda j, i: (layer, 0, j))
    return pl.pallas_call(
        _merge_kernel,
        grid=(nj, n // tm),
        in_specs=[act, act, act, gate(0), gate(1), gate(2), wt, wt, wt],
        out_specs=pl.BlockSpec((tm, tn), lambda j, i: (i, j)),
        out_shape=jax.ShapeDtypeStruct((n, d), bf16),
        scratch_shapes=[pltpu.VMEM((3, width, tn), bf16)],
        compiler_params=_params("arbitrary", "arbitrary"),
        name="gated_merge",
    )(ya, yb, yc, p_gate, p_gate, p_gate, w_pa, w_pb, w_pc)


def _oproj_kernel(m_ref, wo_ref, x_ref, g1_ref, g2_ref, x1_ref, h2_ref):
    mix = _dot(m_ref[...], wo_ref[...])
    x1 = x_ref[...] + _rms(mix, g1_ref[...])
    x1_ref[...] = x1
    h2_ref[...] = _rms(x1, g2_ref[...]).astype(h2_ref.dtype)


def _oproj(m, w_o, layer, x, g_post_mix, g_pre_mlp):
    n, d = x.shape
    tm = _tile(n, 512)
    blk = pl.BlockSpec((tm, d), lambda i: (i, 0))
    vec = pl.BlockSpec((1, d), lambda i: (0, 0))
    return pl.pallas_call(
        _oproj_kernel,
        grid=(n // tm,),
        in_specs=[blk, pl.BlockSpec((None, d, d), lambda i: (layer, 0, 0)), blk, vec, vec],
        out_specs=[blk, blk],
        out_shape=[jax.ShapeDtypeStruct((n, d), f32), jax.ShapeDtypeStruct((n, d), bf16)],
        compiler_params=_params("parallel"),
        name="out_proj_norms",
    )(m, w_o, x, g_post_mix, g_pre_mlp)


def _mlp_kernel(h_ref, w1_ref, w2_ref, ff_ref):
    kf = pl.program_id(1)

    @pl.when(kf == 0)
    def _():
        ff_ref[...] = jnp.zeros_like(ff_ref)

    t = jnp.maximum(_dot(h_ref[...], w1_ref[...].astype(bf16)), 0.0)
    ff_ref[...] += _dot((t * t).astype(bf16), w2_ref[...].astype(bf16))


def _mlp(h2, w1, w2, layer):
    n, d = h2.shape
    dff = w1.shape[2]
    tm = _tile(n, 2048)
    fc = _tile(dff, 512, align=V7X_LANES)
    once = pl.Buffered(1)
    return pl.pallas_call(
        _mlp_kernel,
        grid=(n // tm, dff // fc),
        in_specs=[pl.BlockSpec((tm, d), lambda i, kf: (i, 0), pipeline_mode=once),
                  pl.BlockSpec((None, d, fc), lambda i, kf: (layer, 0, kf)),
                  pl.BlockSpec((None, fc, d), lambda i, kf: (layer, kf, 0))],
        out_specs=pl.BlockSpec((tm, d), lambda i, kf: (i, 0), pipeline_mode=once),
        out_shape=jax.ShapeDtypeStruct((n, d), f32),
        compiler_params=_params("parallel", "arbitrary", vmem=V7X_VMEM_BIG_TILE_BYTES),
        name="relu2_mlp",
    )(h2, w1, w2)


def _residual_kernel(*refs, has_next):
    if has_next:
        x1_ref, ff_ref, g_ref, gn_ref, y_ref, h_ref = refs
    else:
        x1_ref, ff_ref, g_ref, y_ref = refs
    y = x1_ref[...] + _rms(ff_ref[...], g_ref[...])
    y_ref[...] = y
    if has_next:
        h_ref[...] = _rms(y, gn_ref[...]).astype(h_ref.dtype)


def _residual(x1, ff, g_post_mlp, g_next):
    n, d = x1.shape
    tm = _tile(n, 512)
    blk = pl.BlockSpec((tm, d), lambda i: (i, 0))
    vec = pl.BlockSpec((1, d), lambda i: (0, 0))
    has_next = g_next is not None
    outs = pl.pallas_call(
        functools.partial(_residual_kernel, has_next=has_next),
        grid=(n // tm,),
        in_specs=[blk, blk, vec] + ([vec] if has_next else []),
        out_specs=[blk] + ([blk] if has_next else []),
        out_shape=[jax.ShapeDtypeStruct((n, d), f32)] + ([jax.ShapeDtypeStruct((n, d), bf16)] if has_next else []),
        compiler_params=_params("parallel"),
        name="mlp_residual_norm",
    )(*([x1, ff, g_post_mlp] + ([g_next] if has_next else [])))
    return (outs[0], outs[1]) if has_next else (outs[0], None)


def _layer(x, h, bsz, tlen, states_in, states_prev, layer, depth, wts, lw, g_next):
    hg_in, rw_in, shift_in, sc_in = states_in
    hg_prev, rw_prev, shift_prev, sc_prev = states_prev
    w_in, offs = wts["w_in"], wts["offs"]
    p_hg = _proj(h, w_in, layer, offs[0], offs[1] - offs[0], "in_proj_hgrn")
    p_rkv = _proj(h, w_in, layer, offs[1], offs[2] - offs[1], "in_proj_rwkv")
    p_lora = _proj(h, w_in, layer, offs[2], offs[3] - offs[2], "in_proj_rwkv_lora")
    p_sc = _proj(h, w_in, layer, offs[3], offs[4] - offs[3], "in_proj_conv", bf16)
    p_gate = _proj(h, w_in, layer, offs[4], offs[5] - offs[4], "in_proj_gate", bf16)

    ya, hg_new = _hgrn(p_hg, lw["lb"], lw["hg_norm"], hg_in, hg_prev, layer, depth, bsz, tlen)

    seq_dtype = bf16 if _rw_mode(bsz, tlen) == "chunk" else f32
    seqs, shift_new = _rw_prep(p_rkv, p_lora, shift_in, shift_prev, layer, depth, lw["rw_prep"], bsz, tlen,
                               seq_dtype)
    r, w, k, v, kn, bb, g = seqs
    o_rw, rw_new = _rw_scan((r, w, k, v, kn, bb), rw_in, rw_prev, layer, depth, bsz, tlen)
    flat = lambda t: t.reshape(bsz * tlen, t.shape[-1])
    yb = _rw_post(flat(o_rw), flat(r), flat(k), flat(v), flat(g), lw["rw_ln_w"], lw["rw_ln_b"], lw["rw_r_k"])

    yc, sc_new = _conv(p_sc, sc_in, sc_prev, layer, depth, lw["sc_conv_w"], bsz, tlen)

    m = _merge(ya, yb, yc, p_gate, wts["w_pa"], wts["w_pb"], wts["w_pc"], layer)
    x1, h2 = _oproj(m, wts["w_o"], layer, x, lw["g_post_mix"], lw["g_pre_mlp"])
    ff = _mlp(h2, wts["w_ff1"], wts["w_ff2"], layer)
    y, h_next = _residual(x1, ff, lw["g_post_mlp"], g_next)
    return y, h_next, (hg_new, rw_new, shift_new, sc_new)


def kernel(x_prompt, x_sample, state_hgrn, state_rwkv, state_rwkv_shift, state_conv, norm_pre_mix, norm_post_mix, norm_pre_mlp, norm_post_mlp, w_in, hg_lb_logits, hg_norm, w_pa, rw_mu, rw_w0, rw_w2, rw_a0, rw_a2, rw_g2, rw_k_k, rw_k_a, rw_r_k, rw_ln_w, rw_ln_b, w_pb, sc_conv_w, w_pc, w_o, w_ff1, w_ff2):
    depth = w_in.shape[0]
    d_model = x_prompt.shape[-1]
    hg_width = hg_norm.shape[1]
    rw_width = rw_w0.shape[1]
    rw_shift_width = rw_mu.shape[1]
    sc_width = sc_conv_w.shape[2]
    off_rw = 4 * hg_width
    off_lora = off_rw + 3 * rw_width
    off_sc = off_rw + rw_shift_width
    off_gate = off_sc + 3 * sc_width
    assert w_in.shape[2] == off_gate + 3 * d_model

    lb_all = _lower_bounds(hg_lb_logits.astype(f32))
    row = lambda a, l: a[l].reshape(1, -1).astype(f32)
    wts = {"w_in": w_in.astype(f32), "offs": (0, off_rw, off_lora, off_sc, off_gate, w_in.shape[2]),
           "w_pa": w_pa.astype(f32), "w_pb": w_pb.astype(f32), "w_pc": w_pc.astype(f32),
           "w_o": w_o.astype(bf16), "w_ff1": w_ff1.astype(f32), "w_ff2": w_ff2.astype(f32)}

    bp, tp, _ = x_prompt.shape
    bs, ts, _ = x_sample.shape
    yp = x_prompt.reshape(bp * tp, d_model).astype(f32)
    ys = x_sample.reshape(bs * ts, d_model).astype(f32)
    hp = _rmsnorm_cast(yp, row(norm_pre_mix, 0))
    hs = _rmsnorm_cast(ys, row(norm_pre_mix, 0))
    rw_lanes = _rw_mode(bs, ts) == "lanes"
    rw_state = state_rwkv.astype(f32)
    sample_in = (state_hgrn.astype(f32), jnp.transpose(rw_state, (0, 2, 3, 4, 1)) if rw_lanes else rw_state,
                 state_rwkv_shift.astype(f32).reshape(depth, bs, 1, rw_shift_width), state_conv.astype(f32))
    new_p = new_s = (None, None, None, None)
    for l in range(depth):
        lw = {
            "g_post_mix": row(norm_post_mix, l), "g_pre_mlp": row(norm_pre_mlp, l),
            "g_post_mlp": row(norm_post_mlp, l),
            "lb": lb_all[l:l + 1], "hg_norm": row(hg_norm, l),
            "rw_prep": (row(rw_mu, l), row(rw_w0, l), rw_w2[l].astype(bf16), row(rw_a0, l), rw_a2[l].astype(bf16),
                        rw_g2[l].astype(bf16), row(rw_k_k, l), row(rw_k_a, l)),
            "rw_ln_w": row(rw_ln_w, l), "rw_ln_b": row(rw_ln_b, l), "rw_r_k": row(rw_r_k, l),
            "sc_conv_w": sc_conv_w[l].astype(f32),
        }
        g_next = row(norm_pre_mix, l + 1) if l + 1 < depth else None
        yp, hp, new_p = _layer(yp, hp, bp, tp, (None, None, None, None), new_p, l, depth, wts, lw, g_next)
        ys, hs, new_s = _layer(ys, hs, bs, ts, sample_in, new_s, l, depth, wts, lw, g_next)
    shift = lambda s, b: s.reshape(depth, b, rw_shift_width)
    return (yp.reshape(bp, tp, d_model), ys.reshape(bs, ts, d_model),
            new_p[0], new_p[1], shift(new_p[2], bp), new_p[3],
            new_s[0], jnp.transpose(new_s[1], (0, 4, 1, 2, 3)) if rw_lanes else new_s[1],
            shift(new_s[2], bs), new_s[3])
```

```python
import functools

import jax
import jax.numpy as jnp
from jax import lax
from jax.experimental import pallas as pl
from jax.experimental.pallas import tpu as pltpu

HG_DK = 128
RW_N = 64
HG_F_MIN = 1e-30
LOG2_E = 1.4426950408889634
RW_GN_EPS = 64e-5
NORM_EPS = 1e-6
SC_KSIZE = 3

V7X_LANES = 128
V7X_SUBLANES = 8
V7X_BF16_ROWS = 16
V7X_VMEM_LIMIT_BYTES = 48 * 1024 * 1024
V7X_VMEM_BIG_TILE_BYTES = 56 * 1024 * 1024
HG_CHUNK = 16
HG_HEADS_PER_STEP = 4
HG_GROUP = 4
RW_PAIR = 2
RW_CHUNK = 16
RW_BLOCK_TOKENS = 128

f32 = jnp.float32
bf16 = jnp.bfloat16


def _tile(n, pref, align=V7X_SUBLANES):
    if n <= pref:
        return n
    for d in range(pref, 0, -1):
        if n % d == 0 and d % align == 0:
            return d
    return n


def _seq_block(bsz, tlen):
    if tlen >= 256:
        return 1, _tile(tlen, 256)
    return _tile(bsz, max(1, 128 // tlen), align=1), tlen


def _params(*sem, vmem=V7X_VMEM_LIMIT_BYTES):
    return pltpu.CompilerParams(dimension_semantics=sem, vmem_limit_bytes=vmem)


def _sigmoid(x):
    return 1.0 / (1.0 + jnp.exp(-x))


def _gate_sigmoid(x):
    return 0.5 * jnp.tanh(0.5 * x) + 0.5


def _dot(a, b):
    return jnp.dot(a, b, preferred_element_type=f32)


def _rms(x, g):
    return x * lax.rsqrt(jnp.mean(x * x, axis=-1, keepdims=True) + NORM_EPS) * g


def _group_ones(width, group):
    r = lax.broadcasted_iota(jnp.int32, (width, width), 0) // group
    c = lax.broadcasted_iota(jnp.int32, (width, width), 1) // group
    return jnp.where(r == c, 1.0, 0.0).astype(bf16)


def _group_sum(z, ones_bd):
    outs = []
    for c in range(z.shape[-1] // V7X_LANES):
        zc = z[:, c * V7X_LANES:(c + 1) * V7X_LANES]
        hi = zc.astype(bf16)
        lo = (zc - hi.astype(f32)).astype(bf16)
        outs.append(_dot(hi, ones_bd) + _dot(lo, ones_bd))
    return outs[0] if len(outs) == 1 else jnp.concatenate(outs, axis=-1)


def _threaded_state(args, in_specs, prev_out, out_index):
    if prev_out is None:
        return {}
    args.append(prev_out)
    in_specs.append(pl.BlockSpec(memory_space=pl.ANY))
    return {len(args) - 1: out_index}


def _lb_kernel(logit_ref, lb_ref):
    z = logit_ref[...]
    depth = z.shape[0]
    m = jnp.max(z, axis=0, keepdims=True)
    e = jnp.exp(z - m)
    p = e / jnp.sum(e, axis=0, keepdims=True)
    acc = jnp.zeros_like(p[0:1])
    for l in range(depth):
        acc = acc + p[l:l + 1]
        lb_ref[l:l + 1, :] = jnp.clip(acc - p[0:1], 0.0, 1.0)


def _lower_bounds(logits):
    return pl.pallas_call(
        _lb_kernel, out_shape=jax.ShapeDtypeStruct(logits.shape, f32), name="hg_lower_bounds",
    )(logits)


def _rmsnorm_kernel(x_ref, g_ref, o_ref):
    o_ref[...] = _rms(x_ref[...], g_ref[...]).astype(o_ref.dtype)


def _rmsnorm_cast(x, g):
    n, d = x.shape
    tm = _tile(n, 512)
    return pl.pallas_call(
        _rmsnorm_kernel,
        grid=(n // tm,),
        in_specs=[pl.BlockSpec((tm, d), lambda i: (i, 0)), pl.BlockSpec((1, d), lambda i: (0, 0))],
        out_specs=pl.BlockSpec((tm, d), lambda i: (i, 0)),
        out_shape=jax.ShapeDtypeStruct((n, d), bf16),
        compiler_params=_params("parallel"),
        name="rmsnorm_cast",
    )(x, g)


def _proj_kernel(a_ref, w_ref, o_ref, wbf_scr):
    @pl.when(pl.program_id(1) == 0)
    def _():
        wbf_scr[...] = w_ref[0].astype(bf16)

    o_ref[...] = _dot(a_ref[...], wbf_scr[...]).astype(o_ref.dtype)


def _proj(h, w_stack, layer, off, width, name, out_dtype=f32):
    n, k = h.shape
    big = out_dtype == bf16
    tm = _tile(n, 2048 if big else 1024)
    tn = _tile(width, 1024, align=V7X_LANES)
    w_spec = pl.BlockSpec((pl.Element(1), pl.Element(k), pl.Element(tn)),
                          lambda j, i: (layer, 0, pl.multiple_of(off + j * tn, V7X_LANES)))
    return pl.pallas_call(
        _proj_kernel,
        grid=(width // tn, n // tm),
        in_specs=[pl.BlockSpec((tm, k), lambda j, i: (i, 0)), w_spec],
        out_specs=pl.BlockSpec((tm, tn), lambda j, i: (i, j)),
        out_shape=jax.ShapeDtypeStruct((n, width), out_dtype),
        scratch_shapes=[pltpu.VMEM((k, tn), bf16)],
        compiler_params=_params("arbitrary", "arbitrary",
                                vmem=V7X_VMEM_BIG_TILE_BYTES if big else V7X_VMEM_LIMIT_BYTES),
        name=name,
    )(h, w_stack)


def _hgrn_kernel(*refs, nb, tb, hb, chunk, has_state, has_prev):
    q_ref, f_ref, i_ref, og_ref, lb_ref, gn_ref, rsel_ref = refs[:7]
    s0_ref = refs[7] if has_state else None
    o_ref, s_ref, st_scr = refs[7 + has_state + has_prev:]
    ti = pl.program_id(2)
    rows = nb * tb
    dk = HG_DK
    nchunk = rows // chunk
    heads = range(hb)

    @pl.when(ti == 0)
    def _():
        for b in range(nb):
            for h in heads:
                st_scr[b, h] = s0_ref[b, h].T if has_state else jnp.zeros((dk, dk), f32)

    grp = HG_GROUP if (tb // chunk) % HG_GROUP == 0 else 1
    row_idx = lax.broadcasted_iota(jnp.int32, (rows, dk), 0)
    t_idx = row_idx % chunk
    pos_in_grp = (row_idx // chunk) % grp
    chunk_r = lax.broadcasted_iota(jnp.int32, (rows, rows), 0) // chunk
    chunk_c = lax.broadcasted_iota(jnp.int32, (rows, rows), 1) // chunk
    chunk_masks = [jnp.where((chunk_r - chunk_c == dist) & (chunk_r // grp == chunk_c // grp), 1.0, 0.0)
                   for dist in range(grp)]
    ngroup = nchunk // grp
    col_group = lax.broadcasted_iota(jnp.int32, (dk, rows), 1) // (grp * chunk)

    def chunk_row(a, s):
        a3 = a.reshape(nchunk, chunk, dk)
        return jnp.broadcast_to(a3[:, s:s + 1, :], (nchunk, chunk, dk)).reshape(rows, dk)

    def head_cols(ref, h):
        return ref[:, :, h * dk:(h + 1) * dk].reshape(rows, dk)

    sub = V7X_SUBLANES
    tpc = chunk // sub

    def tiles(a):
        return a.reshape(nchunk, tpc, sub, dk)

    glogs, q_decs, o_intras, u_alls = [], [], [], []
    for h in heads:
        lb = lb_ref[:, h * dk:(h + 1) * dk]
        hq = head_cols(q_ref, h)
        qq = hq * _gate_sigmoid(hq)
        f = lb + (1.0 - lb) * _sigmoid(head_cols(f_ref, h))
        kk = 1.0 - f
        bcum = jnp.log(jnp.maximum(f, HG_F_MIN))
        step = 1
        while step < chunk:
            bcum = bcum + jnp.where(t_idx >= step, pltpu.roll(bcum, step, 0), 0.0)
            step *= 2
        b_last = chunk_row(bcum, chunk - 1)
        zs = []
        b2 = bcum * LOG2_E
        for s in range(chunk):
            t0 = s // sub

            def col_row(a):
                return jnp.broadcast_to(tiles(a)[:, t0:t0 + 1, s % sub:s % sub + 1, :], (nchunk, 1, sub, dk))

            b2_s, k_s = col_row(b2), col_row(kk)
            parts = [jnp.zeros((nchunk, t0, sub, dk), f32)] if t0 else []
            for tt in range(t0, tpc):
                diff = tiles(b2)[:, tt:tt + 1] - b2_s
                qk = tiles(qq)[:, tt:tt + 1] * k_s
                if tt == t0:
                    z = jnp.where(tiles(t_idx)[:, tt:tt + 1] >= s, qk * jnp.exp2(jnp.minimum(diff, 0.0)), 0.0)
                else:
                    z = qk * jnp.exp2(diff)
                parts.append(z)
            z = parts[0] if len(parts) == 1 else jnp.concatenate(parts, axis=1)
            zs.append(z.reshape(rows, dk).astype(bf16))
        k_dec =(kk * jnp.exp(b_last - bcum)).astype(bf16)
        vv = head_cols(i_ref, h)
        a_rep = _dot(jnp.concatenate(zs, axis=-1), rsel_ref[...])
        scores = a_rep * chunk_masks[0]
        between = jnp.zeros_like(bcum)
        for dist in range(1, grp):
            q_far = (qq * jnp.exp(bcum + between)).astype(bf16)
            far = lax.dot_general(q_far, k_dec, (((1,), (1,)), ((), ())), preferred_element_type=f32)
            scores = scores + far * chunk_masks[dist]
            between = between + pltpu.roll(b_last, dist * chunk, 0)
        reach = jnp.zeros_like(bcum)
        tail = jnp.zeros_like(bcum)
        for dist in range(1, grp):
            reach = reach + jnp.where(pos_in_grp >= dist, pltpu.roll(b_last, dist * chunk, 0), 0.0)
            tail = tail + jnp.where(pos_in_grp + dist < grp, pltpu.roll(b_last, rows - dist * chunk, 0), 0.0)
        glogs.append(bcum + reach)
        q_decs.append((qq * jnp.exp(bcum + reach)).astype(bf16))
        o_intras.append(_dot(scores.astype(bf16), vv.astype(bf16)))
        k_grp = k_dec if grp == 1 else (kk * jnp.exp(b_last - bcum + tail)).astype(bf16)
        v_t = vv.T
        lhs = jnp.concatenate([jnp.where(col_group == g, v_t, 0.0).astype(bf16) for g in range(ngroup)], axis=0)
        u_alls.append(_dot(lhs, k_grp))

    outs = [[] for _ in heads]
    span = grp * chunk
    for b in range(nb):
        sts = [st_scr[b, h] for h in heads]
        for g in range(tb // span):
            gi = b * (tb // span) + g
            r0 = gi * span
            for h in heads:
                inter = lax.dot_general(q_decs[h][r0:r0 + span], sts[h].astype(bf16),
                                        (((1,), (1,)), ((), ())), preferred_element_type=f32)
                outs[h].append(inter + o_intras[h][r0:r0 + span])
                sts[h] = jnp.exp(glogs[h][r0 + span - 1:r0 + span]) * sts[h] + u_alls[h][gi * dk:(gi + 1) * dk]
        for h in heads:
            st_scr[b, h] = sts[h]
    for h in heads:
        o = jnp.concatenate(outs[h], axis=0)
        o = o * lax.rsqrt(jnp.mean(o * o, axis=-1, keepdims=True) + NORM_EPS)
        hog = head_cols(og_ref, h)
        o = o * gn_ref[:, h * dk:(h + 1) * dk] * (hog * _gate_sigmoid(hog))
        o_ref[:, :, h * dk:(h + 1) * dk] = o.reshape(nb, tb, dk).astype(o_ref.dtype)

    @pl.when(ti == pl.num_programs(2) - 1)
    def _():
        for b in range(nb):
            for h in heads:
                s_ref[b, h] = st_scr[b, h].T


def _hgrn(p_hg, lb, gn, s_in, s_prev, layer, depth, bsz, tlen):
    width = p_hg.shape[1] // 4
    heads = width // HG_DK
    chunk = min(HG_CHUNK, tlen)
    nb, tb = _seq_block(bsz, tlen)
    assert tb % chunk == 0
    p3 = p_hg.reshape(bsz, tlen, 4 * width)
    hb = HG_HEADS_PER_STEP if heads % HG_HEADS_PER_STEP == 0 else 1
    hgroups = heads // hb
    wb = hb * HG_DK
    seg = lambda s: pl.BlockSpec((nb, tb, wb), lambda bi, h, ti, s=s: (bi, ti, s * hgroups + h))
    vec = pl.BlockSpec((1, wb), lambda bi, h, ti: (0, h))
    st_spec = pl.BlockSpec((None, nb, hb, HG_DK, HG_DK), lambda bi, h, ti: (layer, bi, h, 0, 0))
    has_state = s_in is not None
    rows = nb * tb
    rsel = (jnp.arange(chunk * HG_DK)[:, None] // HG_DK == jnp.arange(rows)[None, :] % chunk).astype(bf16)
    rsel_spec = pl.BlockSpec((chunk * HG_DK, rows), lambda bi, h, ti: (0, 0))
    in_specs = [seg(0), seg(1), seg(2), seg(3), vec, vec, rsel_spec] + ([st_spec] if has_state else [])
    args = [p3, p3, p3, p3, lb, gn, rsel] + ([s_in] if has_state else [])
    aliases = _threaded_state(args, in_specs, s_prev, 1)
    o, s_new = pl.pallas_call(
        functools.partial(_hgrn_kernel, nb=nb, tb=tb, hb=hb, chunk=chunk, has_state=has_state,
                          has_prev=s_prev is not None),
        grid=(bsz // nb, hgroups, tlen // tb),
        in_specs=in_specs,
        out_specs=[pl.BlockSpec((nb, tb, wb), lambda bi, h, ti: (bi, ti, h)), st_spec],
        out_shape=[jax.ShapeDtypeStruct((bsz, tlen, width), bf16),
                   jax.ShapeDtypeStruct((depth, bsz, heads, HG_DK, HG_DK), f32)],
        scratch_shapes=[pltpu.VMEM((nb, hb, HG_DK, HG_DK), f32)],
        input_output_aliases=aliases,
        compiler_params=_params("parallel", "parallel", "arbitrary"),
        name="hgrn2_scan",
    )(*args)
    return o.reshape(bsz * tlen, width), s_new


def _shift_rows(x, first, tb, by):
    rows = x.shape[0]
    t_idx = lax.broadcasted_iota(jnp.int32, (rows, 1), 0) % tb
    out = pltpu.roll(x, by, 0) if tb > by else x
    for j in range(by):
        out = jnp.where(t_idx == j, first[j], out)
    return out


def _bcast_rows(v3, nb, tb):
    w = v3.shape[-1]
    return jnp.broadcast_to(v3, (nb, tb, w)).reshape(nb * tb, w)


def _rw_prep_kernel(*refs, nb, tb, width, lora, has_state, has_prev):
    pm_ref, pl_ref = refs[:2]
    sh_ref = refs[2] if has_state else None
    mu_ref, w0_ref, w2_ref, a0_ref, a2_ref, g2_ref, kk_ref, ka_ref = refs[2 + has_state:10 + has_state]
    r_ref, w_ref, k_ref, v_ref, kn_ref, b_ref, g_ref, last_ref = refs[10 + has_state + has_prev:]
    ti = pl.program_id(1)
    rows = nb * tb
    tot = last_ref.shape[-1]
    x = jnp.concatenate([pm_ref[...], pl_ref[...]], axis=-1).reshape(rows, tot)

    @pl.when(ti == 0)
    def _():
        last_ref[...] = sh_ref[...] if has_state else jnp.zeros((nb, 1, tot), f32)

    prev = _shift_rows(x, [_bcast_rows(last_ref[...], nb, tb)], tb, 1)
    last_ref[...] = jnp.concatenate([pm_ref[:, tb - 1:tb, :], pl_ref[:, tb - 1:tb, :]], axis=-1)
    xs = x + (prev - x) * mu_ref[...]
    dl, al, gl = lora
    c = 3 * width
    r, kr, vr = xs[:, :width], xs[:, width:2 * width], xs[:, 2 * width:c]
    wd, ad, gd = xs[:, c:c + dl], xs[:, c + dl:c + dl + al], xs[:, c + dl + al:c + dl + al + gl]
    z = -(w0_ref[...] + _dot(jnp.tanh(wd).astype(bf16), w2_ref[...]))
    softplus = jnp.maximum(z, 0.0) + jnp.log(1.0 + jnp.exp(-jnp.abs(z)))
    w_log = -softplus - 0.5
    log_decay = -jnp.exp(w_log)
    a = _gate_sigmoid(a0_ref[...] + _dot(ad.astype(bf16), a2_ref[...]))
    g = _dot(_gate_sigmoid(gd).astype(bf16), g2_ref[...])
    kk = kr * kk_ref[...]
    ones_bd = _group_ones(V7X_LANES, RW_N)
    nrm = jnp.sqrt(_group_sum(kk * kk, ones_bd))
    kk = kk / jnp.maximum(nrm, 1e-12)
    k = kr * (1.0 + (a - 1.0) * ka_ref[...])
    shp = (nb, tb, width)
    r_ref[...] = r.reshape(shp).astype(r_ref.dtype)
    w_ref[...] = log_decay.reshape(shp)
    k_ref[...] = k.reshape(shp).astype(k_ref.dtype)
    v_ref[...] = vr.reshape(shp).astype(v_ref.dtype)
    kn_ref[...] = (-kk).reshape(shp).astype(kn_ref.dtype)
    b_ref[...] = (kk * a).reshape(shp).astype(b_ref.dtype)
    g_ref[...] = g.reshape(shp).astype(g_ref.dtype)


def _rw_prep(p_main, p_lora, shift_in, shift_prev, layer, depth, wts, bsz, tlen, seq_dtype):
    mu, w0, w2, a0, a2, g2, k_k, k_a = wts
    width = w0.shape[1]
    wl = p_lora.shape[1]
    tot = 3 * width + wl
    lora = (w2.shape[0], a2.shape[0], g2.shape[0])
    nb, tb = _seq_block(bsz, tlen)
    has_state = shift_in is not None
    blk = lambda w: pl.BlockSpec((nb, tb, w), lambda bi, ti: (bi, ti, 0))
    full = lambda arr: pl.BlockSpec(arr.shape, lambda bi, ti: (0,) * arr.ndim)
    last_spec = pl.BlockSpec((None, nb, 1, tot), lambda bi, ti: (layer, bi, 0, 0))
    small = [mu, w0, w2, a0, a2, g2, k_k, k_a]
    in_specs = [blk(3 * width), blk(wl)] + ([last_spec] if has_state else []) + [full(s) for s in small]
    args = ([p_main.reshape(bsz, tlen, 3 * width), p_lora.reshape(bsz, tlen, wl)]
            + ([shift_in] if has_state else []) + small)
    aliases = _threaded_state(args, in_specs, shift_prev, 7)
    seq = lambda i: jax.ShapeDtypeStruct((bsz, tlen, width), f32 if i == 1 else seq_dtype)
    outs = pl.pallas_call(
        functools.partial(_rw_prep_kernel, nb=nb, tb=tb, width=width, lora=lora, has_state=has_state,
                          has_prev=shift_prev is not None),
        grid=(bsz // nb, tlen // tb),
        in_specs=in_specs,
        out_specs=[blk(width)] * 7 + [last_spec],
        out_shape=[seq(i) for i in range(7)] + [jax.ShapeDtypeStruct((depth, bsz, 1, tot), f32)],
        input_output_aliases=aliases,
        compiler_params=_params("parallel", "arbitrary"),
        name="rwkv7_prep",
    )(*args)
    return outs[:7], outs[7]


def _rw_state_io(s0_ref, s_ref, st_scr, combos, has_state, ti, last):
    n = RW_N

    @pl.when(ti == 0)
    def _():
        for c, (b, p) in enumerate(combos):
            if has_state:
                st_scr[c] = jnp.concatenate([s0_ref[b, RW_PAIR * p + h] for h in range(RW_PAIR)], axis=-1)
            else:
                st_scr[c] = jnp.zeros((n, RW_PAIR * n), f32)

    def finish():
        @pl.when(ti == last)
        def _():
            for c, (b, p) in enumerate(combos):
                s = st_scr[c]
                for h in range(RW_PAIR):
                    s_ref[b, RW_PAIR * p + h] = s[:, h * n:(h + 1) * n]

    return finish


def _rw_scan_kernel(*refs, nb, tb, pairs, has_state, has_prev):
    r_ref, w_ref, k_ref, v_ref, kn_ref, b_ref = refs[:6]
    s0_ref = refs[6] if has_state else None
    o_ref, s_ref, st_scr, lhs_a, lhs_v, lhs_o = refs[6 + has_state + has_prev:]
    ti = pl.program_id(1)
    n = RW_N
    lanes = RW_PAIR * n
    sub = V7X_SUBLANES
    combos = [(b, p) for b in range(nb) for p in range(pairs)]
    finish = _rw_state_io(s0_ref, s_ref, st_scr, combos, has_state, ti, pl.num_programs(1) - 1)

    ones_bd = _group_ones(lanes, n)
    diag = (lax.broadcasted_iota(jnp.int32, (n, lanes), 0)
            == lax.broadcasted_iota(jnp.int32, (n, lanes), 1) % n)

    def token_group(grp, carry):
        base = pl.multiple_of(grp * sub, sub)

        def row(ref, b, p, j, decay=False):
            tile = ref[b, pl.ds(base, sub), pl.ds(p * lanes, lanes)]
            tile = jnp.exp(tile) if decay else tile
            return jnp.broadcast_to(tile[j:j + 1], (n, lanes))

        o_rows = [[] for _ in combos]
        for j in range(sub):
            for c, (b, p) in enumerate(combos):
                lhs_a[c * n:(c + 1) * n, :] = (st_scr[c] * row(kn_ref, b, p, j)).astype(bf16)
                lhs_v[c * n:(c + 1) * n, :] = jnp.where(diag, row(v_ref, b, p, j), 0.0).astype(bf16)
            sa_all = _dot(lhs_a[...], ones_bd)
            vb_all = _dot(lhs_v[...], ones_bd)
            for c, (b, p) in enumerate(combos):
                s_new = (st_scr[c] * row(w_ref, b, p, j, decay=True)
                         + sa_all[c * n:(c + 1) * n] * row(b_ref, b, p, j)
                         + vb_all[c * n:(c + 1) * n] * row(k_ref, b, p, j))
                st_scr[c] = s_new
                lhs_o[c * n:(c + 1) * n, :] = (s_new * row(r_ref, b, p, j)).astype(bf16)
            o_all = _dot(lhs_o[...], ones_bd)
            for c, (b, p) in enumerate(combos):
                o_rows[c].append(jnp.sum(jnp.where(diag, o_all[c * n:(c + 1) * n], 0.0),
                                         axis=0, keepdims=True))
        for c, (b, p) in enumerate(combos):
            o_ref[b, pl.ds(base, sub), pl.ds(p * lanes, lanes)] = jnp.concatenate(o_rows[c], axis=0)
        return carry

    lax.fori_loop(0, tb // sub, token_group, 0)
    finish()


def _rw_chunk_kernel(*refs, nb, tb, pairs, has_state, has_prev):
    r_ref, w_ref, k_ref, v_ref, kn_ref, b_ref = refs[:6]
    s0_ref = refs[6] if has_state else None
    o_ref, s_ref, st_scr, coef_scr, x_scr = refs[6 + has_state + has_prev:]
    ti = pl.program_id(1)
    n = RW_N
    lanes = RW_PAIR * n
    ln = RW_CHUNK
    combos = [(b, p) for b in range(nb) for p in range(pairs)]
    finish = _rw_state_io(s0_ref, s_ref, st_scr, combos, has_state, ti, pl.num_programs(1) - 1)

    ones_bd = _group_ones(lanes, n)
    t_idx = lax.broadcasted_iota(jnp.int32, (ln, lanes), 0)
    lane_idx = lax.broadcasted_iota(jnp.int32, (ln, lanes), 1)
    row_i = lax.broadcasted_iota(jnp.int32, (4 * ln, 4 * ln), 0)
    col_i = lax.broadcasted_iota(jnp.int32, (4 * ln, 4 * ln), 1)
    same_head = (row_i % (2 * ln)) // ln == (col_i % (2 * ln)) // ln
    strict = jnp.where(row_i < 2 * ln, 1, 0)
    lower2 = jnp.where(same_head & (col_i % ln + strict <= row_i % ln), 1.0, 0.0)
    nt_dims = (((1,), (1,)), ((), ()))

    def by_head(x):
        head0 = lane_idx < n
        return jnp.concatenate([jnp.where(head0, x, 0.0), jnp.where(head0, 0.0, x)], axis=0)

    def stack_heads(x):
        return jnp.concatenate([x[:, :n], x[:, n:]], axis=0)

    def pack_heads(xs):
        return jnp.concatenate([xs[:ln], xs[ln:]], axis=-1)

    half = ln // 2
    solve_cols = list(range(half - 1)) + list(range(half, ln - 1))

    def prepare(ck, c, out):
        b, p = combos[c]
        base = ck * ln
        tile = lambda ref: ref[b, base:base + ln, p * lanes:(p + 1) * lanes].astype(f32)
        lw = tile(w_ref)
        cum = lw
        step = 1
        while step < ln:
            cum = cum + jnp.where(t_idx >= step, pltpu.roll(cum, step, 0), 0.0)
            step *= 2
        c_last = jnp.broadcast_to(cum[ln - 1:ln], (ln, lanes))
        inv_c = jnp.exp(-cum)
        to_end = jnp.exp(c_last - cum)
        kn, bb, kk, vv = tile(kn_ref), tile(b_ref), tile(k_ref), tile(v_ref)
        n_dec = kn * jnp.exp(cum - lw)
        b_dec = bb * inv_c
        k_dec = kk * inv_c
        r_dec = tile(r_ref) * jnp.exp(cum)
        lhs_nr = jnp.concatenate([by_head(n_dec), by_head(r_dec)], axis=0).astype(bf16)
        rhs_bk = jnp.concatenate([by_head(b_dec), by_head(k_dec)], axis=0).astype(bf16)
        sc = (lower2 * lax.dot_general(lhs_nr, rhs_bk, nt_dims, preferred_element_type=f32)).astype(bf16)
        blocks = []
        for s in solve_cols:
            lo = 0 if s < half else half
            blocks.append(jnp.where(t_idx[lo:lo + half] > s,
                                    n_dec[lo:lo + half] * jnp.broadcast_to(b_dec[s:s + 1], (half, lanes)), 0.0))
        coef_scr[ck % 2, c] = _dot(jnp.concatenate(blocks, axis=0).astype(bf16), ones_bd)
        sc_cross = jnp.concatenate([sc[half:ln], sc[ln + half:2 * ln]], axis=0)
        vs = stack_heads(vv)
        upd_v = _dot(vs.T.astype(bf16), by_head(kk * to_end).astype(bf16))
        out.append((lhs_nr, sc[:2 * ln], sc[2 * ln:], vs.astype(bf16), by_head(bb * to_end).astype(bf16),
                    jnp.exp(cum[ln - 1:ln]), sc_cross, upd_v))

    ncombo = len(combos)
    pad = jnp.zeros((2 * ln, n), bf16)

    def advance(ck, pre, fill):
        base = ck * ln
        slot = ck % 2
        per_step = -(-len(fill) // (2 * (half - 1)))
        gs = [lax.dot_general(pre[c][0], st_scr[c].astype(bf16), nt_dims, preferred_element_type=f32)
              for c in range(ncombo)]
        for c in range(ncombo):
            x_scr[c] = pack_heads(gs[c][:2 * ln] + _dot(pre[c][1], jnp.concatenate([pad, pre[c][3]], axis=0)))

        def solve_half(lo, first_block):
            xh = [x_scr[c, lo:lo + half, :] for c in range(ncombo)]
            for si in range(half - 1):
                for c in range(ncombo):
                    coef = coef_scr[slot, c, (first_block + si) * half:(first_block + si + 1) * half, :]
                    xh[c] = xh[c] + coef * jnp.broadcast_to(xh[c][si:si + 1], (half, lanes))
                for _ in range(min(per_step, len(fill))):
                    fill.pop(0)()
            for c in range(ncombo):
                x_scr[c, lo:lo + half, :] = xh[c]

        solve_half(0, 0)
        for c in range(ncombo):
            x_top = jnp.concatenate([x_scr[c, :half, :], jnp.zeros((half, lanes), f32)], axis=0)
            cross = _dot(pre[c][6], jnp.concatenate([stack_heads(x_top).astype(bf16), pad], axis=0))
            x_scr[c, half:, :] = x_scr[c, half:, :] + jnp.concatenate([cross[:half], cross[half:]], axis=-1)
        solve_half(half, half - 1)
        while fill:
            fill.pop(0)()
        xss = [stack_heads(x_scr[c]) for c in range(ncombo)]
        xs_ts = [xs.T.astype(bf16) for xs in xss]
        for c, (b, p) in enumerate(combos):
            _, _, sc_out, vs_bf, rhs_b_end, d_last, _, upd_v = pre[c]
            o = gs[c][2 * ln:] + _dot(sc_out, jnp.concatenate([xss[c].astype(bf16), vs_bf], axis=0))
            o_ref[b, base:base + ln, p * lanes:(p + 1) * lanes] = pack_heads(o)
            st_scr[c] = st_scr[c] * d_last + _dot(xs_ts[c], rhs_b_end) + upd_v

    nck = tb // ln
    pre = []
    for c in range(ncombo):
        prepare(0, c, pre)
    for ck in range(nck):
        nxt = []
        fill = [functools.partial(prepare, ck + 1, c, nxt) for c in range(ncombo)] if ck + 1 < nck else []
        advance(ck, pre, fill)
        pre = nxt
    finish()


def _rw_lanes_kernel(*refs, tlen, has_prev):
    r_ref, w_ref, k_ref, v_ref, kn_ref, b_ref, s0_ref = refs[:7]
    o_ref, s_ref, vec_scr, o_scr = refs[7 + has_prev:]
    n = RW_N
    bsz = V7X_LANES
    sub = V7X_SUBLANES
    s_ref[...] = s0_ref[...]
    for t in range(tlen):
        for slot, ref in enumerate((kn_ref, w_ref, b_ref, k_ref, r_ref, v_ref)):
            x = ref[pl.ds(t, bsz, stride=tlen), :].T
            vec_scr[slot] = jnp.exp(x) if slot == 1 else x

        def value_rows(ig, carry):
            i0 = pl.multiple_of(ig * sub, sub)
            for h in range(RW_PAIR):
                kn_h, w_h, b_h, k_h, r_h = (vec_scr[slot, h * n:(h + 1) * n, :] for slot in range(5))
                v_tile = vec_scr[5, pl.ds(h * n + i0, sub), :]
                o_rows = []
                for ii in range(sub):
                    s_old = s_ref[h, i0 + ii]
                    sa = jnp.sum(s_old * kn_h, axis=0, keepdims=True)
                    s_new = s_old * w_h + sa * b_h + v_tile[ii:ii + 1] * k_h
                    s_ref[h, i0 + ii] = s_new
                    o_rows.append(jnp.sum(s_new * r_h, axis=0, keepdims=True))
                o_scr[pl.ds(h * n + i0, sub), :] = jnp.concatenate(o_rows, axis=0)
            return carry

        lax.fori_loop(0, n // sub, value_rows, 0)
        o_ref[pl.ds(t, bsz, stride=tlen), :] = o_scr[...].T


def _rw_lanes_scan(seqs, s_in, s_prev, layer, depth, bsz, tlen):
    r, w, k, v, kn, bb = seqs
    width = r.shape[-1]
    heads = width // RW_N
    lanes = RW_PAIR * RW_N
    rows = bsz * tlen
    blk = pl.BlockSpec((rows, lanes), lambda p: (0, p))
    st_spec = pl.BlockSpec((None, RW_PAIR, RW_N, RW_N, bsz), lambda p: (layer, p, 0, 0, 0))
    in_specs = [blk] * 6 + [st_spec]
    args = [a.reshape(rows, width) for a in (r, w, k, v, kn, bb)] + [s_in]
    aliases = _threaded_state(args, in_specs, s_prev, 1)
    o, s_new = pl.pallas_call(
        functools.partial(_rw_lanes_kernel, tlen=tlen, has_prev=s_prev is not None),
        grid=(heads // RW_PAIR,),
        in_specs=in_specs,
        out_specs=[blk, st_spec],
        out_shape=[jax.ShapeDtypeStruct((rows, width), f32),
                   jax.ShapeDtypeStruct((depth, heads, RW_N, RW_N, bsz), f32)],
        scratch_shapes=[pltpu.VMEM((6, lanes, bsz), f32), pltpu.VMEM((lanes, bsz), f32)],
        input_output_aliases=aliases,
        compiler_params=_params("parallel"),
        name="rwkv7_lanes_scan",
    )(*args)
    return o.reshape(bsz, tlen, width), s_new


def _rw_mode(bsz, tlen):
    if _tile(tlen, RW_BLOCK_TOKENS) % RW_CHUNK == 0:
        return "chunk"
    return "lanes" if bsz == V7X_LANES else "token"


def _rw_scan(seqs, s_in, s_prev, layer, depth, bsz, tlen):
    if _rw_mode(bsz, tlen) == "lanes" and s_in is not None:
        return _rw_lanes_scan(seqs, s_in, s_prev, layer, depth, bsz, tlen)
    r, w, k, v, kn, bb = seqs
    width = r.shape[-1]
    heads = width // RW_N
    pairs = heads // RW_PAIR
    nb = _tile(bsz, 4, align=1)
    tb = _tile(tlen, RW_BLOCK_TOKENS)
    chunked = tb % RW_CHUNK == 0
    has_state = s_in is not None
    blk = pl.BlockSpec((nb, tb, width), lambda bi, ti: (bi, ti, 0))
    st_spec = pl.BlockSpec((None, nb, heads, RW_N, RW_N), lambda bi, ti: (layer, bi, 0, 0, 0))
    m = nb * pairs * RW_N
    lanes = RW_PAIR * RW_N
    in_specs = [blk] * 6 + ([st_spec] if has_state else [])
    args = [r, w, k, v, kn, bb] + ([s_in] if has_state else [])
    aliases = _threaded_state(args, in_specs, s_prev, 1)
    scratch = [pltpu.VMEM((nb * pairs, RW_N, lanes), f32)]
    if chunked:
        scratch += [pltpu.VMEM((2, nb * pairs, (RW_CHUNK - 2) * (RW_CHUNK // 2), lanes), f32),
                    pltpu.VMEM((nb * pairs, RW_CHUNK, lanes), f32)]
    else:
        scratch += [pltpu.VMEM((m, lanes), bf16)] * 3
    o, s_new = pl.pallas_call(
        functools.partial(_rw_chunk_kernel if chunked else _rw_scan_kernel, nb=nb, tb=tb, pairs=pairs,
                          has_state=has_state, has_prev=s_prev is not None),
        grid=(bsz // nb, tlen // tb),
        in_specs=in_specs,
        out_specs=[blk, st_spec],
        out_shape=[jax.ShapeDtypeStruct((bsz, tlen, width), f32),
                   jax.ShapeDtypeStruct((depth, bsz, heads, RW_N, RW_N), f32)],
        scratch_shapes=scratch,
        input_output_aliases=aliases,
        compiler_params=_params("parallel", "arbitrary"),
        name="rwkv7_chunk_scan" if chunked else "rwkv7_scan",
    )(*args)
    return o, s_new


def _rw_post_kernel(o_ref, r_ref, k_ref, v_ref, g_ref, lw_ref, lbias_ref, rk_ref, y_ref):
    ones_bd = _group_ones(V7X_LANES, RW_N)
    o = o_ref[...]
    mu = _group_sum(o, ones_bd) * (1.0 / RW_N)
    d = o - mu
    var = _group_sum(d * d, ones_bd) * (1.0 / RW_N)
    on = d * lax.rsqrt(var + RW_GN_EPS) * lw_ref[...] + lbias_ref[...]
    rk = r_ref[...].astype(f32) * k_ref[...].astype(f32) * rk_ref[...]
    bonus = _group_sum(rk, ones_bd) * v_ref[...].astype(f32)
    y_ref[...] = ((on + bonus) * g_ref[...].astype(f32)).astype(y_ref.dtype)


def _rw_post(o, r, k, v, g, ln_w, ln_b, r_k):
    n, width = o.shape
    tm = _tile(n, 512)
    blk = pl.BlockSpec((tm, width), lambda i: (i, 0))
    vec = pl.BlockSpec((1, width), lambda i: (0, 0))
    return pl.pallas_call(
        _rw_post_kernel,
        grid=(n // tm,),
        in_specs=[blk] * 5 + [vec] * 3,
        out_specs=blk,
        out_shape=jax.ShapeDtypeStruct((n, width), bf16),
        compiler_params=_params("parallel"),
        name="rwkv7_post",
    )(o, r, k, v, g, ln_w, ln_b, r_k)


def _conv_kernel(*refs, nb, tb, has_state, has_prev):
    sb_ref, sc_ref, sh_ref = refs[:3]
    st_ref = refs[3] if has_state else None
    cw_ref = refs[3 + has_state]
    y_ref, new_ref = refs[4 + has_state + has_prev:]
    ti = pl.program_id(2)
    rows = nb * tb
    w = sb_ref.shape[-1]
    hist = SC_KSIZE - 1

    @pl.when(ti == 0)
    def _():
        new_ref[...] = st_ref[...] if has_state else jnp.zeros((nb, hist, w), f32)

    u = (sc_ref[...].astype(f32) * sh_ref[...].astype(f32)).reshape(rows, w)
    carry = new_ref[...]
    conv = cw_ref[hist:hist + 1, :] * u
    for back in range(1, hist + 1):
        first = [_bcast_rows(carry[:, hist - back + j:hist - back + j + 1, :], nb, tb) for j in range(back)]
        conv = conv + cw_ref[hist - back:hist - back + 1, :] * _shift_rows(u, first, tb, back)
    y_ref[...] = (sb_ref[...].astype(f32).reshape(rows, w) * conv).reshape(nb, tb, w).astype(y_ref.dtype)
    new_ref[...] = u.reshape(nb, tb, w)[:, tb - hist:tb, :]


def _conv(p_sc, st_in, st_prev, layer, depth, conv_w, bsz, tlen):
    width = conv_w.shape[1]
    cw = _tile(width, 1024, align=V7X_LANES)
    ncol = width // cw
    nb, tb = _seq_block(bsz, tlen)
    assert tb >= SC_KSIZE - 1
    has_state = st_in is not None
    p3 = p_sc.reshape(bsz, tlen, 3 * width)
    seg = lambda s: pl.BlockSpec((nb, tb, cw), lambda bi, cj, ti, s=s: (bi, ti, s * ncol + cj))
    st_spec = pl.BlockSpec((None, nb, SC_KSIZE - 1, cw), lambda bi, cj, ti: (layer, bi, 0, cj))
    in_specs = ([seg(0), seg(1), seg(2)] + ([st_spec] if has_state else [])
                + [pl.BlockSpec((SC_KSIZE, cw), lambda bi, cj, ti: (0, cj))])
    args = [p3, p3, p3] + ([st_in] if has_state else []) + [conv_w]
    aliases = _threaded_state(args, in_specs, st_prev, 1)
    y, new = pl.pallas_call(
        functools.partial(_conv_kernel, nb=nb, tb=tb, has_state=has_state, has_prev=st_prev is not None),
        grid=(bsz // nb, ncol, tlen // tb),
        in_specs=in_specs,
        out_specs=[pl.BlockSpec((nb, tb, cw), lambda bi, cj, ti: (bi, ti, cj)), st_spec],
        out_shape=[jax.ShapeDtypeStruct((bsz, tlen, width), bf16),
                   jax.ShapeDtypeStruct((depth, bsz, SC_KSIZE - 1, width), f32)],
        input_output_aliases=aliases,
        compiler_params=_params("parallel", "parallel", "arbitrary"),
        name="short_conv",
    )(*args)
    return y.reshape(bsz * tlen, width), new


def _merge_kernel(a_ref, b_ref, c_ref, ga_ref, gb_ref, gc_ref, pa_ref, pb_ref, pc_ref, m_ref, w_scr):
    @pl.when(pl.program_id(1) == 0)
    def _():
        for s, w_ref in enumerate((pa_ref, pb_ref, pc_ref)):
            w_scr[s] = w_ref[...].astype(bf16)

    m = (_gate_sigmoid(ga_ref[...].astype(f32)) * _dot(a_ref[...], w_scr[0])
         + _gate_sigmoid(gb_ref[...].astype(f32)) * _dot(b_ref[...], w_scr[1])
         + _gate_sigmoid(gc_ref[...].astype(f32)) * _dot(c_ref[...], w_scr[2]))
    m_ref[...] = m.astype(m_ref.dtype)


def _merge(ya, yb, yc, p_gate, w_pa, w_pb, w_pc, layer):
    n = ya.shape[0]
    width, d = w_pa.shape[1:]
    assert w_pb.shape[1] == width and w_pc.shape[1] == width
    tm = _tile(n, 1024)
    tn = _tile(d, 512, align=V7X_LANES)
    nj = d // tn
    act = pl.BlockSpec((tm, width), lambda j, i: (i, 0))
    gate = lambda s: pl.BlockSpec((tm, tn), lambda j, i, s=s: (i, s * nj + j))
    wt = pl.BlockSpec((None, width, tn), lambda j, i: (layer, 0, j))
    return pl.pallas_call(
        _merge_kernel,
        grid=(nj, n // tm),
        in_specs=[act, act, act, gate(0), gate(1), gate(2), wt, wt, wt],
        out_specs=pl.BlockSpec((tm, tn), lambda j, i: (i, j)),
        out_shape=jax.ShapeDtypeStruct((n, d), bf16),
        scratch_shapes=[pltpu.VMEM((3, width, tn), bf16)],
        compiler_params=_params("arbitrary", "arbitrary"),
        name="gated_merge",
    )(ya, yb, yc, p_gate, p_gate, p_gate, w_pa, w_pb, w_pc)


def _oproj_kernel(m_ref, wo_ref, x_ref, g1_ref, g2_ref, x1_ref, h2_ref):
    mix = _dot(m_ref[...], wo_ref[...])
    x1 = x_ref[...] + _rms(mix, g1_ref[...])
    x1_ref[...] = x1
    h2_ref[...] = _rms(x1, g2_ref[...]).astype(h2_ref.dtype)


def _oproj(m, w_o, layer, x, g_post_mix, g_pre_mlp):
    n, d = x.shape
    tm = _tile(n, 512)
    blk = pl.BlockSpec((tm, d), lambda i: (i, 0))
    vec = pl.BlockSpec((1, d), lambda i: (0, 0))
    return pl.pallas_call(
        _oproj_kernel,
        grid=(n // tm,),
        in_specs=[blk, pl.BlockSpec((None, d, d), lambda i: (layer, 0, 0)), blk, vec, vec],
        out_specs=[blk, blk],
        out_shape=[jax.ShapeDtypeStruct((n, d), f32), jax.ShapeDtypeStruct((n, d), bf16)],
        compiler_params=_params("parallel"),
        name="out_proj_norms",
    )(m, w_o, x, g_post_mix, g_pre_mlp)


def _mlp_kernel(h_ref, w1_ref, w2_ref, ff_ref):
    kf = pl.program_id(1)

    @pl.when(kf == 0)
    def _():
        ff_ref[...] = jnp.zeros_like(ff_ref)

    t = jnp.maximum(_dot(h_ref[...], w1_ref[...].astype(bf16)), 0.0)
    ff_ref[...] += _dot((t * t).astype(bf16), w2_ref[...].astype(bf16))


def _mlp(h2, w1, w2, layer):
    n, d = h2.shape
    dff = w1.shape[2]
    tm = _tile(n, 2048)
    fc = _tile(dff, 512, align=V7X_LANES)
    once = pl.Buffered(1)
    return pl.pallas_call(
        _mlp_kernel,
        grid=(n // tm, dff // fc),
        in_specs=[pl.BlockSpec((tm, d), lambda i, kf: (i, 0), pipeline_mode=once),
                  pl.BlockSpec((None, d, fc), lambda i, kf: (layer, 0, kf)),
                  pl.BlockSpec((None, fc, d), lambda i, kf: (layer, kf, 0))],
        out_specs=pl.BlockSpec((tm, d), lambda i, kf: (i, 0), pipeline_mode=once),
        out_shape=jax.ShapeDtypeStruct((n, d), f32),
        compiler_params=_params("parallel", "arbitrary", vmem=V7X_VMEM_BIG_TILE_BYTES),
        name="relu2_mlp",
    )(h2, w1, w2)


def _residual_kernel(*refs, has_next):
    if has_next:
        x1_ref, ff_ref, g_ref, gn_ref, y_ref, h_ref = refs
    else:
        x1_ref, ff_ref, g_ref, y_ref = refs
    y = x1_ref[...] + _rms(ff_ref[...], g_ref[...])
    y_ref[...] = y
    if has_next:
        h_ref[...] = _rms(y, gn_ref[...]).astype(h_ref.dtype)


def _residual(x1, ff, g_post_mlp, g_next):
    n, d = x1.shape
    tm = _tile(n, 512)
    blk = pl.BlockSpec((tm, d), lambda i: (i, 0))
    vec = pl.BlockSpec((1, d), lambda i: (0, 0))
    has_next = g_next is not None
    outs = pl.pallas_call(
        functools.partial(_residual_kernel, has_next=has_next),
        grid=(n // tm,),
        in_specs=[blk, blk, vec] + ([vec] if has_next else []),
        out_specs=[blk] + ([blk] if has_next else []),
        out_shape=[jax.ShapeDtypeStruct((n, d), f32)] + ([jax.ShapeDtypeStruct((n, d), bf16)] if has_next else []),
        compiler_params=_params("parallel"),
        name="mlp_residual_norm",
    )(*([x1, ff, g_post_mlp] + ([g_next] if has_next else [])))
    return (outs[0], outs[1]) if has_next else (outs[0], None)


def _layer(x, h, bsz, tlen, states_in, states_prev, layer, depth, wts, lw, g_next):
    hg_in, rw_in, shift_in, sc_in = states_in
    hg_prev, rw_prev, shift_prev, sc_prev = states_prev
    w_in, offs = wts["w_in"], wts["offs"]
    p_hg = _proj(h, w_in, layer, offs[0], offs[1] - offs[0], "in_proj_hgrn")
    p_rkv = _proj(h, w_in, layer, offs[1], offs[2] - offs[1], "in_proj_rwkv")
    p_lora = _proj(h, w_in, layer, offs[2], offs[3] - offs[2], "in_proj_rwkv_lora")
    p_sc = _proj(h, w_in, layer, offs[3], offs[4] - offs[3], "in_proj_conv", bf16)
    p_gate = _proj(h, w_in, layer, offs[4], offs[5] - offs[4], "in_proj_gate", bf16)

    ya, hg_new = _hgrn(p_hg, lw["lb"], lw["hg_norm"], hg_in, hg_prev, layer, depth, bsz, tlen)

    seq_dtype = bf16 if _rw_mode(bsz, tlen) == "chunk" else f32
    seqs, shift_new = _rw_prep(p_rkv, p_lora, shift_in, shift_prev, layer, depth, lw["rw_prep"], bsz, tlen,
                               seq_dtype)
    r, w, k, v, kn, bb, g = seqs
    o_rw, rw_new = _rw_scan((r, w, k, v, kn, bb), rw_in, rw_prev, layer, depth, bsz, tlen)
    flat = lambda t: t.reshape(bsz * tlen, t.shape[-1])
    yb = _rw_post(flat(o_rw), flat(r), flat(k), flat(v), flat(g), lw["rw_ln_w"], lw["rw_ln_b"], lw["rw_r_k"])

    yc, sc_new = _conv(p_sc, sc_in, sc_prev, layer, depth, lw["sc_conv_w"], bsz, tlen)

    m = _merge(ya, yb, yc, p_gate, wts["w_pa"], wts["w_pb"], wts["w_pc"], layer)
    x1, h2 = _oproj(m, wts["w_o"], layer, x, lw["g_post_mix"], lw["g_pre_mlp"])
    ff = _mlp(h2, wts["w_ff1"], wts["w_ff2"], layer)
    y, h_next = _residual(x1, ff, lw["g_post_mlp"], g_next)
    return y, h_next, (hg_new, rw_new, shift_new, sc_new)


def kernel(x_prompt, x_sample, state_hgrn, state_rwkv, state_rwkv_shift, state_conv, norm_pre_mix, norm_post_mix, norm_pre_mlp, norm_post_mlp, w_in, hg_lb_logits, hg_norm, w_pa, rw_mu, rw_w0, rw_w2, rw_a0, rw_a2, rw_g2, rw_k_k, rw_k_a, rw_r_k, rw_ln_w, rw_ln_b, w_pb, sc_conv_w, w_pc, w_o, w_ff1, w_ff2):
    depth = w_in.shape[0]
    d_model = x_prompt.shape[-1]
    hg_width = hg_norm.shape[1]
    rw_width = rw_w0.shape[1]
    rw_shift_width = rw_mu.shape[1]
    sc_width = sc_conv_w.shape[2]
    off_rw = 4 * hg_width
    off_lora = off_rw + 3 * rw_width
    off_sc = off_rw + rw_shift_width
    off_gate = off_sc + 3 * sc_width
    assert w_in.shape[2] == off_gate + 3 * d_model

    lb_all = _lower_bounds(hg_lb_logits.astype(f32))
    row = lambda a, l: a[l].reshape(1, -1).astype(f32)
    wts = {"w_in": w_in.astype(f32), "offs": (0, off_rw, off_lora, off_sc, off_gate, w_in.shape[2]),
           "w_pa": w_pa.astype(f32), "w_pb": w_pb.astype(f32), "w_pc": w_pc.astype(f32),
           "w_o": w_o.astype(bf16), "w_ff1": w_ff1.astype(f32), "w_ff2": w_ff2.astype(f32)}

    bp, tp, _ = x_prompt.shape
    bs, ts, _ = x_sample.shape
    yp = x_prompt.reshape(bp * tp, d_model).astype(f32)
    ys = x_sample.reshape(bs * ts, d_model).astype(f32)
    hp = _rmsnorm_cast(yp, row(norm_pre_mix, 0))
    hs = _rmsnorm_cast(ys, row(norm_pre_mix, 0))
    rw_lanes = _rw_mode(bs, ts) == "lanes"
    rw_state = state_rwkv.astype(f32)
    sample_in = (state_hgrn.astype(f32), jnp.transpose(rw_state, (0, 2, 3, 4, 1)) if rw_lanes else rw_state,
                 state_rwkv_shift.astype(f32).reshape(depth, bs, 1, rw_shift_width), state_conv.astype(f32))
    new_p = new_s = (None, None, None, None)
    for l in range(depth):
        lw = {
            "g_post_mix": row(norm_post_mix, l), "g_pre_mlp": row(norm_pre_mlp, l),
            "g_post_mlp": row(norm_post_mlp, l),
            "lb": lb_all[l:l + 1], "hg_norm": row(hg_norm, l),
            "rw_prep": (row(rw_mu, l), row(rw_w0, l), rw_w2[l].astype(bf16), row(rw_a0, l), rw_a2[l].astype(bf16),
                        rw_g2[l].astype(bf16), row(rw_k_k, l), row(rw_k_a, l)),
            "rw_ln_w": row(rw_ln_w, l), "rw_ln_b": row(rw_ln_b, l), "rw_r_k": row(rw_r_k, l),
            "sc_conv_w": sc_conv_w[l].astype(f32),
        }
        g_next = row(norm_pre_mix, l + 1) if l + 1 < depth else None
        yp, hp, new_p = _layer(yp, hp, bp, tp, (None, None, None, None), new_p, l, depth, wts, lw, g_next)
        ys, hs, new_s = _layer(ys, hs, bs, ts, sample_in, new_s, l, depth, wts, lw, g_next)
    shift = lambda s, b: s.reshape(depth, b, rw_shift_width)
    return (yp.reshape(bp, tp, d_model), ys.reshape(bs, ts, d_model),
            new_p[0], new_p[1], shift(new_p[2], bp), new_p[3],
            new_s[0], jnp.transpose(new_s[1], (0, 4, 1, 2, 3)) if rw_lanes else new_s[1],
            shift(new_s[2], bs), new_s[3])
```

```python
import functools

import jax
import jax.numpy as jnp
from jax import lax
from jax.experimental import pallas as pl
from jax.experimental.pallas import tpu as pltpu

HG_DK = 128
RW_N = 64
HG_F_MIN = 1e-30
LOG2_E = 1.4426950408889634
RW_GN_EPS = 64e-5
NORM_EPS = 1e-6
SC_KSIZE = 3

V7X_LANES = 128
V7X_SUBLANES = 8
V7X_BF16_ROWS = 16
V7X_VMEM_LIMIT_BYTES = 48 * 1024 * 1024
V7X_VMEM_BIG_TILE_BYTES = 56 * 1024 * 1024
HG_CHUNK = 16
HG_HEADS_PER_STEP = 4
HG_GROUP = 4
RW_PAIR = 2
RW_CHUNK = 16
RW_BLOCK_TOKENS = 128

f32 = jnp.float32
bf16 = jnp.bfloat16


def _tile(n, pref, align=V7X_SUBLANES):
    if n <= pref:
        return n
    for d in range(pref, 0, -1):
        if n % d == 0 and d % align == 0:
            return d
    return n


def _seq_block(bsz, tlen):
    if tlen >= 256:
        return 1, _tile(tlen, 256)
    return _tile(bsz, max(1, 128 // tlen), align=1), tlen


def _params(*sem, vmem=V7X_VMEM_LIMIT_BYTES):
    return pltpu.CompilerParams(dimension_semantics=sem, vmem_limit_bytes=vmem)


def _sigmoid(x):
    return 1.0 / (1.0 + jnp.exp(-x))


def _gate_sigmoid(x):
    return 0.5 * jnp.tanh(0.5 * x) + 0.5


def _dot(a, b):
    return jnp.dot(a, b, preferred_element_type=f32)


def _rms(x, g):
    return x * lax.rsqrt(jnp.mean(x * x, axis=-1, keepdims=True) + NORM_EPS) * g


def _group_ones(width, group):
    r = lax.broadcasted_iota(jnp.int32, (width, width), 0) // group
    c = lax.broadcasted_iota(jnp.int32, (width, width), 1) // group
    return jnp.where(r == c, 1.0, 0.0).astype(bf16)


def _group_sum(z, ones_bd):
    outs = []
    for c in range(z.shape[-1] // V7X_LANES):
        zc = z[:, c * V7X_LANES:(c + 1) * V7X_LANES]
        hi = zc.astype(bf16)
        lo = (zc - hi.astype(f32)).astype(bf16)
        outs.append(_dot(hi, ones_bd) + _dot(lo, ones_bd))
    return outs[0] if len(outs) == 1 else jnp.concatenate(outs, axis=-1)


def _threaded_state(args, in_specs, prev_out, out_index):
    if prev_out is None:
        return {}
    args.append(prev_out)
    in_specs.append(pl.BlockSpec(memory_space=pl.ANY))
    return {len(args) - 1: out_index}


def _lb_kernel(logit_ref, lb_ref):
    z = logit_ref[...]
    depth = z.shape[0]
    m = jnp.max(z, axis=0, keepdims=True)
    e = jnp.exp(z - m)
    p = e / jnp.sum(e, axis=0, keepdims=True)
    acc = jnp.zeros_like(p[0:1])
    for l in range(depth):
        acc = acc + p[l:l + 1]
        lb_ref[l:l + 1, :] = jnp.clip(acc - p[0:1], 0.0, 1.0)


def _lower_bounds(logits):
    return pl.pallas_call(
        _lb_kernel, out_shape=jax.ShapeDtypeStruct(logits.shape, f32), name="hg_lower_bounds",
    )(logits)


def _rmsnorm_kernel(x_ref, g_ref, o_ref):
    o_ref[...] = _rms(x_ref[...], g_ref[...]).astype(o_ref.dtype)


def _rmsnorm_cast(x, g):
    n, d = x.shape
    tm = _tile(n, 512)
    return pl.pallas_call(
        _rmsnorm_kernel,
        grid=(n // tm,),
        in_specs=[pl.BlockSpec((tm, d), lambda i: (i, 0)), pl.BlockSpec((1, d), lambda i: (0, 0))],
        out_specs=pl.BlockSpec((tm, d), lambda i: (i, 0)),
        out_shape=jax.ShapeDtypeStruct((n, d), bf16),
        compiler_params=_params("parallel"),
        name="rmsnorm_cast",
    )(x, g)


def _proj_kernel(a_ref, w_ref, o_ref, wbf_scr):
    @pl.when(pl.program_id(1) == 0)
    def _():
        wbf_scr[...] = w_ref[0].astype(bf16)

    o_ref[...] = _dot(a_ref[...], wbf_scr[...]).astype(o_ref.dtype)


def _proj(h, w_stack, layer, off, width, name, out_dtype=f32):
    n, k = h.shape
    big = out_dtype == bf16
    tm = _tile(n, 2048 if big else 1024)
    tn = _tile(width, 1024, align=V7X_LANES)
    w_spec = pl.BlockSpec((pl.Element(1), pl.Element(k), pl.Element(tn)),
                          lambda j, i: (layer, 0, pl.multiple_of(off + j * tn, V7X_LANES)))
    return pl.pallas_call(
        _proj_kernel,
        grid=(width // tn, n // tm),
        in_specs=[pl.BlockSpec((tm, k), lambda j, i: (i, 0)), w_spec],
        out_specs=pl.BlockSpec((tm, tn), lambda j, i: (i, j)),
        out_shape=jax.ShapeDtypeStruct((n, width), out_dtype),
        scratch_shapes=[pltpu.VMEM((k, tn), bf16)],
        compiler_params=_params("arbitrary", "arbitrary",
                                vmem=V7X_VMEM_BIG_TILE_BYTES if big else V7X_VMEM_LIMIT_BYTES),
        name=name,
    )(h, w_stack)


def _hgrn_kernel(*refs, nb, tb, hb, chunk, has_state, has_prev):
    q_ref, f_ref, i_ref, og_ref, lb_ref, gn_ref, rsel_ref = refs[:7]
    s0_ref = refs[7] if has_state else None
    o_ref, s_ref, st_scr = refs[7 + has_state + has_prev:]
    ti = pl.program_id(2)
    rows = nb * tb
    dk = HG_DK
    nchunk = rows // chunk
    heads = range(hb)

    @pl.when(ti == 0)
    def _():
        for b in range(nb):
            for h in heads:
                st_scr[b, h] = s0_ref[b, h].T if has_state else jnp.zeros((dk, dk), f32)

    grp = HG_GROUP if (tb // chunk) % HG_GROUP == 0 else 1
    row_idx = lax.broadcasted_iota(jnp.int32, (rows, dk), 0)
    t_idx = row_idx % chunk
    pos_in_grp = (row_idx // chunk) % grp
    chunk_r = lax.broadcasted_iota(jnp.int32, (rows, rows), 0) // chunk
    chunk_c = lax.broadcasted_iota(jnp.int32, (rows, rows), 1) // chunk
    chunk_masks = [jnp.where((chunk_r - chunk_c == dist) & (chunk_r // grp == chunk_c // grp), 1.0, 0.0)
                   for dist in range(grp)]
    ngroup = nchunk // grp
    col_group = lax.broadcasted_iota(jnp.int32, (dk, rows), 1) // (grp * chunk)

    def chunk_row(a, s):
        a3 = a.reshape(nchunk, chunk, dk)
        return jnp.broadcast_to(a3[:, s:s + 1, :], (nchunk, chunk, dk)).reshape(rows, dk)

    def head_cols(ref, h):
        return ref[:, :, h * dk:(h + 1) * dk].reshape(rows, dk)

    sub = V7X_SUBLANES
    tpc = chunk // sub

    def tiles(a):
        return a.reshape(nchunk, tpc, sub, dk)

    span = grp * chunk

    def state_steps(h, q_dec, glog, o_intra, u_all):
        outs = []

        def step(b, g):
            gi = b * (tb // span) + g
            r0 = gi * span
            st = st_scr[b, h]
            inter = lax.dot_general(q_dec[r0:r0 + span], st.astype(bf16),
                                    (((1,), (1,)), ((), ())), preferred_element_type=f32)
            outs.append(inter + o_intra[r0:r0 + span])
            st_scr[b, h] = jnp.exp(glog[r0 + span - 1:r0 + span]) * st + u_all[gi * dk:(gi + 1) * dk]

        def finish():
            o = jnp.concatenate(outs, axis=0)
            o = o * lax.rsqrt(jnp.mean(o * o, axis=-1, keepdims=True) + NORM_EPS)
            hog = head_cols(og_ref, h)
            o = o * gn_ref[:, h * dk:(h + 1) * dk] * (hog * _gate_sigmoid(hog))
            o_ref[:, :, h * dk:(h + 1) * dk] = o.reshape(nb, tb, dk).astype(o_ref.dtype)

        return [functools.partial(step, b, g) for b in range(nb) for g in range(tb // span)] + [finish]

    fill = []
    for h in heads:
        per_col = -(-len(fill) // chunk)
        lb = lb_ref[:, h * dk:(h + 1) * dk]
        hq = head_cols(q_ref, h)
        qq = hq * _gate_sigmoid(hq)
        f = lb + (1.0 - lb) * _sigmoid(head_cols(f_ref, h))
        kk = 1.0 - f
        bcum = jnp.log(jnp.maximum(f, HG_F_MIN))
        step = 1
        while step < chunk:
            bcum = bcum + jnp.where(t_idx >= step, pltpu.roll(bcum, step, 0), 0.0)
            step *= 2
        b_last = chunk_row(bcum, chunk - 1)
        zs = []
        b2 = bcum * LOG2_E
        for s in range(chunk):
            t0 = s // sub

            def col_row(a):
                return jnp.broadcast_to(tiles(a)[:, t0:t0 + 1, s % sub:s % sub + 1, :], (nchunk, 1, sub, dk))

            b2_s, k_s = col_row(b2), col_row(kk)
            parts = [jnp.zeros((nchunk, t0, sub, dk), f32)] if t0 else []
            for tt in range(t0, tpc):
                diff = tiles(b2)[:, tt:tt + 1] - b2_s
                qk = tiles(qq)[:, tt:tt + 1] * k_s
                if tt == t0:
                    z = jnp.where(tiles(t_idx)[:, tt:tt + 1] >= s, qk * jnp.exp2(jnp.minimum(diff, 0.0)), 0.0)
                else:
                    z = qk * jnp.exp2(diff)
                parts.append(z)
            z = parts[0] if len(parts) == 1 else jnp.concatenate(parts, axis=1)
            zs.append(z.reshape(rows, dk).astype(bf16))
            for _ in range(min(per_col, len(fill))):
                fill.pop(0)()
        k_dec = (kk * jnp.exp(b_last - bcum)).astype(bf16)
        vv = head_cols(i_ref, h)
        a_rep = _dot(jnp.concatenate(zs, axis=-1), rsel_ref[...])
        scores = a_rep * chunk_masks[0]
        between = jnp.zeros_like(bcum)
        for dist in range(1, grp):
            q_far = (qq * jnp.exp(bcum + between)).astype(bf16)
            far = lax.dot_general(q_far, k_dec, (((1,), (1,)), ((), ())), preferred_element_type=f32)
            scores = scores + far * chunk_masks[dist]
            between = between + pltpu.roll(b_last, dist * chunk, 0)
        reach = jnp.zeros_like(bcum)
        tail = jnp.zeros_like(bcum)
        for dist in range(1, grp):
            reach = reach + jnp.where(pos_in_grp >= dist, pltpu.roll(b_last, dist * chunk, 0), 0.0)
            tail = tail + jnp.where(pos_in_grp + dist < grp, pltpu.roll(b_last, rows - dist * chunk, 0), 0.0)
        glog = bcum + reach
        q_dec = (qq * jnp.exp(glog)).astype(bf16)
        o_intra = _dot(scores.astype(bf16), vv.astype(bf16))
        k_grp = k_dec if grp == 1 else (kk * jnp.exp(b_last - bcum + tail)).astype(bf16)
        v_t = vv.T
        lhs = jnp.concatenate([jnp.where(col_group == g, v_t, 0.0).astype(bf16) for g in range(ngroup)], axis=0)
        u_all = _dot(lhs, k_grp)
        while fill:
            fill.pop(0)()
        fill = state_steps(h, q_dec, glog, o_intra, u_all)
    while fill:
        fill.pop(0)()

    @pl.when(ti == pl.num_programs(2) - 1)
    def _():
        for b in range(nb):
            for h in heads:
                s_ref[b, h] = st_scr[b, h].T


def _hgrn(p_hg, lb, gn, s_in, s_prev, layer, depth, bsz, tlen):
    width = p_hg.shape[1] // 4
    heads = width // HG_DK
    chunk = min(HG_CHUNK, tlen)
    nb, tb = _seq_block(bsz, tlen)
    assert tb % chunk == 0
    p3 = p_hg.reshape(bsz, tlen, 4 * width)
    hb = HG_HEADS_PER_STEP if heads % HG_HEADS_PER_STEP == 0 else 1
    hgroups = heads // hb
    wb = hb * HG_DK
    seg = lambda s: pl.BlockSpec((nb, tb, wb), lambda bi, h, ti, s=s: (bi, ti, s * hgroups + h))
    vec = pl.BlockSpec((1, wb), lambda bi, h, ti: (0, h))
    st_spec = pl.BlockSpec((None, nb, hb, HG_DK, HG_DK), lambda bi, h, ti: (layer, bi, h, 0, 0))
    has_state = s_in is not None
    rows = nb * tb
    rsel = (jnp.arange(chunk * HG_DK)[:, None] // HG_DK == jnp.arange(rows)[None, :] % chunk).astype(bf16)
    rsel_spec = pl.BlockSpec((chunk * HG_DK, rows), lambda bi, h, ti: (0, 0))
    in_specs = [seg(0), seg(1), seg(2), seg(3), vec, vec, rsel_spec] + ([st_spec] if has_state else [])
    args = [p3, p3, p3, p3, lb, gn, rsel] + ([s_in] if has_state else [])
    aliases = _threaded_state(args, in_specs, s_prev, 1)
    o, s_new = pl.pallas_call(
        functools.partial(_hgrn_kernel, nb=nb, tb=tb, hb=hb, chunk=chunk, has_state=has_state,
                          has_prev=s_prev is not None),
        grid=(bsz // nb, hgroups, tlen // tb),
        in_specs=in_specs,
        out_specs=[pl.BlockSpec((nb, tb, wb), lambda bi, h, ti: (bi, ti, h)), st_spec],
        out_shape=[jax.ShapeDtypeStruct((bsz, tlen, width), bf16),
                   jax.ShapeDtypeStruct((depth, bsz, heads, HG_DK, HG_DK), f32)],
        scratch_shapes=[pltpu.VMEM((nb, hb, HG_DK, HG_DK), f32)],
        input_output_aliases=aliases,
        compiler_params=_params("parallel", "parallel", "arbitrary"),
        name="hgrn2_scan",
    )(*args)
    return o.reshape(bsz * tlen, width), s_new


def _shift_rows(x, first, tb, by):
    rows = x.shape[0]
    t_idx = lax.broadcasted_iota(jnp.int32, (rows, 1), 0) % tb
    out = pltpu.roll(x, by, 0) if tb > by else x
    for j in range(by):
        out = jnp.where(t_idx == j, first[j], out)
    return out


def _bcast_rows(v3, nb, tb):
    w = v3.shape[-1]
    return jnp.broadcast_to(v3, (nb, tb, w)).reshape(nb * tb, w)


def _rw_prep_kernel(*refs, nb, tb, width, lora, has_state, has_prev):
    pm_ref, pl_ref = refs[:2]
    sh_ref = refs[2] if has_state else None
    mu_ref, w0_ref, w2_ref, a0_ref, a2_ref, g2_ref, kk_ref, ka_ref = refs[2 + has_state:10 + has_state]
    r_ref, w_ref, k_ref, v_ref, kn_ref, b_ref, g_ref, last_ref = refs[10 + has_state + has_prev:]
    ti = pl.program_id(1)
    rows = nb * tb
    tot = last_ref.shape[-1]
    x = jnp.concatenate([pm_ref[...], pl_ref[...]], axis=-1).reshape(rows, tot)

    @pl.when(ti == 0)
    def _():
        last_ref[...] = sh_ref[...] if has_state else jnp.zeros((nb, 1, tot), f32)

    prev = _shift_rows(x, [_bcast_rows(last_ref[...], nb, tb)], tb, 1)
    last_ref[...] = jnp.concatenate([pm_ref[:, tb - 1:tb, :], pl_ref[:, tb - 1:tb, :]], axis=-1)
    xs = x + (prev - x) * mu_ref[...]
    dl, al, gl = lora
    c = 3 * width
    r, kr, vr = xs[:, :width], xs[:, width:2 * width], xs[:, 2 * width:c]
    wd, ad, gd = xs[:, c:c + dl], xs[:, c + dl:c + dl + al], xs[:, c + dl + al:c + dl + al + gl]
    z = -(w0_ref[...] + _dot(jnp.tanh(wd).astype(bf16), w2_ref[...]))
    softplus = jnp.maximum(z, 0.0) + jnp.log(1.0 + jnp.exp(-jnp.abs(z)))
    w_log = -softplus - 0.5
    log_decay = -jnp.exp(w_log)
    a = _gate_sigmoid(a0_ref[...] + _dot(ad.astype(bf16), a2_ref[...]))
    g = _dot(_gate_sigmoid(gd).astype(bf16), g2_ref[...])
    kk = kr * kk_ref[...]
    ones_bd = _group_ones(V7X_LANES, RW_N)
    nrm = jnp.sqrt(_group_sum(kk * kk, ones_bd))
    kk = kk / jnp.maximum(nrm, 1e-12)
    k = kr * (1.0 + (a - 1.0) * ka_ref[...])
    shp = (nb, tb, width)
    r_ref[...] = r.reshape(shp).astype(r_ref.dtype)
    w_ref[...] = log_decay.reshape(shp)
    k_ref[...] = k.reshape(shp).astype(k_ref.dtype)
    v_ref[...] = vr.reshape(shp).astype(v_ref.dtype)
    kn_ref[...] = (-kk).reshape(shp).astype(kn_ref.dtype)
    b_ref[...] = (kk * a).reshape(shp).astype(b_ref.dtype)
    g_ref[...] = g.reshape(shp).astype(g_ref.dtype)


def _rw_prep(p_main, p_lora, shift_in, shift_prev, layer, depth, wts, bsz, tlen, seq_dtype):
    mu, w0, w2, a0, a2, g2, k_k, k_a = wts
    width = w0.shape[1]
    wl = p_lora.shape[1]
    tot = 3 * width + wl
    lora = (w2.shape[0], a2.shape[0], g2.shape[0])
    nb, tb = _seq_block(bsz, tlen)
    has_state = shift_in is not None
    blk = lambda w: pl.BlockSpec((nb, tb, w), lambda bi, ti: (bi, ti, 0))
    full = lambda arr: pl.BlockSpec(arr.shape, lambda bi, ti: (0,) * arr.ndim)
    last_spec = pl.BlockSpec((None, nb, 1, tot), lambda bi, ti: (layer, bi, 0, 0))
    small = [mu, w0, w2, a0, a2, g2, k_k, k_a]
    in_specs = [blk(3 * width), blk(wl)] + ([last_spec] if has_state else []) + [full(s) for s in small]
    args = ([p_main.reshape(bsz, tlen, 3 * width), p_lora.reshape(bsz, tlen, wl)]
            + ([shift_in] if has_state else []) + small)
    aliases = _threaded_state(args, in_specs, shift_prev, 7)
    seq = lambda i: jax.ShapeDtypeStruct((bsz, tlen, width), f32 if i == 1 else seq_dtype)
    outs = pl.pallas_call(
        functools.partial(_rw_prep_kernel, nb=nb, tb=tb, width=width, lora=lora, has_state=has_state,
                          has_prev=shift_prev is not None),
        grid=(bsz // nb, tlen // tb),
        in_specs=in_specs,
        out_specs=[blk(width)] * 7 + [last_spec],
        out_shape=[seq(i) for i in range(7)] + [jax.ShapeDtypeStruct((depth, bsz, 1, tot), f32)],
        input_output_aliases=aliases,
        compiler_params=_params("parallel", "arbitrary"),
        name="rwkv7_prep",
    )(*args)
    return outs[:7], outs[7]


def _rw_state_io(s0_ref, s_ref, st_scr, combos, has_state, ti, last):
    n = RW_N

    @pl.when(ti == 0)
    def _():
        for c, (b, p) in enumerate(combos):
            if has_state:
                st_scr[c] = jnp.concatenate([s0_ref[b, RW_PAIR * p + h] for h in range(RW_PAIR)], axis=-1)
            else:
                st_scr[c] = jnp.zeros((n, RW_PAIR * n), f32)

    def finish():
        @pl.when(ti == last)
        def _():
            for c, (b, p) in enumerate(combos):
                s = st_scr[c]
                for h in range(RW_PAIR):
                    s_ref[b, RW_PAIR * p + h] = s[:, h * n:(h + 1) * n]

    return finish


def _rw_scan_kernel(*refs, nb, tb, pairs, has_state, has_prev):
    r_ref, w_ref, k_ref, v_ref, kn_ref, b_ref = refs[:6]
    s0_ref = refs[6] if has_state else None
    o_ref, s_ref, st_scr, lhs_a, lhs_v, lhs_o = refs[6 + has_state + has_prev:]
    ti = pl.program_id(1)
    n = RW_N
    lanes = RW_PAIR * n
    sub = V7X_SUBLANES
    combos = [(b, p) for b in range(nb) for p in range(pairs)]
    finish = _rw_state_io(s0_ref, s_ref, st_scr, combos, has_state, ti, pl.num_programs(1) - 1)

    ones_bd = _group_ones(lanes, n)
    diag = (lax.broadcasted_iota(jnp.int32, (n, lanes), 0)
            == lax.broadcasted_iota(jnp.int32, (n, lanes), 1) % n)

    def token_group(grp, carry):
        base = pl.multiple_of(grp * sub, sub)

        def row(ref, b, p, j, decay=False):
            tile = ref[b, pl.ds(base, sub), pl.ds(p * lanes, lanes)]
            tile = jnp.exp(tile) if decay else tile
            return jnp.broadcast_to(tile[j:j + 1], (n, lanes))

        o_rows = [[] for _ in combos]
        for j in range(sub):
            for c, (b, p) in enumerate(combos):
                lhs_a[c * n:(c + 1) * n, :] = (st_scr[c] * row(kn_ref, b, p, j)).astype(bf16)
                lhs_v[c * n:(c + 1) * n, :] = jnp.where(diag, row(v_ref, b, p, j), 0.0).astype(bf16)
            sa_all = _dot(lhs_a[...], ones_bd)
            vb_all = _dot(lhs_v[...], ones_bd)
            for c, (b, p) in enumerate(combos):
                s_new = (st_scr[c] * row(w_ref, b, p, j, decay=True)
                         + sa_all[c * n:(c + 1) * n] * row(b_ref, b, p, j)
                         + vb_all[c * n:(c + 1) * n] * row(k_ref, b, p, j))
                st_scr[c] = s_new
                lhs_o[c * n:(c + 1) * n, :] = (s_new * row(r_ref, b, p, j)).astype(bf16)
            o_all = _dot(lhs_o[...], ones_bd)
            for c, (b, p) in enumerate(combos):
                o_rows[c].append(jnp.sum(jnp.where(diag, o_all[c * n:(c + 1) * n], 0.0),
                                         axis=0, keepdims=True))
        for c, (b, p) in enumerate(combos):
            o_ref[b, pl.ds(base, sub), pl.ds(p * lanes, lanes)] = jnp.concatenate(o_rows[c], axis=0)
        return carry

    lax.fori_loop(0, tb // sub, token_group, 0)
    finish()


def _rw_chunk_kernel(*refs, nb, tb, pairs, has_state, has_prev):
    r_ref, w_ref, k_ref, v_ref, kn_ref, b_ref = refs[:6]
    s0_ref = refs[6] if has_state else None
    o_ref, s_ref, st_scr, coef_scr, x_scr = refs[6 + has_state + has_prev:]
    ti = pl.program_id(1)
    n = RW_N
    lanes = RW_PAIR * n
    ln = RW_CHUNK
    combos = [(b, p) for b in range(nb) for p in range(pairs)]
    finish = _rw_state_io(s0_ref, s_ref, st_scr, combos, has_state, ti, pl.num_programs(1) - 1)

    ones_bd = _group_ones(lanes, n)
    t_idx = lax.broadcasted_iota(jnp.int32, (ln, lanes), 0)
    lane_idx = lax.broadcasted_iota(jnp.int32, (ln, lanes), 1)
    row_i = lax.broadcasted_iota(jnp.int32, (4 * ln, 4 * ln), 0)
    col_i = lax.broadcasted_iota(jnp.int32, (4 * ln, 4 * ln), 1)
    same_head = (row_i % (2 * ln)) // ln == (col_i % (2 * ln)) // ln
    strict = jnp.where(row_i < 2 * ln, 1, 0)
    lower2 = jnp.where(same_head & (col_i % ln + strict <= row_i % ln), 1.0, 0.0)
    nt_dims = (((1,), (1,)), ((), ()))

    def by_head(x):
        head0 = lane_idx < n
        return jnp.concatenate([jnp.where(head0, x, 0.0), jnp.where(head0, 0.0, x)], axis=0)

    def stack_heads(x):
        return jnp.concatenate([x[:, :n], x[:, n:]], axis=0)

    def pack_heads(xs):
        return jnp.concatenate([xs[:ln], xs[ln:]], axis=-1)

    half = ln // 2
    solve_cols = list(range(half - 1)) + list(range(half, ln - 1))

    def prepare(ck, c, out):
        b, p = combos[c]
        base = ck * ln
        tile = lambda ref: ref[b, base:base + ln, p * lanes:(p + 1) * lanes].astype(f32)
        lw = tile(w_ref)
        cum = lw
        step = 1
        while step < ln:
            cum = cum + jnp.where(t_idx >= step, pltpu.roll(cum, step, 0), 0.0)
            step *= 2
        c_last = jnp.broadcast_to(cum[ln - 1:ln], (ln, lanes))
        inv_c = jnp.exp(-cum)
        to_end = jnp.exp(c_last - cum)
        kn, bb, kk, vv = tile(kn_ref), tile(b_ref), tile(k_ref), tile(v_ref)
        n_dec = kn * jnp.exp(cum - lw)
        b_dec = bb * inv_c
        k_dec = kk * inv_c
        r_dec = tile(r_ref) * jnp.exp(cum)
        lhs_nr = jnp.concatenate([by_head(n_dec), by_head(r_dec)], axis=0).astype(bf16)
        rhs_bk = jnp.concatenate([by_head(b_dec), by_head(k_dec)], axis=0).astype(bf16)
        sc = (lower2 * lax.dot_general(lhs_nr, rhs_bk, nt_dims, preferred_element_type=f32)).astype(bf16)
        blocks = []
        for s in solve_cols:
            lo = 0 if s < half else half
            blocks.append(jnp.where(t_idx[lo:lo + half] > s,
                                    n_dec[lo:lo + half] * jnp.broadcast_to(b_dec[s:s + 1], (half, lanes)), 0.0))
        coef_scr[ck % 2, c] = _dot(jnp.concatenate(blocks, axis=0).astype(bf16), ones_bd)
        sc_cross = jnp.concatenate([sc[half:ln], sc[ln + half:2 * ln]], axis=0)
        vs = stack_heads(vv)
        upd_v = _dot(vs.T.astype(bf16), by_head(kk * to_end).astype(bf16))
        out.append((lhs_nr, sc[:2 * ln], sc[2 * ln:], vs.astype(bf16), by_head(bb * to_end).astype(bf16),
                    jnp.exp(cum[ln - 1:ln]), sc_cross, upd_v))

    ncombo = len(combos)
    pad = jnp.zeros((2 * ln, n), bf16)

    def advance(ck, pre, fill):
        base = ck * ln
        slot = ck % 2
        per_step = -(-len(fill) // (2 * (half - 1)))
        gs = [lax.dot_general(pre[c][0], st_scr[c].astype(bf16), nt_dims, preferred_element_type=f32)
              for c in range(ncombo)]
        for c in range(ncombo):
            x_scr[c] = pack_heads(gs[c][:2 * ln] + _dot(pre[c][1], jnp.concatenate([pad, pre[c][3]], axis=0)))

        def solve_half(lo, first_block):
            xh = [x_scr[c, lo:lo + half, :] for c in range(ncombo)]
            for si in range(half - 1):
                for c in range(ncombo):
                    coef = coef_scr[slot, c, (first_block + si) * half:(first_block + si + 1) * half, :]
                    xh[c] = xh[c] + coef * jnp.broadcast_to(xh[c][si:si + 1], (half, lanes))
                for _ in range(min(per_step, len(fill))):
                    fill.pop(0)()
            for c in range(ncombo):
                x_scr[c, lo:lo + half, :] = xh[c]

        solve_half(0, 0)
        for c in range(ncombo):
            x_top = jnp.concatenate([x_scr[c, :half, :], jnp.zeros((half, lanes), f32)], axis=0)
            cross = _dot(pre[c][6], jnp.concatenate([stack_heads(x_top).astype(bf16), pad], axis=0))
            x_scr[c, half:, :] = x_scr[c, half:, :] + jnp.concatenate([cross[:half], cross[half:]], axis=-1)
        solve_half(half, half - 1)
        while fill:
            fill.pop(0)()
        xss = [stack_heads(x_scr[c]) for c in range(ncombo)]
        xs_ts = [xs.T.astype(bf16) for xs in xss]
        for c, (b, p) in enumerate(combos):
            _, _, sc_out, vs_bf, rhs_b_end, d_last, _, upd_v = pre[c]
            o = gs[c][2 * ln:] + _dot(sc_out, jnp.concatenate([xss[c].astype(bf16), vs_bf], axis=0))
            o_ref[b, base:base + ln, p * lanes:(p + 1) * lanes] = pack_heads(o)
            st_scr[c] = st_scr[c] * d_last + _dot(xs_ts[c], rhs_b_end) + upd_v

    nck = tb // ln
    pre = []
    for c in range(ncombo):
        prepare(0, c, pre)
    for ck in range(nck):
        nxt = []
        fill = [functools.partial(prepare, ck + 1, c, nxt) for c in range(ncombo)] if ck + 1 < nck else []
        advance(ck, pre, fill)
        pre = nxt
    finish()


def _rw_lanes_kernel(*refs, tlen, has_prev):
    r_ref, w_ref, k_ref, v_ref, kn_ref, b_ref, s0_ref = refs[:7]
    o_ref, s_ref, vec_scr, o_scr = refs[7 + has_prev:]
    n = RW_N
    bsz = V7X_LANES
    sub = V7X_SUBLANES
    s_ref[...] = s0_ref[...]
    for t in range(tlen):
        for slot, ref in enumerate((kn_ref, w_ref, b_ref, k_ref, r_ref, v_ref)):
            x = ref[pl.ds(t, bsz, stride=tlen), :].T
            vec_scr[slot] = jnp.exp(x) if slot == 1 else x

        def value_rows(ig, carry):
            i0 = pl.multiple_of(ig * sub, sub)
            for h in range(RW_PAIR):
                kn_h, w_h, b_h, k_h, r_h = (vec_scr[slot, h * n:(h + 1) * n, :] for slot in range(5))
                v_tile = vec_scr[5, pl.ds(h * n + i0, sub), :]
                o_rows = []
                for ii in range(sub):
                    s_old = s_ref[h, i0 + ii]
                    sa = jnp.sum(s_old * kn_h, axis=0, keepdims=True)
                    s_new = s_old * w_h + sa * b_h + v_tile[ii:ii + 1] * k_h
                    s_ref[h, i0 + ii] = s_new
                    o_rows.append(jnp.sum(s_new * r_h, axis=0, keepdims=True))
                o_scr[pl.ds(h * n + i0, sub), :] = jnp.concatenate(o_rows, axis=0)
            return carry

        lax.fori_loop(0, n // sub, value_rows, 0)
        o_ref[pl.ds(t, bsz, stride=tlen), :] = o_scr[...].T


def _rw_lanes_scan(seqs, s_in, s_prev, layer, depth, bsz, tlen):
    r, w, k, v, kn, bb = seqs
    width = r.shape[-1]
    heads = width // RW_N
    lanes = RW_PAIR * RW_N
    rows = bsz * tlen
    blk = pl.BlockSpec((rows, lanes), lambda p: (0, p))
    st_spec = pl.BlockSpec((None, RW_PAIR, RW_N, RW_N, bsz), lambda p: (layer, p, 0, 0, 0))
    in_specs = [blk] * 6 + [st_spec]
    args = [a.reshape(rows, width) for a in (r, w, k, v, kn, bb)] + [s_in]
    aliases = _threaded_state(args, in_specs, s_prev, 1)
    o, s_new = pl.pallas_call(
        functools.partial(_rw_lanes_kernel, tlen=tlen, has_prev=s_prev is not None),
        grid=(heads // RW_PAIR,),
        in_specs=in_specs,
        out_specs=[blk, st_spec],
        out_shape=[jax.ShapeDtypeStruct((rows, width), f32),
                   jax.ShapeDtypeStruct((depth, heads, RW_N, RW_N, bsz), f32)],
        scratch_shapes=[pltpu.VMEM((6, lanes, bsz), f32), pltpu.VMEM((lanes, bsz), f32)],
        input_output_aliases=aliases,
        compiler_params=_params("parallel"),
        name="rwkv7_lanes_scan",
    )(*args)
    return o.reshape(bsz, tlen, width), s_new


def _rw_mode(bsz, tlen):
    if _tile(tlen, RW_BLOCK_TOKENS) % RW_CHUNK == 0:
        return "chunk"
    return "lanes" if bsz == V7X_LANES else "token"


def _rw_scan(seqs, s_in, s_prev, layer, depth, bsz, tlen):
    if _rw_mode(bsz, tlen) == "lanes" and s_in is not None:
        return _rw_lanes_scan(seqs, s_in, s_prev, layer, depth, bsz, tlen)
    r, w, k, v, kn, bb = seqs
    width = r.shape[-1]
    heads = width // RW_N
    pairs = heads // RW_PAIR
    nb = _tile(bsz, 4, align=1)
    tb = _tile(tlen, RW_BLOCK_TOKENS)
    chunked = tb % RW_CHUNK == 0
    has_state = s_in is not None
    blk = pl.BlockSpec((nb, tb, width), lambda bi, ti: (bi, ti, 0))
    st_spec = pl.BlockSpec((None, nb, heads, RW_N, RW_N), lambda bi, ti: (layer, bi, 0, 0, 0))
    m = nb * pairs * RW_N
    lanes = RW_PAIR * RW_N
    in_specs = [blk] * 6 + ([st_spec] if has_state else [])
    args = [r, w, k, v, kn, bb] + ([s_in] if has_state else [])
    aliases = _threaded_state(args, in_specs, s_prev, 1)
    scratch = [pltpu.VMEM((nb * pairs, RW_N, lanes), f32)]
    if chunked:
        scratch += [pltpu.VMEM((2, nb * pairs, (RW_CHUNK - 2) * (RW_CHUNK // 2), lanes), f32),
                    pltpu.VMEM((nb * pairs, RW_CHUNK, lanes), f32)]
    else:
        scratch += [pltpu.VMEM((m, lanes), bf16)] * 3
    o, s_new = pl.pallas_call(
        functools.partial(_rw_chunk_kernel if chunked else _rw_scan_kernel, nb=nb, tb=tb, pairs=pairs,
                          has_state=has_state, has_prev=s_prev is not None),
        grid=(bsz // nb, tlen // tb),
        in_specs=in_specs,
        out_specs=[blk, st_spec],
        out_shape=[jax.ShapeDtypeStruct((bsz, tlen, width), f32),
                   jax.ShapeDtypeStruct((depth, bsz, heads, RW_N, RW_N), f32)],
        scratch_shapes=scratch,
        input_output_aliases=aliases,
        compiler_params=_params("parallel", "arbitrary"),
        name="rwkv7_chunk_scan" if chunked else "rwkv7_scan",
    )(*args)
    return o, s_new


def _rw_post_kernel(o_ref, r_ref, k_ref, v_ref, g_ref, lw_ref, lbias_ref, rk_ref, y_ref):
    ones_bd = _group_ones(V7X_LANES, RW_N)
    o = o_ref[...]
    mu = _group_sum(o, ones_bd) * (1.0 / RW_N)
    d = o - mu
    var = _group_sum(d * d, ones_bd) * (1.0 / RW_N)
    on = d * lax.rsqrt(var + RW_GN_EPS) * lw_ref[...] + lbias_ref[...]
    rk = r_ref[...].astype(f32) * k_ref[...].astype(f32) * rk_ref[...]
    bonus = _group_sum(rk, ones_bd) * v_ref[...].astype(f32)
    y_ref[...] = ((on + bonus) * g_ref[...].astype(f32)).astype(y_ref.dtype)


def _rw_post(o, r, k, v, g, ln_w, ln_b, r_k):
    n, width = o.shape
    tm = _tile(n, 512)
    blk = pl.BlockSpec((tm, width), lambda i: (i, 0))
    vec = pl.BlockSpec((1, width), lambda i: (0, 0))
    return pl.pallas_call(
        _rw_post_kernel,
        grid=(n // tm,),
        in_specs=[blk] * 5 + [vec] * 3,
        out_specs=blk,
        out_shape=jax.ShapeDtypeStruct((n, width), bf16),
        compiler_params=_params("parallel"),
        name="rwkv7_post",
    )(o, r, k, v, g, ln_w, ln_b, r_k)


def _conv_kernel(*refs, nb, tb, has_state, has_prev):
    sb_ref, sc_ref, sh_ref = refs[:3]
    st_ref = refs[3] if has_state else None
    cw_ref = refs[3 + has_state]
    y_ref, new_ref = refs[4 + has_state + has_prev:]
    ti = pl.program_id(2)
    rows = nb * tb
    w = sb_ref.shape[-1]
    hist = SC_KSIZE - 1

    @pl.when(ti == 0)
    def _():
        new_ref[...] = st_ref[...] if has_state else jnp.zeros((nb, hist, w), f32)

    u = (sc_ref[...].astype(f32) * sh_ref[...].astype(f32)).reshape(rows, w)
    carry = new_ref[...]
    conv = cw_ref[hist:hist + 1, :] * u
    for back in range(1, hist + 1):
        first = [_bcast_rows(carry[:, hist - back + j:hist - back + j + 1, :], nb, tb) for j in range(back)]
        conv = conv + cw_ref[hist - back:hist - back + 1, :] * _shift_rows(u, first, tb, back)
    y_ref[...] = (sb_ref[...].astype(f32).reshape(rows, w) * conv).reshape(nb, tb, w).astype(y_ref.dtype)
    new_ref[...] = u.reshape(nb, tb, w)[:, tb - hist:tb, :]


def _conv(p_sc, st_in, st_prev, layer, depth, conv_w, bsz, tlen):
    width = conv_w.shape[1]
    cw = _tile(width, 1024, align=V7X_LANES)
    ncol = width // cw
    nb, tb = _seq_block(bsz, tlen)
    assert tb >= SC_KSIZE - 1
    has_state = st_in is not None
    p3 = p_sc.reshape(bsz, tlen, 3 * width)
    seg = lambda s: pl.BlockSpec((nb, tb, cw), lambda bi, cj, ti, s=s: (bi, ti, s * ncol + cj))
    st_spec = pl.BlockSpec((None, nb, SC_KSIZE - 1, cw), lambda bi, cj, ti: (layer, bi, 0, cj))
    in_specs = ([seg(0), seg(1), seg(2)] + ([st_spec] if has_state else [])
                + [pl.BlockSpec((SC_KSIZE, cw), lambda bi, cj, ti: (0, cj))])
    args = [p3, p3, p3] + ([st_in] if has_state else []) + [conv_w]
    aliases = _threaded_state(args, in_specs, st_prev, 1)
    y, new = pl.pallas_call(
        functools.partial(_conv_kernel, nb=nb, tb=tb, has_state=has_state, has_prev=st_prev is not None),
        grid=(bsz // nb, ncol, tlen // tb),
        in_specs=in_specs,
        out_specs=[pl.BlockSpec((nb, tb, cw), lambda bi, cj, ti: (bi, ti, cj)), st_spec],
        out_shape=[jax.ShapeDtypeStruct((bsz, tlen, width), bf16),
                   jax.ShapeDtypeStruct((depth, bsz, SC_KSIZE - 1, width), f32)],
        input_output_aliases=aliases,
        compiler_params=_params("parallel", "parallel", "arbitrary"),
        name="short_conv",
    )(*args)
    return y.reshape(bsz * tlen, width), new


def _merge_kernel(a_ref, b_ref, c_ref, ga_ref, gb_ref, gc_ref, pa_ref, pb_ref, pc_ref, m_ref, w_scr):
    @pl.when(pl.program_id(1) == 0)
    def _():
        for s, w_ref in enumerate((pa_ref, pb_ref, pc_ref)):
            w_scr[s] = w_ref[...].astype(bf16)

    m = (_gate_sigmoid(ga_ref[...].astype(f32)) * _dot(a_ref[...], w_scr[0])
         + _gate_sigmoid(gb_ref[...].astype(f32)) * _dot(b_ref[...], w_scr[1])
         + _gate_sigmoid(gc_ref[...].astype(f32)) * _dot(c_ref[...], w_scr[2]))
    m_ref[...] = m.astype(m_ref.dtype)


def _merge(ya, yb, yc, p_gate, w_pa, w_pb, w_pc, layer):
    n = ya.shape[0]
    width, d = w_pa.shape[1:]
    assert w_pb.shape[1] == width and w_pc.shape[1] == width
    tm = _tile(n, 1024)
    tn = _tile(d, 512, align=V7X_LANES)
    nj = d // tn
    act = pl.BlockSpec((tm, width), lambda j, i: (i, 0))
    gate = lambda s: pl.BlockSpec((tm, tn), lambda j, i, s=s: (i, s * nj + j))
    wt = pl.BlockSpec((None, width, tn), lambda j, i: (layer, 0, j))
    return pl.pallas_call(
        _merge_kernel,
        grid=(nj, n // tm),
        in_specs=[act, act, act, gate(0), gate(1), gate(2), wt, wt, wt],
        out_specs=pl.BlockSpec((tm, tn), lambda j, i: (i, j)),
        out_shape=jax.ShapeDtypeStruct((n, d), bf16),
        scratch_shapes=[pltpu.VMEM((3, width, tn), bf16)],
        compiler_params=_params("arbitrary", "arbitrary"),
        name="gated_merge",
    )(ya, yb, yc, p_gate, p_gate, p_gate, w_pa, w_pb, w_pc)


def _oproj_kernel(m_ref, wo_ref, x_ref, g1_ref, g2_ref, x1_ref, h2_ref):
    mix = _dot(m_ref[...], wo_ref[...])
    x1 = x_ref[...] + _rms(mix, g1_ref[...])
    x1_ref[...] = x1
    h2_ref[...] = _rms(x1, g2_ref[...]).astype(h2_ref.dtype)


def _oproj(m, w_o, layer, x, g_post_mix, g_pre_mlp):
    n, d = x.shape
    tm = _tile(n, 512)
    blk = pl.BlockSpec((tm, d), lambda i: (i, 0))
    vec = pl.BlockSpec((1, d), lambda i: (0, 0))
    return pl.pallas_call(
        _oproj_kernel,
        grid=(n // tm,),
        in_specs=[blk, pl.BlockSpec((None, d, d), lambda i: (layer, 0, 0)), blk, vec, vec],
        out_specs=[blk, blk],
        out_shape=[jax.ShapeDtypeStruct((n, d), f32), jax.ShapeDtypeStruct((n, d), bf16)],
        compiler_params=_params("parallel"),
        name="out_proj_norms",
    )(m, w_o, x, g_post_mix, g_pre_mlp)


def _mlp_kernel(h_ref, w1_ref, w2_ref, ff_ref):
    kf = pl.program_id(1)

    @pl.when(kf == 0)
    def _():
        ff_ref[...] = jnp.zeros_like(ff_ref)

    t = jnp.maximum(_dot(h_ref[...], w1_ref[...].astype(bf16)), 0.0)
    ff_ref[...] += _dot((t * t).astype(bf16), w2_ref[...].astype(bf16))


def _mlp(h2, w1, w2, layer):
    n, d = h2.shape
    dff = w1.shape[2]
    tm = _tile(n, 2048)
    fc = _tile(dff, 512, align=V7X_LANES)
    once = pl.Buffered(1)
    return pl.pallas_call(
        _mlp_kernel,
        grid=(n // tm, dff // fc),
        in_specs=[pl.BlockSpec((tm, d), lambda i, kf: (i, 0), pipeline_mode=once),
                  pl.BlockSpec((None, d, fc), lambda i, kf: (layer, 0, kf)),
                  pl.BlockSpec((None, fc, d), lambda i, kf: (layer, kf, 0))],
        out_specs=pl.BlockSpec((tm, d), lambda i, kf: (i, 0), pipeline_mode=once),
        out_shape=jax.ShapeDtypeStruct((n, d), f32),
        compiler_params=_params("parallel", "arbitrary", vmem=V7X_VMEM_BIG_TILE_BYTES),
        name="relu2_mlp",
    )(h2, w1, w2)


def _residual_kernel(*refs, has_next):
    if has_next:
        x1_ref, ff_ref, g_ref, gn_ref, y_ref, h_ref = refs
    else:
        x1_ref, ff_ref, g_ref, y_ref = refs
    y = x1_ref[...] + _rms(ff_ref[...], g_ref[...])
    y_ref[...] = y
    if has_next:
        h_ref[...] = _rms(y, gn_ref[...]).astype(h_ref.dtype)


def _residual(x1, ff, g_post_mlp, g_next):
    n, d = x1.shape
    tm = _tile(n, 512)
    blk = pl.BlockSpec((tm, d), lambda i: (i, 0))
    vec = pl.BlockSpec((1, d), lambda i: (0, 0))
    has_next = g_next is not None
    outs = pl.pallas_call(
        functools.partial(_residual_kernel, has_next=has_next),
        grid=(n // tm,),
        in_specs=[blk, blk, vec] + ([vec] if has_next else []),
        out_specs=[blk] + ([blk] if has_next else []),
        out_shape=[jax.ShapeDtypeStruct((n, d), f32)] + ([jax.ShapeDtypeStruct((n, d), bf16)] if has_next else []),
        compiler_params=_params("parallel"),
        name="mlp_residual_norm",
    )(*([x1, ff, g_post_mlp] + ([g_next] if has_next else [])))
    return (outs[0], outs[1]) if has_next else (outs[0], None)


def _layer(x, h, bsz, tlen, states_in, states_prev, layer, depth, wts, lw, g_next):
    hg_in, rw_in, shift_in, sc_in = states_in
    hg_prev, rw_prev, shift_prev, sc_prev = states_prev
    w_in, offs = wts["w_in"], wts["offs"]
    p_hg = _proj(h, w_in, layer, offs[0], offs[1] - offs[0], "in_proj_hgrn")
    p_rkv = _proj(h, w_in, layer, offs[1], offs[2] - offs[1], "in_proj_rwkv")
    p_lora = _proj(h, w_in, layer, offs[2], offs[3] - offs[2], "in_proj_rwkv_lora")
    p_sc = _proj(h, w_in, layer, offs[3], offs[4] - offs[3], "in_proj_conv", bf16)
    p_gate = _proj(h, w_in, layer, offs[4], offs[5] - offs[4], "in_proj_gate", bf16)

    ya, hg_new = _hgrn(p_hg, lw["lb"], lw["hg_norm"], hg_in, hg_prev, layer, depth, bsz, tlen)

    seq_dtype = bf16 if _rw_mode(bsz, tlen) == "chunk" else f32
    seqs, shift_new = _rw_prep(p_rkv, p_lora, shift_in, shift_prev, layer, depth, lw["rw_prep"], bsz, tlen,
                               seq_dtype)
    r, w, k, v, kn, bb, g = seqs
    o_rw, rw_new = _rw_scan((r, w, k, v, kn, bb), rw_in, rw_prev, layer, depth, bsz, tlen)
    flat = lambda t: t.reshape(bsz * tlen, t.shape[-1])
    yb = _rw_post(flat(o_rw), flat(r), flat(k), flat(v), flat(g), lw["rw_ln_w"], lw["rw_ln_b"], lw["rw_r_k"])

    yc, sc_new = _conv(p_sc, sc_in, sc_prev, layer, depth, lw["sc_conv_w"], bsz, tlen)

    m = _merge(ya, yb, yc, p_gate, wts["w_pa"], wts["w_pb"], wts["w_pc"], layer)
    x1, h2 = _oproj(m, wts["w_o"], layer, x, lw["g_post_mix"], lw["g_pre_mlp"])
    ff = _mlp(h2, wts["w_ff1"], wts["w_ff2"], layer)
    y, h_next = _residual(x1, ff, lw["g_post_mlp"], g_next)
    return y, h_next, (hg_new, rw_new, shift_new, sc_new)


def kernel(x_prompt, x_sample, state_hgrn, state_rwkv, state_rwkv_shift, state_conv, norm_pre_mix, norm_post_mix, norm_pre_mlp, norm_post_mlp, w_in, hg_lb_logits, hg_norm, w_pa, rw_mu, rw_w0, rw_w2, rw_a0, rw_a2, rw_g2, rw_k_k, rw_k_a, rw_r_k, rw_ln_w, rw_ln_b, w_pb, sc_conv_w, w_pc, w_o, w_ff1, w_ff2):
    depth = w_in.shape[0]
    d_model = x_prompt.shape[-1]
    hg_width = hg_norm.shape[1]
    rw_width = rw_w0.shape[1]
    rw_shift_width = rw_mu.shape[1]
    sc_width = sc_conv_w.shape[2]
    off_rw = 4 * hg_width
    off_lora = off_rw + 3 * rw_width
    off_sc = off_rw + rw_shift_width
    off_gate = off_sc + 3 * sc_width
    assert w_in.shape[2] == off_gate + 3 * d_model

    lb_all = _lower_bounds(hg_lb_logits.astype(f32))
    row = lambda a, l: a[l].reshape(1, -1).astype(f32)
    wts = {"w_in": w_in.astype(f32), "offs": (0, off_rw, off_lora, off_sc, off_gate, w_in.shape[2]),
           "w_pa": w_pa.astype(f32), "w_pb": w_pb.astype(f32), "w_pc": w_pc.astype(f32),
           "w_o": w_o.astype(bf16), "w_ff1": w_ff1.astype(f32), "w_ff2": w_ff2.astype(f32)}

    bp, tp, _ = x_prompt.shape
    bs, ts, _ = x_sample.shape
    yp = x_prompt.reshape(bp * tp, d_model).astype(f32)
    ys = x_sample.reshape(bs * ts, d_model).astype(f32)
    hp = _rmsnorm_cast(yp, row(norm_pre_mix, 0))
    hs = _rmsnorm_cast(ys, row(norm_pre_mix, 0))
    rw_lanes = _rw_mode(bs, ts) == "lanes"
    rw_state = state_rwkv.astype(f32)
    sample_in = (state_hgrn.astype(f32), jnp.transpose(rw_state, (0, 2, 3, 4, 1)) if rw_lanes else rw_state,
                 state_rwkv_shift.astype(f32).reshape(depth, bs, 1, rw_shift_width), state_conv.astype(f32))
    new_p = new_s = (None, None, None, None)
    for l in range(depth):
        lw = {
            "g_post_mix": row(norm_post_mix, l), "g_pre_mlp": row(norm_pre_mlp, l),
            "g_post_mlp": row(norm_post_mlp, l),
            "lb": lb_all[l:l + 1], "hg_norm": row(hg_norm, l),
            "rw_prep": (row(rw_mu, l), row(rw_w0, l), rw_w2[l].astype(bf16), row(rw_a0, l), rw_a2[l].astype(bf16),
                        rw_g2[l].astype(bf16), row(rw_k_k, l), row(rw_k_a, l)),
            "rw_ln_w": row(rw_ln_w, l), "rw_ln_b": row(rw_ln_b, l), "rw_r_k": row(rw_r_k, l),
            "sc_conv_w": sc_conv_w[l].astype(f32),
        }
        g_next = row(norm_pre_mix, l + 1) if l + 1 < depth else None
        yp, hp, new_p = _layer(yp, hp, bp, tp, (None, None, None, None), new_p, l, depth, wts, lw, g_next)
        ys, hs, new_s = _layer(ys, hs, bs, ts, sample_in, new_s, l, depth, wts, lw, g_next)
    shift = lambda s, b: s.reshape(depth, b, rw_shift_width)
    return (yp.reshape(bp, tp, d_model), ys.reshape(bs, ts, d_model),
            new_p[0], new_p[1], shift(new_p[2], bp), new_p[3],
            new_s[0], jnp.transpose(new_s[1], (0, 4, 1, 2, 3)) if rw_lanes else new_s[1],
            shift(new_s[2], bs), new_s[3])
```

```python
import functools

import jax
import jax.numpy as jnp
from jax import lax
from jax.experimental import pallas as pl
from jax.experimental.pallas import tpu as pltpu

HG_DK = 128
RW_N = 64
HG_F_MIN = 1e-30
LOG2_E = 1.4426950408889634
RW_GN_EPS = 64e-5
NORM_EPS = 1e-6
SC_KSIZE = 3

V7X_LANES = 128
V7X_SUBLANES = 8
V7X_BF16_ROWS = 16
V7X_VMEM_LIMIT_BYTES = 48 * 1024 * 1024
V7X_VMEM_BIG_TILE_BYTES = 56 * 1024 * 1024
HG_CHUNK = 16
HG_HEADS_PER_STEP = 8
HG_GROUP = 4
RW_PAIR = 2
RW_CHUNK = 16
RW_BLOCK_TOKENS = 128

f32 = jnp.float32
bf16 = jnp.bfloat16


def _tile(n, pref, align=V7X_SUBLANES):
    if n <= pref:
        return n
    for d in range(pref, 0, -1):
        if n % d == 0 and d % align == 0:
            return d
    return n


def _seq_block(bsz, tlen):
    if tlen >= 256:
        return 1, _tile(tlen, 256)
    return _tile(bsz, max(1, 128 // tlen), align=1), tlen


def _params(*sem, vmem=V7X_VMEM_LIMIT_BYTES):
    return pltpu.CompilerParams(dimension_semantics=sem, vmem_limit_bytes=vmem)


def _sigmoid(x):
    return 1.0 / (1.0 + jnp.exp(-x))


def _gate_sigmoid(x):
    return 0.5 * jnp.tanh(0.5 * x) + 0.5


def _dot(a, b):
    return jnp.dot(a, b, preferred_element_type=f32)


def _rms(x, g):
    return x * lax.rsqrt(jnp.mean(x * x, axis=-1, keepdims=True) + NORM_EPS) * g


def _group_ones(width, group):
    r = lax.broadcasted_iota(jnp.int32, (width, width), 0) // group
    c = lax.broadcasted_iota(jnp.int32, (width, width), 1) // group
    return jnp.where(r == c, 1.0, 0.0).astype(bf16)


def _group_sum(z, ones_bd):
    outs = []
    for c in range(z.shape[-1] // V7X_LANES):
        zc = z[:, c * V7X_LANES:(c + 1) * V7X_LANES]
        hi = zc.astype(bf16)
        lo = (zc - hi.astype(f32)).astype(bf16)
        outs.append(_dot(hi, ones_bd) + _dot(lo, ones_bd))
    return outs[0] if len(outs) == 1 else jnp.concatenate(outs, axis=-1)


def _threaded_state(args, in_specs, prev_out, out_index):
    if prev_out is None:
        return {}
    args.append(prev_out)
    in_specs.append(pl.BlockSpec(memory_space=pl.ANY))
    return {len(args) - 1: out_index}


def _lb_kernel(logit_ref, lb_ref):
    z = logit_ref[...]
    depth = z.shape[0]
    m = jnp.max(z, axis=0, keepdims=True)
    e = jnp.exp(z - m)
    p = e / jnp.sum(e, axis=0, keepdims=True)
    acc = jnp.zeros_like(p[0:1])
    for l in range(depth):
        acc = acc + p[l:l + 1]
        lb_ref[l:l + 1, :] = jnp.clip(acc - p[0:1], 0.0, 1.0)


def _lower_bounds(logits):
    return pl.pallas_call(
        _lb_kernel, out_shape=jax.ShapeDtypeStruct(logits.shape, f32), name="hg_lower_bounds",
    )(logits)


def _rmsnorm_kernel(x_ref, g_ref, o_ref):
    o_ref[...] = _rms(x_ref[...], g_ref[...]).astype(o_ref.dtype)


def _rmsnorm_cast(x, g):
    n, d = x.shape
    tm = _tile(n, 512)
    return pl.pallas_call(
        _rmsnorm_kernel,
        grid=(n // tm,),
        in_specs=[pl.BlockSpec((tm, d), lambda i: (i, 0)), pl.BlockSpec((1, d), lambda i: (0, 0))],
        out_specs=pl.BlockSpec((tm, d), lambda i: (i, 0)),
        out_shape=jax.ShapeDtypeStruct((n, d), bf16),
        compiler_params=_params("parallel"),
        name="rmsnorm_cast",
    )(x, g)


def _proj_kernel(a_ref, w_ref, o_ref, wbf_scr):
    @pl.when(pl.program_id(1) == 0)
    def _():
        wbf_scr[...] = w_ref[0].astype(bf16)

    o_ref[...] = _dot(a_ref[...], wbf_scr[...]).astype(o_ref.dtype)


def _proj(h, w_stack, layer, off, width, name, out_dtype=f32):
    n, k = h.shape
    big = out_dtype == bf16
    tm = _tile(n, 2048 if big else 1024)
    tn = _tile(width, 1024, align=V7X_LANES)
    w_spec = pl.BlockSpec((pl.Element(1), pl.Element(k), pl.Element(tn)),
                          lambda j, i: (layer, 0, pl.multiple_of(off + j * tn, V7X_LANES)))
    return pl.pallas_call(
        _proj_kernel,
        grid=(width // tn, n // tm),
        in_specs=[pl.BlockSpec((tm, k), lambda j, i: (i, 0)), w_spec],
        out_specs=pl.BlockSpec((tm, tn), lambda j, i: (i, j)),
        out_shape=jax.ShapeDtypeStruct((n, width), out_dtype),
        scratch_shapes=[pltpu.VMEM((k, tn), bf16)],
        compiler_params=_params("arbitrary", "arbitrary",
                                vmem=V7X_VMEM_BIG_TILE_BYTES if big else V7X_VMEM_LIMIT_BYTES),
        name=name,
    )(h, w_stack)


def _hgrn_kernel(*refs, nb, tb, hb, chunk, has_state, has_prev):
    q_ref, f_ref, i_ref, og_ref, lb_ref, gn_ref, rsel_ref = refs[:7]
    s0_ref = refs[7] if has_state else None
    o_ref, s_ref, st_scr = refs[7 + has_state + has_prev:]
    ti = pl.program_id(2)
    rows = nb * tb
    dk = HG_DK
    nchunk = rows // chunk
    heads = range(hb)

    @pl.when(ti == 0)
    def _():
        for b in range(nb):
            for h in heads:
                st_scr[b, h] = s0_ref[b, h].T if has_state else jnp.zeros((dk, dk), f32)

    grp = HG_GROUP if (tb // chunk) % HG_GROUP == 0 else 1
    row_idx = lax.broadcasted_iota(jnp.int32, (rows, dk), 0)
    t_idx = row_idx % chunk
    pos_in_grp = (row_idx // chunk) % grp
    chunk_r = lax.broadcasted_iota(jnp.int32, (rows, rows), 0) // chunk
    chunk_c = lax.broadcasted_iota(jnp.int32, (rows, rows), 1) // chunk
    chunk_masks = [jnp.where((chunk_r - chunk_c == dist) & (chunk_r // grp == chunk_c // grp), 1.0, 0.0)
                   for dist in range(grp)]
    ngroup = nchunk // grp
    col_group = lax.broadcasted_iota(jnp.int32, (dk, rows), 1) // (grp * chunk)

    def chunk_row(a, s):
        a3 = a.reshape(nchunk, chunk, dk)
        return jnp.broadcast_to(a3[:, s:s + 1, :], (nchunk, chunk, dk)).reshape(rows, dk)

    def head_cols(ref, h):
        return ref[:, :, h * dk:(h + 1) * dk].reshape(rows, dk)

    sub = V7X_SUBLANES
    tpc = chunk // sub

    def tiles(a):
        return a.reshape(nchunk, tpc, sub, dk)

    span = grp * chunk

    def state_steps(h, q_dec, glog, o_intra, u_all):
        outs = []

        def step(b, g):
            gi = b * (tb // span) + g
            r0 = gi * span
            st = st_scr[b, h]
            inter = lax.dot_general(q_dec[r0:r0 + span], st.astype(bf16),
                                    (((1,), (1,)), ((), ())), preferred_element_type=f32)
            outs.append(inter + o_intra[r0:r0 + span])
            st_scr[b, h] = jnp.exp(glog[r0 + span - 1:r0 + span]) * st + u_all[gi * dk:(gi + 1) * dk]

        def finish():
            o = jnp.concatenate(outs, axis=0)
            o = o * lax.rsqrt(jnp.mean(o * o, axis=-1, keepdims=True) + NORM_EPS)
            hog = head_cols(og_ref, h)
            o = o * gn_ref[:, h * dk:(h + 1) * dk] * (hog * _gate_sigmoid(hog))
            o_ref[:, :, h * dk:(h + 1) * dk] = o.reshape(nb, tb, dk).astype(o_ref.dtype)

        return [functools.partial(step, b, g) for b in range(nb) for g in range(tb // span)] + [finish]

    fill = []
    for h in heads:
        per_col = -(-len(fill) // chunk)
        lb = lb_ref[:, h * dk:(h + 1) * dk]
        hq = head_cols(q_ref, h)
        qq = hq * _gate_sigmoid(hq)
        f = lb + (1.0 - lb) * _sigmoid(head_cols(f_ref, h))
        kk = 1.0 - f
        bcum = jnp.log(jnp.maximum(f, HG_F_MIN))
        step = 1
        while step < chunk:
            bcum = bcum + jnp.where(t_idx >= step, pltpu.roll(bcum, step, 0), 0.0)
            step *= 2
        b_last = chunk_row(bcum, chunk - 1)
        zs = []
        b2 = bcum * LOG2_E
        for s in range(chunk):
            t0 = s // sub

            def col_row(a):
                return jnp.broadcast_to(tiles(a)[:, t0:t0 + 1, s % sub:s % sub + 1, :], (nchunk, 1, sub, dk))

            b2_s, k_s = col_row(b2), col_row(kk)
            parts = [jnp.zeros((nchunk, t0, sub, dk), f32)] if t0 else []
            for tt in range(t0, tpc):
                diff = tiles(b2)[:, tt:tt + 1] - b2_s
                qk = tiles(qq)[:, tt:tt + 1] * k_s
                if tt == t0:
                    z = jnp.where(tiles(t_idx)[:, tt:tt + 1] >= s, qk * jnp.exp2(jnp.minimum(diff, 0.0)), 0.0)
                else:
                    z = qk * jnp.exp2(diff)
                parts.append(z)
            z = parts[0] if len(parts) == 1 else jnp.concatenate(parts, axis=1)
            zs.append(z.reshape(rows, dk).astype(bf16))
            for _ in range(min(per_col, len(fill))):
                fill.pop(0)()
        k_dec = (kk * jnp.exp(b_last - bcum)).astype(bf16)
        vv = head_cols(i_ref, h)
        a_rep = _dot(jnp.concatenate(zs, axis=-1), rsel_ref[...])
        scores = a_rep * chunk_masks[0]
        between = jnp.zeros_like(bcum)
        for dist in range(1, grp):
            q_far = (qq * jnp.exp(bcum + between)).astype(bf16)
            far = lax.dot_general(q_far, k_dec, (((1,), (1,)), ((), ())), preferred_element_type=f32)
            scores = scores + far * chunk_masks[dist]
            between = between + pltpu.roll(b_last, dist * chunk, 0)
        reach = jnp.zeros_like(bcum)
        tail = jnp.zeros_like(bcum)
        for dist in range(1, grp):
            reach = reach + jnp.where(pos_in_grp >= dist, pltpu.roll(b_last, dist * chunk, 0), 0.0)
            tail = tail + jnp.where(pos_in_grp + dist < grp, pltpu.roll(b_last, rows - dist * chunk, 0), 0.0)
        glog = bcum + reach
        q_dec = (qq * jnp.exp(glog)).astype(bf16)
        o_intra = _dot(scores.astype(bf16), vv.astype(bf16))
        k_grp = k_dec if grp == 1 else (kk * jnp.exp(b_last - bcum + tail)).astype(bf16)
        v_t = vv.T
        lhs = jnp.concatenate([jnp.where(col_group == g, v_t, 0.0).astype(bf16) for g in range(ngroup)], axis=0)
        u_all = _dot(lhs, k_grp)
        while fill:
            fill.pop(0)()
        fill = state_steps(h, q_dec, glog, o_intra, u_all)
    while fill:
        fill.pop(0)()

    @pl.when(ti == pl.num_programs(2) - 1)
    def _():
        for b in range(nb):
            for h in heads:
                s_ref[b, h] = st_scr[b, h].T


def _hgrn(p_hg, lb, gn, s_in, s_prev, layer, depth, bsz, tlen):
    width = p_hg.shape[1] // 4
    heads = width // HG_DK
    chunk = min(HG_CHUNK, tlen)
    nb, tb = _seq_block(bsz, tlen)
    assert tb % chunk == 0
    p3 = p_hg.reshape(bsz, tlen, 4 * width)
    hb = HG_HEADS_PER_STEP if heads % HG_HEADS_PER_STEP == 0 else 1
    hgroups = heads // hb
    wb = hb * HG_DK
    seg = lambda s: pl.BlockSpec((nb, tb, wb), lambda bi, h, ti, s=s: (bi, ti, s * hgroups + h))
    vec = pl.BlockSpec((1, wb), lambda bi, h, ti: (0, h))
    st_spec = pl.BlockSpec((None, nb, hb, HG_DK, HG_DK), lambda bi, h, ti: (layer, bi, h, 0, 0))
    has_state = s_in is not None
    rows = nb * tb
    rsel = (jnp.arange(chunk * HG_DK)[:, None] // HG_DK == jnp.arange(rows)[None, :] % chunk).astype(bf16)
    rsel_spec = pl.BlockSpec((chunk * HG_DK, rows), lambda bi, h, ti: (0, 0))
    in_specs = [seg(0), seg(1), seg(2), seg(3), vec, vec, rsel_spec] + ([st_spec] if has_state else [])
    args = [p3, p3, p3, p3, lb, gn, rsel] + ([s_in] if has_state else [])
    aliases = _threaded_state(args, in_specs, s_prev, 1)
    o, s_new = pl.pallas_call(
        functools.partial(_hgrn_kernel, nb=nb, tb=tb, hb=hb, chunk=chunk, has_state=has_state,
                          has_prev=s_prev is not None),
        grid=(bsz // nb, hgroups, tlen // tb),
        in_specs=in_specs,
        out_specs=[pl.BlockSpec((nb, tb, wb), lambda bi, h, ti: (bi, ti, h)), st_spec],
        out_shape=[jax.ShapeDtypeStruct((bsz, tlen, width), bf16),
                   jax.ShapeDtypeStruct((depth, bsz, heads, HG_DK, HG_DK), f32)],
        scratch_shapes=[pltpu.VMEM((nb, hb, HG_DK, HG_DK), f32)],
        input_output_aliases=aliases,
        compiler_params=_params("parallel", "parallel", "arbitrary", vmem=V7X_VMEM_BIG_TILE_BYTES),
        name="hgrn2_scan",
    )(*args)
    return o.reshape(bsz * tlen, width), s_new


def _shift_rows(x, first, tb, by):
    rows = x.shape[0]
    t_idx = lax.broadcasted_iota(jnp.int32, (rows, 1), 0) % tb
    out = pltpu.roll(x, by, 0) if tb > by else x
    for j in range(by):
        out = jnp.where(t_idx == j, first[j], out)
    return out


def _bcast_rows(v3, nb, tb):
    w = v3.shape[-1]
    return jnp.broadcast_to(v3, (nb, tb, w)).reshape(nb * tb, w)


def _rw_prep_kernel(*refs, nb, tb, width, lora, has_state, has_prev):
    pm_ref, pl_ref = refs[:2]
    sh_ref = refs[2] if has_state else None
    mu_ref, w0_ref, w2_ref, a0_ref, a2_ref, g2_ref, kk_ref, ka_ref = refs[2 + has_state:10 + has_state]
    r_ref, w_ref, k_ref, v_ref, kn_ref, b_ref, g_ref, last_ref = refs[10 + has_state + has_prev:]
    ti = pl.program_id(1)
    rows = nb * tb
    tot = last_ref.shape[-1]
    x = jnp.concatenate([pm_ref[...], pl_ref[...]], axis=-1).reshape(rows, tot)

    @pl.when(ti == 0)
    def _():
        last_ref[...] = sh_ref[...] if has_state else jnp.zeros((nb, 1, tot), f32)

    prev = _shift_rows(x, [_bcast_rows(last_ref[...], nb, tb)], tb, 1)
    last_ref[...] = jnp.concatenate([pm_ref[:, tb - 1:tb, :], pl_ref[:, tb - 1:tb, :]], axis=-1)
    xs = x + (prev - x) * mu_ref[...]
    dl, al, gl = lora
    c = 3 * width
    r, kr, vr = xs[:, :width], xs[:, width:2 * width], xs[:, 2 * width:c]
    wd, ad, gd = xs[:, c:c + dl], xs[:, c + dl:c + dl + al], xs[:, c + dl + al:c + dl + al + gl]
    z = -(w0_ref[...] + _dot(jnp.tanh(wd).astype(bf16), w2_ref[...]))
    softplus = jnp.maximum(z, 0.0) + jnp.log(1.0 + jnp.exp(-jnp.abs(z)))
    w_log = -softplus - 0.5
    log_decay = -jnp.exp(w_log)
    a = _gate_sigmoid(a0_ref[...] + _dot(ad.astype(bf16), a2_ref[...]))
    g = _dot(_gate_sigmoid(gd).astype(bf16), g2_ref[...])
    kk = kr * kk_ref[...]
    ones_bd = _group_ones(V7X_LANES, RW_N)
    nrm = jnp.sqrt(_group_sum(kk * kk, ones_bd))
    kk = kk / jnp.maximum(nrm, 1e-12)
    k = kr * (1.0 + (a - 1.0) * ka_ref[...])
    shp = (nb, tb, width)
    r_ref[...] = r.reshape(shp).astype(r_ref.dtype)
    w_ref[...] = log_decay.reshape(shp)
    k_ref[...] = k.reshape(shp).astype(k_ref.dtype)
    v_ref[...] = vr.reshape(shp).astype(v_ref.dtype)
    kn_ref[...] = (-kk).reshape(shp).astype(kn_ref.dtype)
    b_ref[...] = (kk * a).reshape(shp).astype(b_ref.dtype)
    g_ref[...] = g.reshape(shp).astype(g_ref.dtype)


def _rw_prep(p_main, p_lora, shift_in, shift_prev, layer, depth, wts, bsz, tlen, seq_dtype):
    mu, w0, w2, a0, a2, g2, k_k, k_a = wts
    width = w0.shape[1]
    wl = p_lora.shape[1]
    tot = 3 * width + wl
    lora = (w2.shape[0], a2.shape[0], g2.shape[0])
    nb, tb = _seq_block(bsz, tlen)
    has_state = shift_in is not None
    blk = lambda w: pl.BlockSpec((nb, tb, w), lambda bi, ti: (bi, ti, 0))
    full = lambda arr: pl.BlockSpec(arr.shape, lambda bi, ti: (0,) * arr.ndim)
    last_spec = pl.BlockSpec((None, nb, 1, tot), lambda bi, ti: (layer, bi, 0, 0))
    small = [mu, w0, w2, a0, a2, g2, k_k, k_a]
    in_specs = [blk(3 * width), blk(wl)] + ([last_spec] if has_state else []) + [full(s) for s in small]
    args = ([p_main.reshape(bsz, tlen, 3 * width), p_lora.reshape(bsz, tlen, wl)]
            + ([shift_in] if has_state else []) + small)
    aliases = _threaded_state(args, in_specs, shift_prev, 7)
    seq = lambda i: jax.ShapeDtypeStruct((bsz, tlen, width), f32 if i == 1 else seq_dtype)
    outs = pl.pallas_call(
        functools.partial(_rw_prep_kernel, nb=nb, tb=tb, width=width, lora=lora, has_state=has_state,
                          has_prev=shift_prev is not None),
        grid=(bsz // nb, tlen // tb),
        in_specs=in_specs,
        out_specs=[blk(width)] * 7 + [last_spec],
        out_shape=[seq(i) for i in range(7)] + [jax.ShapeDtypeStruct((depth, bsz, 1, tot), f32)],
        input_output_aliases=aliases,
        compiler_params=_params("parallel", "arbitrary"),
        name="rwkv7_prep",
    )(*args)
    return outs[:7], outs[7]


def _rw_state_io(s0_ref, s_ref, st_scr, combos, has_state, ti, last):
    n = RW_N

    @pl.when(ti == 0)
    def _():
        for c, (b, p) in enumerate(combos):
            if has_state:
                st_scr[c] = jnp.concatenate([s0_ref[b, RW_PAIR * p + h] for h in range(RW_PAIR)], axis=-1)
            else:
                st_scr[c] = jnp.zeros((n, RW_PAIR * n), f32)

    def finish():
        @pl.when(ti == last)
        def _():
            for c, (b, p) in enumerate(combos):
                s = st_scr[c]
                for h in range(RW_PAIR):
                    s_ref[b, RW_PAIR * p + h] = s[:, h * n:(h + 1) * n]

    return finish


def _rw_scan_kernel(*refs, nb, tb, pairs, has_state, has_prev):
    r_ref, w_ref, k_ref, v_ref, kn_ref, b_ref = refs[:6]
    s0_ref = refs[6] if has_state else None
    o_ref, s_ref, st_scr, lhs_a, lhs_v, lhs_o = refs[6 + has_state + has_prev:]
    ti = pl.program_id(1)
    n = RW_N
    lanes = RW_PAIR * n
    sub = V7X_SUBLANES
    combos = [(b, p) for b in range(nb) for p in range(pairs)]
    finish = _rw_state_io(s0_ref, s_ref, st_scr, combos, has_state, ti, pl.num_programs(1) - 1)

    ones_bd = _group_ones(lanes, n)
    diag = (lax.broadcasted_iota(jnp.int32, (n, lanes), 0)
            == lax.broadcasted_iota(jnp.int32, (n, lanes), 1) % n)

    def token_group(grp, carry):
        base = pl.multiple_of(grp * sub, sub)

        def row(ref, b, p, j, decay=False):
            tile = ref[b, pl.ds(base, sub), pl.ds(p * lanes, lanes)]
            tile = jnp.exp(tile) if decay else tile
            return jnp.broadcast_to(tile[j:j + 1], (n, lanes))

        o_rows = [[] for _ in combos]
        for j in range(sub):
            for c, (b, p) in enumerate(combos):
                lhs_a[c * n:(c + 1) * n, :] = (st_scr[c] * row(kn_ref, b, p, j)).astype(bf16)
                lhs_v[c * n:(c + 1) * n, :] = jnp.where(diag, row(v_ref, b, p, j), 0.0).astype(bf16)
            sa_all = _dot(lhs_a[...], ones_bd)
            vb_all = _dot(lhs_v[...], ones_bd)
            for c, (b, p) in enumerate(combos):
                s_new = (st_scr[c] * row(w_ref, b, p, j, decay=True)
                         + sa_all[c * n:(c + 1) * n] * row(b_ref, b, p, j)
                         + vb_all[c * n:(c + 1) * n] * row(k_ref, b, p, j))
                st_scr[c] = s_new
                lhs_o[c * n:(c + 1) * n, :] = (s_new * row(r_ref, b, p, j)).astype(bf16)
            o_all = _dot(lhs_o[...], ones_bd)
            for c, (b, p) in enumerate(combos):
                o_rows[c].append(jnp.sum(jnp.where(diag, o_all[c * n:(c + 1) * n], 0.0),
                                         axis=0, keepdims=True))
        for c, (b, p) in enumerate(combos):
            o_ref[b, pl.ds(base, sub), pl.ds(p * lanes, lanes)] = jnp.concatenate(o_rows[c], axis=0)
        return carry

    lax.fori_loop(0, tb // sub, token_group, 0)
    finish()


def _rw_chunk_kernel(*refs, nb, tb, pairs, has_state, has_prev):
    r_ref, w_ref, k_ref, v_ref, kn_ref, b_ref = refs[:6]
    s0_ref = refs[6] if has_state else None
    o_ref, s_ref, st_scr, coef_scr, x_scr = refs[6 + has_state + has_prev:]
    ti = pl.program_id(1)
    n = RW_N
    lanes = RW_PAIR * n
    ln = RW_CHUNK
    combos = [(b, p) for b in range(nb) for p in range(pairs)]
    finish = _rw_state_io(s0_ref, s_ref, st_scr, combos, has_state, ti, pl.num_programs(1) - 1)

    ones_bd = _group_ones(lanes, n)
    t_idx = lax.broadcasted_iota(jnp.int32, (ln, lanes), 0)
    lane_idx = lax.broadcasted_iota(jnp.int32, (ln, lanes), 1)
    row_i = lax.broadcasted_iota(jnp.int32, (4 * ln, 4 * ln), 0)
    col_i = lax.broadcasted_iota(jnp.int32, (4 * ln, 4 * ln), 1)
    same_head = (row_i % (2 * ln)) // ln == (col_i % (2 * ln)) // ln
    strict = jnp.where(row_i < 2 * ln, 1, 0)
    lower2 = jnp.where(same_head & (col_i % ln + strict <= row_i % ln), 1.0, 0.0)
    nt_dims = (((1,), (1,)), ((), ()))

    def by_head(x):
        head0 = lane_idx < n
        return jnp.concatenate([jnp.where(head0, x, 0.0), jnp.where(head0, 0.0, x)], axis=0)

    def stack_heads(x):
        return jnp.concatenate([x[:, :n], x[:, n:]], axis=0)

    def pack_heads(xs):
        return jnp.concatenate([xs[:ln], xs[ln:]], axis=-1)

    half = ln // 2
    solve_cols = list(range(half - 1)) + list(range(half, ln - 1))

    def prepare(ck, c, out):
        b, p = combos[c]
        base = ck * ln
        tile = lambda ref: ref[b, base:base + ln, p * lanes:(p + 1) * lanes].astype(f32)
        lw = tile(w_ref)
        cum = lw
        step = 1
        while step < ln:
            cum = cum + jnp.where(t_idx >= step, pltpu.roll(cum, step, 0), 0.0)
            step *= 2
        c_last = jnp.broadcast_to(cum[ln - 1:ln], (ln, lanes))
        inv_c = jnp.exp(-cum)
        to_end = jnp.exp(c_last - cum)
        kn, bb, kk, vv = tile(kn_ref), tile(b_ref), tile(k_ref), tile(v_ref)
        n_dec = kn * jnp.exp(cum - lw)
        b_dec = bb * inv_c
        k_dec = kk * inv_c
        r_dec = tile(r_ref) * jnp.exp(cum)
        lhs_nr = jnp.concatenate([by_head(n_dec), by_head(r_dec)], axis=0).astype(bf16)
        rhs_bk = jnp.concatenate([by_head(b_dec), by_head(k_dec)], axis=0).astype(bf16)
        sc = (lower2 * lax.dot_general(lhs_nr, rhs_bk, nt_dims, preferred_element_type=f32)).astype(bf16)
        blocks = []
        for s in solve_cols:
            lo = 0 if s < half else half
            blocks.append(jnp.where(t_idx[lo:lo + half] > s,
                                    n_dec[lo:lo + half] * jnp.broadcast_to(b_dec[s:s + 1], (half, lanes)), 0.0))
        coef_scr[ck % 2, c] = _dot(jnp.concatenate(blocks, axis=0).astype(bf16), ones_bd)
        sc_cross = jnp.concatenate([sc[half:ln], sc[ln + half:2 * ln]], axis=0)
        vs = stack_heads(vv)
        upd_v = _dot(vs.T.astype(bf16), by_head(kk * to_end).astype(bf16))
        out.append((lhs_nr, sc[:2 * ln], sc[2 * ln:], vs.astype(bf16), by_head(bb * to_end).astype(bf16),
                    jnp.exp(cum[ln - 1:ln]), sc_cross, upd_v))

    ncombo = len(combos)
    pad = jnp.zeros((2 * ln, n), bf16)

    def advance(ck, pre, fill):
        base = ck * ln
        slot = ck % 2
        per_step = -(-len(fill) // (2 * (half - 1)))
        gs = [lax.dot_general(pre[c][0], st_scr[c].astype(bf16), nt_dims, preferred_element_type=f32)
              for c in range(ncombo)]
        for c in range(ncombo):
            x_scr[c] = pack_heads(gs[c][:2 * ln] + _dot(pre[c][1], jnp.concatenate([pad, pre[c][3]], axis=0)))

        def solve_half(lo, first_block):
            xh = [x_scr[c, lo:lo + half, :] for c in range(ncombo)]
            for si in range(half - 1):
                for c in range(ncombo):
                    coef = coef_scr[slot, c, (first_block + si) * half:(first_block + si + 1) * half, :]
                    xh[c] = xh[c] + coef * jnp.broadcast_to(xh[c][si:si + 1], (half, lanes))
                for _ in range(min(per_step, len(fill))):
                    fill.pop(0)()
            for c in range(ncombo):
                x_scr[c, lo:lo + half, :] = xh[c]

        solve_half(0, 0)
        for c in range(ncombo):
            x_top = jnp.concatenate([x_scr[c, :half, :], jnp.zeros((half, lanes), f32)], axis=0)
            cross = _dot(pre[c][6], jnp.concatenate([stack_heads(x_top).astype(bf16), pad], axis=0))
            x_scr[c, half:, :] = x_scr[c, half:, :] + jnp.concatenate([cross[:half], cross[half:]], axis=-1)
        solve_half(half, half - 1)
        while fill:
            fill.pop(0)()
        xss = [stack_heads(x_scr[c]) for c in range(ncombo)]
        xs_ts = [xs.T.astype(bf16) for xs in xss]
        for c, (b, p) in enumerate(combos):
            _, _, sc_out, vs_bf, rhs_b_end, d_last, _, upd_v = pre[c]
            o = gs[c][2 * ln:] + _dot(sc_out, jnp.concatenate([xss[c].astype(bf16), vs_bf], axis=0))
            o_ref[b, base:base + ln, p * lanes:(p + 1) * lanes] = pack_heads(o)
            st_scr[c] = st_scr[c] * d_last + _dot(xs_ts[c], rhs_b_end) + upd_v

    nck = tb // ln
    pre = []
    for c in range(ncombo):
        prepare(0, c, pre)
    for ck in range(nck):
        nxt = []
        fill = [functools.partial(prepare, ck + 1, c, nxt) for c in range(ncombo)] if ck + 1 < nck else []
        advance(ck, pre, fill)
        pre = nxt
    finish()


def _rw_lanes_kernel(*refs, tlen, has_prev):
    r_ref, w_ref, k_ref, v_ref, kn_ref, b_ref, s0_ref = refs[:7]
    o_ref, s_ref, vec_scr, o_scr = refs[7 + has_prev:]
    n = RW_N
    bsz = V7X_LANES
    sub = V7X_SUBLANES
    s_ref[...] = s0_ref[...]
    for t in range(tlen):
        for slot, ref in enumerate((kn_ref, w_ref, b_ref, k_ref, r_ref, v_ref)):
            x = ref[pl.ds(t, bsz, stride=tlen), :].T
            vec_scr[slot] = jnp.exp(x) if slot == 1 else x

        def value_rows(ig, carry):
            i0 = pl.multiple_of(ig * sub, sub)
            for h in range(RW_PAIR):
                kn_h, w_h, b_h, k_h, r_h = (vec_scr[slot, h * n:(h + 1) * n, :] for slot in range(5))
                v_tile = vec_scr[5, pl.ds(h * n + i0, sub), :]
                o_rows = []
                for ii in range(sub):
                    s_old = s_ref[h, i0 + ii]
                    sa = jnp.sum(s_old * kn_h, axis=0, keepdims=True)
                    s_new = s_old * w_h + sa * b_h + v_tile[ii:ii + 1] * k_h
                    s_ref[h, i0 + ii] = s_new
                    o_rows.append(jnp.sum(s_new * r_h, axis=0, keepdims=True))
                o_scr[pl.ds(h * n + i0, sub), :] = jnp.concatenate(o_rows, axis=0)
            return carry

        lax.fori_loop(0, n // sub, value_rows, 0)
        o_ref[pl.ds(t, bsz, stride=tlen), :] = o_scr[...].T


def _rw_lanes_scan(seqs, s_in, s_prev, layer, depth, bsz, tlen):
    r, w, k, v, kn, bb = seqs
    width = r.shape[-1]
    heads = width // RW_N
    lanes = RW_PAIR * RW_N
    rows = bsz * tlen
    blk = pl.BlockSpec((rows, lanes), lambda p: (0, p))
    st_spec = pl.BlockSpec((None, RW_PAIR, RW_N, RW_N, bsz), lambda p: (layer, p, 0, 0, 0))
    in_specs = [blk] * 6 + [st_spec]
    args = [a.reshape(rows, width) for a in (r, w, k, v, kn, bb)] + [s_in]
    aliases = _threaded_state(args, in_specs, s_prev, 1)
    o, s_new = pl.pallas_call(
        functools.partial(_rw_lanes_kernel, tlen=tlen, has_prev=s_prev is not None),
        grid=(heads // RW_PAIR,),
        in_specs=in_specs,
        out_specs=[blk, st_spec],
        out_shape=[jax.ShapeDtypeStruct((rows, width), f32),
                   jax.ShapeDtypeStruct((depth, heads, RW_N, RW_N, bsz), f32)],
        scratch_shapes=[pltpu.VMEM((6, lanes, bsz), f32), pltpu.VMEM((lanes, bsz), f32)],
        input_output_aliases=aliases,
        compiler_params=_params("parallel"),
        name="rwkv7_lanes_scan",
    )(*args)
    return o.reshape(bsz, tlen, width), s_new


def _rw_mode(bsz, tlen):
    if _tile(tlen, RW_BLOCK_TOKENS) % RW_CHUNK == 0:
        return "chunk"
    return "lanes" if bsz == V7X_LANES else "token"


def _rw_scan(seqs, s_in, s_prev, layer, depth, bsz, tlen):
    if _rw_mode(bsz, tlen) == "lanes" and s_in is not None:
        return _rw_lanes_scan(seqs, s_in, s_prev, layer, depth, bsz, tlen)
    r, w, k, v, kn, bb = seqs
    width = r.shape[-1]
    heads = width // RW_N
    pairs = heads // RW_PAIR
    nb = _tile(bsz, 4, align=1)
    tb = _tile(tlen, RW_BLOCK_TOKENS)
    chunked = tb % RW_CHUNK == 0
    has_state = s_in is not None
    blk = pl.BlockSpec((nb, tb, width), lambda bi, ti: (bi, ti, 0))
    st_spec = pl.BlockSpec((None, nb, heads, RW_N, RW_N), lambda bi, ti: (layer, bi, 0, 0, 0))
    m = nb * pairs * RW_N
    lanes = RW_PAIR * RW_N
    in_specs = [blk] * 6 + ([st_spec] if has_state else [])
    args = [r, w, k, v, kn, bb] + ([s_in] if has_state else [])
    aliases = _threaded_state(args, in_specs, s_prev, 1)
    scratch = [pltpu.VMEM((nb * pairs, RW_N, lanes), f32)]
    if chunked:
        scratch += [pltpu.VMEM((2, nb * pairs, (RW_CHUNK - 2) * (RW_CHUNK // 2), lanes), f32),
                    pltpu.VMEM((nb * pairs, RW_CHUNK, lanes), f32)]
    else:
        scratch += [pltpu.VMEM((m, lanes), bf16)] * 3
    o, s_new = pl.pallas_call(
        functools.partial(_rw_chunk_kernel if chunked else _rw_scan_kernel, nb=nb, tb=tb, pairs=pairs,
                          has_state=has_state, has_prev=s_prev is not None),
        grid=(bsz // nb, tlen // tb),
        in_specs=in_specs,
        out_specs=[blk, st_spec],
        out_shape=[jax.ShapeDtypeStruct((bsz, tlen, width), f32),
                   jax.ShapeDtypeStruct((depth, bsz, heads, RW_N, RW_N), f32)],
        scratch_shapes=scratch,
        input_output_aliases=aliases,
        compiler_params=_params("parallel", "arbitrary"),
        name="rwkv7_chunk_scan" if chunked else "rwkv7_scan",
    )(*args)
    return o, s_new


def _rw_post_kernel(o_ref, r_ref, k_ref, v_ref, g_ref, lw_ref, lbias_ref, rk_ref, y_ref):
    ones_bd = _group_ones(V7X_LANES, RW_N)
    o = o_ref[...]
    mu = _group_sum(o, ones_bd) * (1.0 / RW_N)
    d = o - mu
    var = _group_sum(d * d, ones_bd) * (1.0 / RW_N)
    on = d * lax.rsqrt(var + RW_GN_EPS) * lw_ref[...] + lbias_ref[...]
    rk = r_ref[...].astype(f32) * k_ref[...].astype(f32) * rk_ref[...]
    bonus = _group_sum(rk, ones_bd) * v_ref[...].astype(f32)
    y_ref[...] = ((on + bonus) * g_ref[...].astype(f32)).astype(y_ref.dtype)


def _rw_post(o, r, k, v, g, ln_w, ln_b, r_k):
    n, width = o.shape
    tm = _tile(n, 512)
    blk = pl.BlockSpec((tm, width), lambda i: (i, 0))
    vec = pl.BlockSpec((1, width), lambda i: (0, 0))
    return pl.pallas_call(
        _rw_post_kernel,
        grid=(n // tm,),
        in_specs=[blk] * 5 + [vec] * 3,
        out_specs=blk,
        out_shape=jax.ShapeDtypeStruct((n, width), bf16),
        compiler_params=_params("parallel"),
        name="rwkv7_post",
    )(o, r, k, v, g, ln_w, ln_b, r_k)


def _conv_kernel(*refs, nb, tb, has_state, has_prev):
    sb_ref, sc_ref, sh_ref = refs[:3]
    st_ref = refs[3] if has_state else None
    cw_ref = refs[3 + has_state]
    y_ref, new_ref = refs[4 + has_state + has_prev:]
    ti = pl.program_id(2)
    rows = nb * tb
    w = sb_ref.shape[-1]
    hist = SC_KSIZE - 1

    @pl.when(ti == 0)
    def _():
        new_ref[...] = st_ref[...] if has_state else jnp.zeros((nb, hist, w), f32)

    u = (sc_ref[...].astype(f32) * sh_ref[...].astype(f32)).reshape(rows, w)
    carry = new_ref[...]
    conv = cw_ref[hist:hist + 1, :] * u
    for back in range(1, hist + 1):
        first = [_bcast_rows(carry[:, hist - back + j:hist - back + j + 1, :], nb, tb) for j in range(back)]
        conv = conv + cw_ref[hist - back:hist - back + 1, :] * _shift_rows(u, first, tb, back)
    y_ref[...] = (sb_ref[...].astype(f32).reshape(rows, w) * conv).reshape(nb, tb, w).astype(y_ref.dtype)
    new_ref[...] = u.reshape(nb, tb, w)[:, tb - hist:tb, :]


def _conv(p_sc, st_in, st_prev, layer, depth, conv_w, bsz, tlen):
    width = conv_w.shape[1]
    cw = _tile(width, 1024, align=V7X_LANES)
    ncol = width // cw
    nb, tb = _seq_block(bsz, tlen)
    assert tb >= SC_KSIZE - 1
    has_state = st_in is not None
    p3 = p_sc.reshape(bsz, tlen, 3 * width)
    seg = lambda s: pl.BlockSpec((nb, tb, cw), lambda bi, cj, ti, s=s: (bi, ti, s * ncol + cj))
    st_spec = pl.BlockSpec((None, nb, SC_KSIZE - 1, cw), lambda bi, cj, ti: (layer, bi, 0, cj))
    in_specs = ([seg(0), seg(1), seg(2)] + ([st_spec] if has_state else [])
                + [pl.BlockSpec((SC_KSIZE, cw), lambda bi, cj, ti: (0, cj))])
    args = [p3, p3, p3] + ([st_in] if has_state else []) + [conv_w]
    aliases = _threaded_state(args, in_specs, st_prev, 1)
    y, new = pl.pallas_call(
        functools.partial(_conv_kernel, nb=nb, tb=tb, has_state=has_state, has_prev=st_prev is not None),
        grid=(bsz // nb, ncol, tlen // tb),
        in_specs=in_specs,
        out_specs=[pl.BlockSpec((nb, tb, cw), lambda bi, cj, ti: (bi, ti, cj)), st_spec],
        out_shape=[jax.ShapeDtypeStruct((bsz, tlen, width), bf16),
                   jax.ShapeDtypeStruct((depth, bsz, SC_KSIZE - 1, width), f32)],
        input_output_aliases=aliases,
        compiler_params=_params("parallel", "parallel", "arbitrary"),
        name="short_conv",
    )(*args)
    return y.reshape(bsz * tlen, width), new


def _merge_kernel(a_ref, b_ref, c_ref, ga_ref, gb_ref, gc_ref, pa_ref, pb_ref, pc_ref, m_ref, w_scr):
    @pl.when(pl.program_id(1) == 0)
    def _():
        for s, w_ref in enumerate((pa_ref, pb_ref, pc_ref)):
            w_scr[s] = w_ref[...].astype(bf16)

    m = (_gate_sigmoid(ga_ref[...].astype(f32)) * _dot(a_ref[...], w_scr[0])
         + _gate_sigmoid(gb_ref[...].astype(f32)) * _dot(b_ref[...], w_scr[1])
         + _gate_sigmoid(gc_ref[...].astype(f32)) * _dot(c_ref[...], w_scr[2]))
    m_ref[...] = m.astype(m_ref.dtype)


def _merge(ya, yb, yc, p_gate, w_pa, w_pb, w_pc, layer):
    n = ya.shape[0]
    width, d = w_pa.shape[1:]
    assert w_pb.shape[1] == width and w_pc.shape[1] == width
    tm = _tile(n, 1024)
    tn = _tile(d, 512, align=V7X_LANES)
    nj = d // tn
    act = pl.BlockSpec((tm, width), lambda j, i: (i, 0))
    gate = lambda s: pl.BlockSpec((tm, tn), lambda j, i, s=s: (i, s * nj + j))
    wt = pl.BlockSpec((None, width, tn), lambda j, i: (layer, 0, j))
    return pl.pallas_call(
        _merge_kernel,
        grid=(nj, n // tm),
        in_specs=[act, act, act, gate(0), gate(1), gate(2), wt, wt, wt],
        out_specs=pl.BlockSpec((tm, tn), lambda j, i: (i, j)),
        out_shape=jax.ShapeDtypeStruct((n, d), bf16),
        scratch_shapes=[pltpu.VMEM((3, width, tn), bf16)],
        compiler_params=_params("arbitrary", "arbitrary"),
        name="gated_merge",
    )(ya, yb, yc, p_gate, p_gate, p_gate, w_pa, w_pb, w_pc)


def _oproj_kernel(m_ref, wo_ref, x_ref, g1_ref, g2_ref, x1_ref, h2_ref):
    mix = _dot(m_ref[...], wo_ref[...])
    x1 = x_ref[...] + _rms(mix, g1_ref[...])
    x1_ref[...] = x1
    h2_ref[...] = _rms(x1, g2_ref[...]).astype(h2_ref.dtype)


def _oproj(m, w_o, layer, x, g_post_mix, g_pre_mlp):
    n, d = x.shape
    tm = _tile(n, 512)
    blk = pl.BlockSpec((tm, d), lambda i: (i, 0))
    vec = pl.BlockSpec((1, d), lambda i: (0, 0))
    return pl.pallas_call(
        _oproj_kernel,
        grid=(n // tm,),
        in_specs=[blk, pl.BlockSpec((None, d, d), lambda i: (layer, 0, 0)), blk, vec, vec],
        out_specs=[blk, blk],
        out_shape=[jax.ShapeDtypeStruct((n, d), f32), jax.ShapeDtypeStruct((n, d), bf16)],
        compiler_params=_params("parallel"),
        name="out_proj_norms",
    )(m, w_o, x, g_post_mix, g_pre_mlp)


def _mlp_kernel(h_ref, w1_ref, w2_ref, ff_ref):
    kf = pl.program_id(1)

    @pl.when(kf == 0)
    def _():
        ff_ref[...] = jnp.zeros_like(ff_ref)

    t = jnp.maximum(_dot(h_ref[...], w1_ref[...].astype(bf16)), 0.0)
    ff_ref[...] += _dot((t * t).astype(bf16), w2_ref[...].astype(bf16))


def _mlp(h2, w1, w2, layer):
    n, d = h2.shape
    dff = w1.shape[2]
    tm = _tile(n, 2048)
    fc = _tile(dff, 512, align=V7X_LANES)
    once = pl.Buffered(1)
    return pl.pallas_call(
        _mlp_kernel,
        grid=(n // tm, dff // fc),
        in_specs=[pl.BlockSpec((tm, d), lambda i, kf: (i, 0), pipeline_mode=once),
                  pl.BlockSpec((None, d, fc), lambda i, kf: (layer, 0, kf)),
                  pl.BlockSpec((None, fc, d), lambda i, kf: (layer, kf, 0))],
        out_specs=pl.BlockSpec((tm, d), lambda i, kf: (i, 0), pipeline_mode=once),
        out_shape=jax.ShapeDtypeStruct((n, d), f32),
        compiler_params=_params("parallel", "arbitrary", vmem=V7X_VMEM_BIG_TILE_BYTES),
        name="relu2_mlp",
    )(h2, w1, w2)


def _residual_kernel(*refs, has_next):
    if has_next:
        x1_ref, ff_ref, g_ref, gn_ref, y_ref, h_ref = refs
    else:
        x1_ref, ff_ref, g_ref, y_ref = refs
    y = x1_ref[...] + _rms(ff_ref[...], g_ref[...])
    y_ref[...] = y
    if has_next:
        h_ref[...] = _rms(y, gn_ref[...]).astype(h_ref.dtype)


def _residual(x1, ff, g_post_mlp, g_next):
    n, d = x1.shape
    tm = _tile(n, 512)
    blk = pl.BlockSpec((tm, d), lambda i: (i, 0))
    vec = pl.BlockSpec((1, d), lambda i: (0, 0))
    has_next = g_next is not None
    outs = pl.pallas_call(
        functools.partial(_residual_kernel, has_next=has_next),
        grid=(n // tm,),
        in_specs=[blk, blk, vec] + ([vec] if has_next else []),
        out_specs=[blk] + ([blk] if has_next else []),
        out_shape=[jax.ShapeDtypeStruct((n, d), f32)] + ([jax.ShapeDtypeStruct((n, d), bf16)] if has_next else []),
        compiler_params=_params("parallel"),
        name="mlp_residual_norm",
    )(*([x1, ff, g_post_mlp] + ([g_next] if has_next else [])))
    return (outs[0], outs[1]) if has_next else (outs[0], None)


def _layer(x, h, bsz, tlen, states_in, states_prev, layer, depth, wts, lw, g_next):
    hg_in, rw_in, shift_in, sc_in = states_in
    hg_prev, rw_prev, shift_prev, sc_prev = states_prev
    w_in, offs = wts["w_in"], wts["offs"]
    p_hg = _proj(h, w_in, layer, offs[0], offs[1] - offs[0], "in_proj_hgrn")
    p_rkv = _proj(h, w_in, layer, offs[1], offs[2] - offs[1], "in_proj_rwkv")
    p_lora = _proj(h, w_in, layer, offs[2], offs[3] - offs[2], "in_proj_rwkv_lora")
    p_sc = _proj(h, w_in, layer, offs[3], offs[4] - offs[3], "in_proj_conv", bf16)
    p_gate = _proj(h, w_in, layer, offs[4], offs[5] - offs[4], "in_proj_gate", bf16)

    ya, hg_new = _hgrn(p_hg, lw["lb"], lw["hg_norm"], hg_in, hg_prev, layer, depth, bsz, tlen)

    seq_dtype = bf16 if _rw_mode(bsz, tlen) == "chunk" else f32
    seqs, shift_new = _rw_prep(p_rkv, p_lora, shift_in, shift_prev, layer, depth, lw["rw_prep"], bsz, tlen,
                               seq_dtype)
    r, w, k, v, kn, bb, g = seqs
    o_rw, rw_new = _rw_scan((r, w, k, v, kn, bb), rw_in, rw_prev, layer, depth, bsz, tlen)
    flat = lambda t: t.reshape(bsz * tlen, t.shape[-1])
    yb = _rw_post(flat(o_rw), flat(r), flat(k), flat(v), flat(g), lw["rw_ln_w"], lw["rw_ln_b"], lw["rw_r_k"])

    yc, sc_new = _conv(p_sc, sc_in, sc_prev, layer, depth, lw["sc_conv_w"], bsz, tlen)

    m = _merge(ya, yb, yc, p_gate, wts["w_pa"], wts["w_pb"], wts["w_pc"], layer)
    x1, h2 = _oproj(m, wts["w_o"], layer, x, lw["g_post_mix"], lw["g_pre_mlp"])
    ff = _mlp(h2, wts["w_ff1"], wts["w_ff2"], layer)
    y, h_next = _residual(x1, ff, lw["g_post_mlp"], g_next)
    return y, h_next, (hg_new, rw_new, shift_new, sc_new)


def kernel(x_prompt, x_sample, state_hgrn, state_rwkv, state_rwkv_shift, state_conv, norm_pre_mix, norm_post_mix, norm_pre_mlp, norm_post_mlp, w_in, hg_lb_logits, hg_norm, w_pa, rw_mu, rw_w0, rw_w2, rw_a0, rw_a2, rw_g2, rw_k_k, rw_k_a, rw_r_k, rw_ln_w, rw_ln_b, w_pb, sc_conv_w, w_pc, w_o, w_ff1, w_ff2):
    depth = w_in.shape[0]
    d_model = x_prompt.shape[-1]
    hg_width = hg_norm.shape[1]
    rw_width = rw_w0.shape[1]
    rw_shift_width = rw_mu.shape[1]
    sc_width = sc_conv_w.shape[2]
    off_rw = 4 * hg_width
    off_lora = off_rw + 3 * rw_width
    off_sc = off_rw + rw_shift_width
    off_gate = off_sc + 3 * sc_width
    assert w_in.shape[2] == off_gate + 3 * d_model

    lb_all = _lower_bounds(hg_lb_logits.astype(f32))
    row = lambda a, l: a[l].reshape(1, -1).astype(f32)
    wts = {"w_in": w_in.astype(f32), "offs": (0, off_rw, off_lora, off_sc, off_gate, w_in.shape[2]),
           "w_pa": w_pa.astype(f32), "w_pb": w_pb.astype(f32), "w_pc": w_pc.astype(f32),
           "w_o": w_o.astype(bf16), "w_ff1": w_ff1.astype(f32), "w_ff2": w_ff2.astype(f32)}

    bp, tp, _ = x_prompt.shape
    bs, ts, _ = x_sample.shape
    yp = x_prompt.reshape(bp * tp, d_model).astype(f32)
    ys = x_sample.reshape(bs * ts, d_model).astype(f32)
    hp = _rmsnorm_cast(yp, row(norm_pre_mix, 0))
    hs = _rmsnorm_cast(ys, row(norm_pre_mix, 0))
    rw_lanes = _rw_mode(bs, ts) == "lanes"
    rw_state = state_rwkv.astype(f32)
    sample_in = (state_hgrn.astype(f32), jnp.transpose(rw_state, (0, 2, 3, 4, 1)) if rw_lanes else rw_state,
                 state_rwkv_shift.astype(f32).reshape(depth, bs, 1, rw_shift_width), state_conv.astype(f32))
    new_p = new_s = (None, None, None, None)
    for l in range(depth):
        lw = {
            "g_post_mix": row(norm_post_mix, l), "g_pre_mlp": row(norm_pre_mlp, l),
            "g_post_mlp": row(norm_post_mlp, l),
            "lb": lb_all[l:l + 1], "hg_norm": row(hg_norm, l),
            "rw_prep": (row(rw_mu, l), row(rw_w0, l), rw_w2[l].astype(bf16), row(rw_a0, l), rw_a2[l].astype(bf16),
                        rw_g2[l].astype(bf16), row(rw_k_k, l), row(rw_k_a, l)),
            "rw_ln_w": row(rw_ln_w, l), "rw_ln_b": row(rw_ln_b, l), "rw_r_k": row(rw_r_k, l),
            "sc_conv_w": sc_conv_w[l].astype(f32),
        }
        g_next = row(norm_pre_mix, l + 1) if l + 1 < depth else None
        yp, hp, new_p = _layer(yp, hp, bp, tp, (None, None, None, None), new_p, l, depth, wts, lw, g_next)
        ys, hs, new_s = _layer(ys, hs, bs, ts, sample_in, new_s, l, depth, wts, lw, g_next)
    shift = lambda s, b: s.reshape(depth, b, rw_shift_width)
    return (yp.reshape(bp, tp, d_model), ys.reshape(bs, ts, d_model),
            new_p[0], new_p[1], shift(new_p[2], bp), new_p[3],
            new_s[0], jnp.transpose(new_s[1], (0, 4, 1, 2, 3)) if rw_lanes else new_s[1],
            shift(new_s[2], bs), new_s[3])
```

```python
import functools

import jax
import jax.numpy as jnp
from jax import lax
from jax.experimental import pallas as pl
from jax.experimental.pallas import tpu as pltpu

HG_DK = 128
RW_N = 64
HG_F_MIN = 1e-30
LOG2_E = 1.4426950408889634
RW_GN_EPS = 64e-5
NORM_EPS = 1e-6
SC_KSIZE = 3

V7X_LANES = 128
V7X_SUBLANES = 8
V7X_BF16_ROWS = 16
V7X_VMEM_LIMIT_BYTES = 48 * 1024 * 1024
V7X_VMEM_BIG_TILE_BYTES = 56 * 1024 * 1024
HG_CHUNK = 16
HG_HEADS_PER_STEP = 8
HG_GROUP = 4
RW_PAIR = 2
RW_CHUNK = 16
RW_BLOCK_TOKENS = 128

f32 = jnp.float32
bf16 = jnp.bfloat16


def _tile(n, pref, align=V7X_SUBLANES):
    if n <= pref:
        return n
    for d in range(pref, 0, -1):
        if n % d == 0 and d % align == 0:
            return d
    return n


def _seq_block(bsz, tlen):
    if tlen >= 256:
        return 1, _tile(tlen, 256)
    return _tile(bsz, max(1, 128 // tlen), align=1), tlen


def _params(*sem, vmem=V7X_VMEM_LIMIT_BYTES):
    return pltpu.CompilerParams(dimension_semantics=sem, vmem_limit_bytes=vmem)


def _sigmoid(x):
    return 1.0 / (1.0 + jnp.exp(-x))


def _gate_sigmoid(x):
    return 0.5 * jnp.tanh(0.5 * x) + 0.5


def _dot(a, b):
    return jnp.dot(a, b, preferred_element_type=f32)


def _rms(x, g):
    return x * lax.rsqrt(jnp.mean(x * x, axis=-1, keepdims=True) + NORM_EPS) * g


def _group_ones(width, group):
    r = lax.broadcasted_iota(jnp.int32, (width, width), 0) // group
    c = lax.broadcasted_iota(jnp.int32, (width, width), 1) // group
    return jnp.where(r == c, 1.0, 0.0).astype(bf16)


def _group_sum(z, ones_bd):
    outs = []
    for c in range(z.shape[-1] // V7X_LANES):
        zc = z[:, c * V7X_LANES:(c + 1) * V7X_LANES]
        hi = zc.astype(bf16)
        lo = (zc - hi.astype(f32)).astype(bf16)
        outs.append(_dot(hi, ones_bd) + _dot(lo, ones_bd))
    return outs[0] if len(outs) == 1 else jnp.concatenate(outs, axis=-1)


def _threaded_state(args, in_specs, prev_out, out_index):
    if prev_out is None:
        return {}
    args.append(prev_out)
    in_specs.append(pl.BlockSpec(memory_space=pl.ANY))
    return {len(args) - 1: out_index}


def _lb_kernel(logit_ref, lb_ref):
    z = logit_ref[...]
    depth = z.shape[0]
    m = jnp.max(z, axis=0, keepdims=True)
    e = jnp.exp(z - m)
    p = e / jnp.sum(e, axis=0, keepdims=True)
    acc = jnp.zeros_like(p[0:1])
    for l in range(depth):
        acc = acc + p[l:l + 1]
        lb_ref[l:l + 1, :] = jnp.clip(acc - p[0:1], 0.0, 1.0)


def _lower_bounds(logits):
    return pl.pallas_call(
        _lb_kernel, out_shape=jax.ShapeDtypeStruct(logits.shape, f32), name="hg_lower_bounds",
    )(logits)


def _rmsnorm_kernel(x_ref, g_ref, o_ref):
    o_ref[...] = _rms(x_ref[...], g_ref[...]).astype(o_ref.dtype)


def _rmsnorm_cast(x, g):
    n, d = x.shape
    tm = _tile(n, 512)
    return pl.pallas_call(
        _rmsnorm_kernel,
        grid=(n // tm,),
        in_specs=[pl.BlockSpec((tm, d), lambda i: (i, 0)), pl.BlockSpec((1, d), lambda i: (0, 0))],
        out_specs=pl.BlockSpec((tm, d), lambda i: (i, 0)),
        out_shape=jax.ShapeDtypeStruct((n, d), bf16),
        compiler_params=_params("parallel"),
        name="rmsnorm_cast",
    )(x, g)


def _proj_kernel(a_ref, w_ref, o_ref, wbf_scr):
    @pl.when(pl.program_id(1) == 0)
    def _():
        wbf_scr[...] = w_ref[0].astype(bf16)

    o_ref[...] = _dot(a_ref[...], wbf_scr[...]).astype(o_ref.dtype)


def _proj(h, w_stack, layer, off, width, name, out_dtype=f32):
    n, k = h.shape
    big = out_dtype == bf16
    tm = _tile(n, 2048 if big else 1024)
    tn = _tile(width, 1024, align=V7X_LANES)
    w_spec = pl.BlockSpec((pl.Element(1), pl.Element(k), pl.Element(tn)),
                          lambda j, i: (layer, 0, pl.multiple_of(off + j * tn, V7X_LANES)))
    return pl.pallas_call(
        _proj_kernel,
        grid=(width // tn, n // tm),
        in_specs=[pl.BlockSpec((tm, k), lambda j, i: (i, 0)), w_spec],
        out_specs=pl.BlockSpec((tm, tn), lambda j, i: (i, j)),
        out_shape=jax.ShapeDtypeStruct((n, width), out_dtype),
        scratch_shapes=[pltpu.VMEM((k, tn), bf16)],
        compiler_params=_params("arbitrary", "arbitrary",
                                vmem=V7X_VMEM_BIG_TILE_BYTES if big else V7X_VMEM_LIMIT_BYTES),
        name=name,
    )(h, w_stack)


def _hgrn_kernel(*refs, nb, tb, hb, chunk, has_state, has_prev):
    q_ref, f_ref, i_ref, og_ref, lb_ref, gn_ref, rsel_ref = refs[:7]
    s0_ref = refs[7] if has_state else None
    o_ref, s_ref, st_scr = refs[7 + has_state + has_prev:]
    ti = pl.program_id(2)
    rows = nb * tb
    dk = HG_DK
    nchunk = rows // chunk
    heads = range(hb)

    @pl.when(ti == 0)
    def _():
        for b in range(nb):
            for h in heads:
                st_scr[b, h] = s0_ref[b, h].T if has_state else jnp.zeros((dk, dk), f32)

    grp = HG_GROUP if (tb // chunk) % HG_GROUP == 0 else 1
    row_idx = lax.broadcasted_iota(jnp.int32, (rows, dk), 0)
    t_idx = row_idx % chunk
    pos_in_grp = (row_idx // chunk) % grp
    chunk_r = lax.broadcasted_iota(jnp.int32, (rows, rows), 0) // chunk
    chunk_c = lax.broadcasted_iota(jnp.int32, (rows, rows), 1) // chunk
    chunk_masks = [jnp.where((chunk_r - chunk_c == dist) & (chunk_r // grp == chunk_c // grp), 1.0, 0.0)
                   for dist in range(grp)]
    ngroup = nchunk // grp
    col_group = lax.broadcasted_iota(jnp.int32, (dk, rows), 1) // (grp * chunk)

    def chunk_row(a, s):
        a3 = a.reshape(nchunk, chunk, dk)
        return jnp.broadcast_to(a3[:, s:s + 1, :], (nchunk, chunk, dk)).reshape(rows, dk)

    def head_cols(ref, h):
        return ref[:, :, h * dk:(h + 1) * dk].reshape(rows, dk)

    sub = V7X_SUBLANES
    tpc = chunk // sub

    def tiles(a):
        return a.reshape(nchunk, tpc, sub, dk)

    span = grp * chunk

    def state_steps(h, q_dec, glog, o_intra, u_all):
        outs = []

        def step(b, g):
            gi = b * (tb // span) + g
            r0 = gi * span
            st = st_scr[b, h]
            inter = lax.dot_general(q_dec[r0:r0 + span], st.astype(bf16),
                                    (((1,), (1,)), ((), ())), preferred_element_type=f32)
            outs.append(inter + o_intra[r0:r0 + span])
            st_scr[b, h] = jnp.exp(glog[r0 + span - 1:r0 + span]) * st + u_all[gi * dk:(gi + 1) * dk]

        def finish():
            o = jnp.concatenate(outs, axis=0)
            o = o * lax.rsqrt(jnp.mean(o * o, axis=-1, keepdims=True) + NORM_EPS)
            hog = head_cols(og_ref, h)
            o = o * gn_ref[:, h * dk:(h + 1) * dk] * (hog * _gate_sigmoid(hog))
            o_ref[:, :, h * dk:(h + 1) * dk] = o.reshape(nb, tb, dk).astype(o_ref.dtype)

        return [functools.partial(step, b, g) for b in range(nb) for g in range(tb // span)] + [finish]

    fill = []
    for h in heads:
        per_col = -(-len(fill) // chunk)
        lb = lb_ref[:, h * dk:(h + 1) * dk]
        hq = head_cols(q_ref, h)
        qq = hq * _gate_sigmoid(hq)
        f = lb + (1.0 - lb) * _sigmoid(head_cols(f_ref, h))
        kk = 1.0 - f
        bcum = jnp.log(jnp.maximum(f, HG_F_MIN))
        step = 1
        while step < chunk:
            bcum = bcum + jnp.where(t_idx >= step, pltpu.roll(bcum, step, 0), 0.0)
            step *= 2
        b_last = chunk_row(bcum, chunk - 1)
        zs = []
        b2 = bcum * LOG2_E
        for s in range(chunk):
            t0 = s // sub

            def col_row(a):
                return jnp.broadcast_to(tiles(a)[:, t0:t0 + 1, s % sub:s % sub + 1, :], (nchunk, 1, sub, dk))

            b2_s, k_s = col_row(b2), col_row(kk)
            parts = [jnp.zeros((nchunk, t0, sub, dk), f32)] if t0 else []
            for tt in range(t0, tpc):
                diff = tiles(b2)[:, tt:tt + 1] - b2_s
                qk = tiles(qq)[:, tt:tt + 1] * k_s
                if tt == t0:
                    z = jnp.where(tiles(t_idx)[:, tt:tt + 1] >= s, qk * jnp.exp2(jnp.minimum(diff, 0.0)), 0.0)
                else:
                    z = qk * jnp.exp2(diff)
                parts.append(z)
            z = parts[0] if len(parts) == 1 else jnp.concatenate(parts, axis=1)
            zs.append(z.reshape(rows, dk).astype(bf16))
            for _ in range(min(per_col, len(fill))):
                fill.pop(0)()
        k_dec = (kk * jnp.exp(b_last - bcum)).astype(bf16)
        vv = head_cols(i_ref, h)
        a_rep = _dot(jnp.concatenate(zs, axis=-1), rsel_ref[...])
        scores = a_rep * chunk_masks[0]
        between = jnp.zeros_like(bcum)
        for dist in range(1, grp):
            q_far = (qq * jnp.exp(bcum + between)).astype(bf16)
            far = lax.dot_general(q_far, k_dec, (((1,), (1,)), ((), ())), preferred_element_type=f32)
            scores = scores + far * chunk_masks[dist]
            between = between + pltpu.roll(b_last, dist * chunk, 0)
        reach = jnp.zeros_like(bcum)
        tail = jnp.zeros_like(bcum)
        for dist in range(1, grp):
            reach = reach + jnp.where(pos_in_grp >= dist, pltpu.roll(b_last, dist * chunk, 0), 0.0)
            tail = tail + jnp.where(pos_in_grp + dist < grp, pltpu.roll(b_last, rows - dist * chunk, 0), 0.0)
        glog = bcum + reach
        q_dec = (qq * jnp.exp(glog)).astype(bf16)
        o_intra = _dot(scores.astype(bf16), vv.astype(bf16))
        k_grp = k_dec if grp == 1 else (kk * jnp.exp(b_last - bcum + tail)).astype(bf16)
        v_t = vv.T
        lhs = jnp.concatenate([jnp.where(col_group == g, v_t, 0.0).astype(bf16) for g in range(ngroup)], axis=0)
        u_all = _dot(lhs, k_grp)
        while fill:
            fill.pop(0)()
        fill = state_steps(h, q_dec, glog, o_intra, u_all)
    while fill:
        fill.pop(0)()

    @pl.when(ti == pl.num_programs(2) - 1)
    def _():
        for b in range(nb):
            for h in heads:
                s_ref[b, h] = st_scr[b, h].T


def _hgrn(p_hg, lb, gn, s_in, s_prev, layer, depth, bsz, tlen):
    width = p_hg.shape[1] // 4
    heads = width // HG_DK
    chunk = min(HG_CHUNK, tlen)
    nb, tb = _seq_block(bsz, tlen)
    assert tb % chunk == 0
    p3 = p_hg.reshape(bsz, tlen, 4 * width)
    hb = HG_HEADS_PER_STEP if heads % HG_HEADS_PER_STEP == 0 else 1
    hgroups = heads // hb
    wb = hb * HG_DK
    seg = lambda s: pl.BlockSpec((nb, tb, wb), lambda bi, h, ti, s=s: (bi, ti, s * hgroups + h))
    vec = pl.BlockSpec((1, wb), lambda bi, h, ti: (0, h))
    st_spec = pl.BlockSpec((None, nb, hb, HG_DK, HG_DK), lambda bi, h, ti: (layer, bi, h, 0, 0))
    has_state = s_in is not None
    rows = nb * tb
    rsel = (jnp.arange(chunk * HG_DK)[:, None] // HG_DK == jnp.arange(rows)[None, :] % chunk).astype(bf16)
    rsel_spec = pl.BlockSpec((chunk * HG_DK, rows), lambda bi, h, ti: (0, 0))
    in_specs = [seg(0), seg(1), seg(2), seg(3), vec, vec, rsel_spec] + ([st_spec] if has_state else [])
    args = [p3, p3, p3, p3, lb, gn, rsel] + ([s_in] if has_state else [])
    aliases = _threaded_state(args, in_specs, s_prev, 1)
    o, s_new = pl.pallas_call(
        functools.partial(_hgrn_kernel, nb=nb, tb=tb, hb=hb, chunk=chunk, has_state=has_state,
                          has_prev=s_prev is not None),
        grid=(bsz // nb, hgroups, tlen // tb),
        in_specs=in_specs,
        out_specs=[pl.BlockSpec((nb, tb, wb), lambda bi, h, ti: (bi, ti, h)), st_spec],
        out_shape=[jax.ShapeDtypeStruct((bsz, tlen, width), bf16),
                   jax.ShapeDtypeStruct((depth, bsz, heads, HG_DK, HG_DK), f32)],
        scratch_shapes=[pltpu.VMEM((nb, hb, HG_DK, HG_DK), f32)],
        input_output_aliases=aliases,
        compiler_params=_params("parallel", "parallel", "arbitrary", vmem=V7X_VMEM_BIG_TILE_BYTES),
        name="hgrn2_scan",
    )(*args)
    return o.reshape(bsz * tlen, width), s_new


def _shift_rows(x, first, tb, by):
    rows = x.shape[0]
    t_idx = lax.broadcasted_iota(jnp.int32, (rows, 1), 0) % tb
    out = pltpu.roll(x, by, 0) if tb > by else x
    for j in range(by):
        out = jnp.where(t_idx == j, first[j], out)
    return out


def _bcast_rows(v3, nb, tb):
    w = v3.shape[-1]
    return jnp.broadcast_to(v3, (nb, tb, w)).reshape(nb * tb, w)


def _rw_prep_kernel(*refs, nb, tb, width, lora, has_state, has_prev):
    pm_ref, pl_ref = refs[:2]
    sh_ref = refs[2] if has_state else None
    mu_ref, w0_ref, w2_ref, a0_ref, a2_ref, g2_ref, kk_ref, ka_ref = refs[2 + has_state:10 + has_state]
    r_ref, w_ref, k_ref, v_ref, kn_ref, b_ref, g_ref, last_ref = refs[10 + has_state + has_prev:]
    ti = pl.program_id(1)
    rows = nb * tb
    tot = last_ref.shape[-1]
    x = jnp.concatenate([pm_ref[...], pl_ref[...]], axis=-1).reshape(rows, tot)

    @pl.when(ti == 0)
    def _():
        last_ref[...] = sh_ref[...] if has_state else jnp.zeros((nb, 1, tot), f32)

    prev = _shift_rows(x, [_bcast_rows(last_ref[...], nb, tb)], tb, 1)
    last_ref[...] = jnp.concatenate([pm_ref[:, tb - 1:tb, :], pl_ref[:, tb - 1:tb, :]], axis=-1)
    xs = x + (prev - x) * mu_ref[...]
    dl, al, gl = lora
    c = 3 * width
    r, kr, vr = xs[:, :width], xs[:, width:2 * width], xs[:, 2 * width:c]
    wd, ad, gd = xs[:, c:c + dl], xs[:, c + dl:c + dl + al], xs[:, c + dl + al:c + dl + al + gl]
    z = -(w0_ref[...] + _dot(jnp.tanh(wd).astype(bf16), w2_ref[...]))
    softplus = jnp.maximum(z, 0.0) + jnp.log(1.0 + jnp.exp(-jnp.abs(z)))
    w_log = -softplus - 0.5
    log_decay = -jnp.exp(w_log)
    a = _gate_sigmoid(a0_ref[...] + _dot(ad.astype(bf16), a2_ref[...]))
    g = _dot(_gate_sigmoid(gd).astype(bf16), g2_ref[...])
    kk = kr * kk_ref[...]
    ones_bd = _group_ones(V7X_LANES, RW_N)
    nrm = jnp.sqrt(_group_sum(kk * kk, ones_bd))
    kk = kk / jnp.maximum(nrm, 1e-12)
    k = kr * (1.0 + (a - 1.0) * ka_ref[...])
    shp = (nb, tb, width)
    r_ref[...] = r.reshape(shp).astype(r_ref.dtype)
    w_ref[...] = log_decay.reshape(shp)
    k_ref[...] = k.reshape(shp).astype(k_ref.dtype)
    v_ref[...] = vr.reshape(shp).astype(v_ref.dtype)
    kn_ref[...] = (-kk).reshape(shp).astype(kn_ref.dtype)
    b_ref[...] = (kk * a).reshape(shp).astype(b_ref.dtype)
    g_ref[...] = g.reshape(shp).astype(g_ref.dtype)


def _rw_prep(p_main, p_lora, shift_in, shift_prev, layer, depth, wts, bsz, tlen, seq_dtype):
    mu, w0, w2, a0, a2, g2, k_k, k_a = wts
    width = w0.shape[1]
    wl = p_lora.shape[1]
    tot = 3 * width + wl
    lora = (w2.shape[0], a2.shape[0], g2.shape[0])
    nb, tb = _seq_block(bsz, tlen)
    has_state = shift_in is not None
    blk = lambda w: pl.BlockSpec((nb, tb, w), lambda bi, ti: (bi, ti, 0))
    full = lambda arr: pl.BlockSpec(arr.shape, lambda bi, ti: (0,) * arr.ndim)
    last_spec = pl.BlockSpec((None, nb, 1, tot), lambda bi, ti: (layer, bi, 0, 0))
    small = [mu, w0, w2, a0, a2, g2, k_k, k_a]
    in_specs = [blk(3 * width), blk(wl)] + ([last_spec] if has_state else []) + [full(s) for s in small]
    args = ([p_main.reshape(bsz, tlen, 3 * width), p_lora.reshape(bsz, tlen, wl)]
            + ([shift_in] if has_state else []) + small)
    aliases = _threaded_state(args, in_specs, shift_prev, 7)
    seq = lambda i: jax.ShapeDtypeStruct((bsz, tlen, width), f32 if i == 1 else seq_dtype)
    outs = pl.pallas_call(
        functools.partial(_rw_prep_kernel, nb=nb, tb=tb, width=width, lora=lora, has_state=has_state,
                          has_prev=shift_prev is not None),
        grid=(bsz // nb, tlen // tb),
        in_specs=in_specs,
        out_specs=[blk(width)] * 7 + [last_spec],
        out_shape=[seq(i) for i in range(7)] + [jax.ShapeDtypeStruct((depth, bsz, 1, tot), f32)],
        input_output_aliases=aliases,
        compiler_params=_params("parallel", "arbitrary"),
        name="rwkv7_prep",
    )(*args)
    return outs[:7], outs[7]


def _rw_state_io(s0_ref, s_ref, st_scr, combos, has_state, ti, last):
    n = RW_N

    @pl.when(ti == 0)
    def _():
        for c, (b, p) in enumerate(combos):
            if has_state:
                st_scr[c] = jnp.concatenate([s0_ref[b, RW_PAIR * p + h] for h in range(RW_PAIR)], axis=-1)
            else:
                st_scr[c] = jnp.zeros((n, RW_PAIR * n), f32)

    def finish():
        @pl.when(ti == last)
        def _():
            for c, (b, p) in enumerate(combos):
                s = st_scr[c]
                for h in range(RW_PAIR):
                    s_ref[b, RW_PAIR * p + h] = s[:, h * n:(h + 1) * n]

    return finish


def _rw_scan_kernel(*refs, nb, tb, pairs, has_state, has_prev):
    r_ref, w_ref, k_ref, v_ref, kn_ref, b_ref = refs[:6]
    s0_ref = refs[6] if has_state else None
    o_ref, s_ref, st_scr, lhs_a, lhs_v, lhs_o = refs[6 + has_state + has_prev:]
    ti = pl.program_id(1)
    n = RW_N
    lanes = RW_PAIR * n
    sub = V7X_SUBLANES
    combos = [(b, p) for b in range(nb) for p in range(pairs)]
    finish = _rw_state_io(s0_ref, s_ref, st_scr, combos, has_state, ti, pl.num_programs(1) - 1)

    ones_bd = _group_ones(lanes, n)
    diag = (lax.broadcasted_iota(jnp.int32, (n, lanes), 0)
            == lax.broadcasted_iota(jnp.int32, (n, lanes), 1) % n)

    def token_group(grp, carry):
        base = pl.multiple_of(grp * sub, sub)

        def row(ref, b, p, j, decay=False):
            tile = ref[b, pl.ds(base, sub), pl.ds(p * lanes, lanes)]
            tile = jnp.exp(tile) if decay else tile
            return jnp.broadcast_to(tile[j:j + 1], (n, lanes))

        o_rows = [[] for _ in combos]
        for j in range(sub):
            for c, (b, p) in enumerate(combos):
                lhs_a[c * n:(c + 1) * n, :] = (st_scr[c] * row(kn_ref, b, p, j)).astype(bf16)
                lhs_v[c * n:(c + 1) * n, :] = jnp.where(diag, row(v_ref, b, p, j), 0.0).astype(bf16)
            sa_all = _dot(lhs_a[...], ones_bd)
            vb_all = _dot(lhs_v[...], ones_bd)
            for c, (b, p) in enumerate(combos):
                s_new = (st_scr[c] * row(w_ref, b, p, j, decay=True)
                         + sa_all[c * n:(c + 1) * n] * row(b_ref, b, p, j)
                         + vb_all[c * n:(c + 1) * n] * row(k_ref, b, p, j))
                st_scr[c] = s_new
                lhs_o[c * n:(c + 1) * n, :] = (s_new * row(r_ref, b, p, j)).astype(bf16)
            o_all = _dot(lhs_o[...], ones_bd)
            for c, (b, p) in enumerate(combos):
                o_rows[c].append(jnp.sum(jnp.where(diag, o_all[c * n:(c + 1) * n], 0.0),
                                         axis=0, keepdims=True))
        for c, (b, p) in enumerate(combos):
            o_ref[b, pl.ds(base, sub), pl.ds(p * lanes, lanes)] = jnp.concatenate(o_rows[c], axis=0)
        return carry

    lax.fori_loop(0, tb // sub, token_group, 0)
    finish()


def _rw_chunk_kernel(*refs, nb, tb, pairs, has_state, has_prev):
    r_ref, w_ref, k_ref, v_ref, kn_ref, b_ref = refs[:6]
    s0_ref = refs[6] if has_state else None
    o_ref, s_ref, st_scr, coef_scr, x_scr = refs[6 + has_state + has_prev:]
    ti = pl.program_id(1)
    n = RW_N
    lanes = RW_PAIR * n
    ln = RW_CHUNK
    combos = [(b, p) for b in range(nb) for p in range(pairs)]
    finish = _rw_state_io(s0_ref, s_ref, st_scr, combos, has_state, ti, pl.num_programs(1) - 1)

    ones_bd = _group_ones(lanes, n)
    t_idx = lax.broadcasted_iota(jnp.int32, (ln, lanes), 0)
    lane_idx = lax.broadcasted_iota(jnp.int32, (ln, lanes), 1)
    row_i = lax.broadcasted_iota(jnp.int32, (4 * ln, 4 * ln), 0)
    col_i = lax.broadcasted_iota(jnp.int32, (4 * ln, 4 * ln), 1)
    same_head = (row_i % (2 * ln)) // ln == (col_i % (2 * ln)) // ln
    strict = jnp.where(row_i < 2 * ln, 1, 0)
    lower2 = jnp.where(same_head & (col_i % ln + strict <= row_i % ln), 1.0, 0.0)
    nt_dims = (((1,), (1,)), ((), ()))

    def by_head(x):
        head0 = lane_idx < n
        return jnp.concatenate([jnp.where(head0, x, 0.0), jnp.where(head0, 0.0, x)], axis=0)

    def stack_heads(x):
        return jnp.concatenate([x[:, :n], x[:, n:]], axis=0)

    def pack_heads(xs):
        return jnp.concatenate([xs[:ln], xs[ln:]], axis=-1)

    half = ln // 2
    solve_cols = list(range(half - 1)) + list(range(half, ln - 1))

    def prepare(ck, c, out):
        b, p = combos[c]
        base = ck * ln
        tile = lambda ref: ref[b, base:base + ln, p * lanes:(p + 1) * lanes].astype(f32)
        lw = tile(w_ref)
        cum = lw
        step = 1
        while step < ln:
            cum = cum + jnp.where(t_idx >= step, pltpu.roll(cum, step, 0), 0.0)
            step *= 2
        c_last = jnp.broadcast_to(cum[ln - 1:ln], (ln, lanes))
        inv_c = jnp.exp(-cum)
        to_end = jnp.exp(c_last - cum)
        kn, bb, kk, vv = tile(kn_ref), tile(b_ref), tile(k_ref), tile(v_ref)
        n_dec = kn * jnp.exp(cum - lw)
        b_dec = bb * inv_c
        k_dec = kk * inv_c
        r_dec = tile(r_ref) * jnp.exp(cum)
        lhs_nr = jnp.concatenate([by_head(n_dec), by_head(r_dec)], axis=0).astype(bf16)
        rhs_bk = jnp.concatenate([by_head(b_dec), by_head(k_dec)], axis=0).astype(bf16)
        sc = (lower2 * lax.dot_general(lhs_nr, rhs_bk, nt_dims, preferred_element_type=f32)).astype(bf16)
        blocks = []
        for s in solve_cols:
            lo = 0 if s < half else half
            blocks.append(jnp.where(t_idx[lo:lo + half] > s,
                                    n_dec[lo:lo + half] * jnp.broadcast_to(b_dec[s:s + 1], (half, lanes)), 0.0))
        coef_scr[ck % 2, c] = _dot(jnp.concatenate(blocks, axis=0).astype(bf16), ones_bd)
        sc_cross = jnp.concatenate([sc[half:ln], sc[ln + half:2 * ln]], axis=0)
        vs = stack_heads(vv)
        upd_v = _dot(vs.T.astype(bf16), by_head(kk * to_end).astype(bf16))
        out.append((lhs_nr, sc[:2 * ln], sc[2 * ln:], vs.astype(bf16), by_head(bb * to_end).astype(bf16),
                    jnp.exp(cum[ln - 1:ln]), sc_cross, upd_v))

    ncombo = len(combos)
    pad = jnp.zeros((2 * ln, n), bf16)

    def advance(ck, pre, fill):
        base = ck * ln
        slot = ck % 2
        per_step = -(-len(fill) // (2 * (half - 1)))
        gs = [lax.dot_general(pre[c][0], st_scr[c].astype(bf16), nt_dims, preferred_element_type=f32)
              for c in range(ncombo)]
        for c in range(ncombo):
            x_scr[c] = pack_heads(gs[c][:2 * ln] + _dot(pre[c][1], jnp.concatenate([pad, pre[c][3]], axis=0)))

        def solve_half(lo, first_block):
            xh = [x_scr[c, lo:lo + half, :] for c in range(ncombo)]
            for si in range(half - 1):
                for c in range(ncombo):
                    coef = coef_scr[slot, c, (first_block + si) * half:(first_block + si + 1) * half, :]
                    xh[c] = xh[c] + coef * jnp.broadcast_to(xh[c][si:si + 1], (half, lanes))
                for _ in range(min(per_step, len(fill))):
                    fill.pop(0)()
            for c in range(ncombo):
                x_scr[c, lo:lo + half, :] = xh[c]

        solve_half(0, 0)
        for c in range(ncombo):
            x_top = jnp.concatenate([x_scr[c, :half, :], jnp.zeros((half, lanes), f32)], axis=0)
            cross = _dot(pre[c][6], jnp.concatenate([stack_heads(x_top).astype(bf16), pad], axis=0))
            x_scr[c, half:, :] = x_scr[c, half:, :] + jnp.concatenate([cross[:half], cross[half:]], axis=-1)
        solve_half(half, half - 1)
        while fill:
            fill.pop(0)()
        xss = [stack_heads(x_scr[c]) for c in range(ncombo)]
        xs_ts = [xs.T.astype(bf16) for xs in xss]
        for c, (b, p) in enumerate(combos):
            _, _, sc_out, vs_bf, rhs_b_end, d_last, _, upd_v = pre[c]
            o = gs[c][2 * ln:] + _dot(sc_out, jnp.concatenate([xss[c].astype(bf16), vs_bf], axis=0))
            o_ref[b, base:base + ln, p * lanes:(p + 1) * lanes] = pack_heads(o)
            st_scr[c] = st_scr[c] * d_last + _dot(xs_ts[c], rhs_b_end) + upd_v

    nck = tb // ln
    pre = []
    for c in range(ncombo):
        prepare(0, c, pre)
    for ck in range(nck):
        nxt = []
        fill = [functools.partial(prepare, ck + 1, c, nxt) for c in range(ncombo)] if ck + 1 < nck else []
        advance(ck, pre, fill)
        pre = nxt
    finish()


def _rw_lanes_kernel(*refs, tlen, has_prev):
    r_ref, w_ref, k_ref, v_ref, kn_ref, b_ref, s0_ref = refs[:7]
    o_ref, s_ref, vec_scr, o_scr = refs[7 + has_prev:]
    n = RW_N
    bsz = V7X_LANES
    sub = V7X_SUBLANES
    s_ref[...] = s0_ref[...]
    for t in range(tlen):
        for slot, ref in enumerate((kn_ref, w_ref, b_ref, k_ref, r_ref, v_ref)):
            x = ref[pl.ds(t, bsz, stride=tlen), :].T
            vec_scr[slot] = jnp.exp(x) if slot == 1 else x

        def value_rows(ig, carry):
            i0 = pl.multiple_of(ig * sub, sub)
            for h in range(RW_PAIR):
                kn_h, w_h, b_h, k_h, r_h = (vec_scr[slot, h * n:(h + 1) * n, :] for slot in range(5))
                v_tile = vec_scr[5, pl.ds(h * n + i0, sub), :]
                o_rows = []
                for ii in range(sub):
                    s_old = s_ref[h, i0 + ii]
                    sa = jnp.sum(s_old * kn_h, axis=0, keepdims=True)
                    s_new = s_old * w_h + sa * b_h + v_tile[ii:ii + 1] * k_h
                    s_ref[h, i0 + ii] = s_new
                    o_rows.append(jnp.sum(s_new * r_h, axis=0, keepdims=True))
                o_scr[pl.ds(h * n + i0, sub), :] = jnp.concatenate(o_rows, axis=0)
            return carry

        lax.fori_loop(0, n // sub, value_rows, 0)
        o_ref[pl.ds(t, bsz, stride=tlen), :] = o_scr[...].T


def _rw_lanes_scan(seqs, s_in, s_prev, layer, depth, bsz, tlen):
    r, w, k, v, kn, bb = seqs
    width = r.shape[-1]
    heads = width // RW_N
    lanes = RW_PAIR * RW_N
    rows = bsz * tlen
    blk = pl.BlockSpec((rows, lanes), lambda p: (0, p))
    st_spec = pl.BlockSpec((None, RW_PAIR, RW_N, RW_N, bsz), lambda p: (layer, p, 0, 0, 0))
    in_specs = [blk] * 6 + [st_spec]
    args = [a.reshape(rows, width) for a in (r, w, k, v, kn, bb)] + [s_in]
    aliases = _threaded_state(args, in_specs, s_prev, 1)
    o, s_new = pl.pallas_call(
        functools.partial(_rw_lanes_kernel, tlen=tlen, has_prev=s_prev is not None),
        grid=(heads // RW_PAIR,),
        in_specs=in_specs,
        out_specs=[blk, st_spec],
        out_shape=[jax.ShapeDtypeStruct((rows, width), f32),
                   jax.ShapeDtypeStruct((depth, heads, RW_N, RW_N, bsz), f32)],
        scratch_shapes=[pltpu.VMEM((6, lanes, bsz), f32), pltpu.VMEM((lanes, bsz), f32)],
        input_output_aliases=aliases,
        compiler_params=_params("parallel"),
        name="rwkv7_lanes_scan",
    )(*args)
    return o.reshape(bsz, tlen, width), s_new


def _rw_mode(bsz, tlen):
    if _tile(tlen, RW_BLOCK_TOKENS) % RW_CHUNK == 0:
        return "chunk"
    return "lanes" if bsz == V7X_LANES else "token"


def _rw_scan(seqs, s_in, s_prev, layer, depth, bsz, tlen):
    if _rw_mode(bsz, tlen) == "lanes" and s_in is not None:
        return _rw_lanes_scan(seqs, s_in, s_prev, layer, depth, bsz, tlen)
    r, w, k, v, kn, bb = seqs
    width = r.shape[-1]
    heads = width // RW_N
    pairs = heads // RW_PAIR
    nb = _tile(bsz, 4, align=1)
    tb = _tile(tlen, RW_BLOCK_TOKENS)
    chunked = tb % RW_CHUNK == 0
    has_state = s_in is not None
    blk = pl.BlockSpec((nb, tb, width), lambda bi, ti: (bi, ti, 0))
    st_spec = pl.BlockSpec((None, nb, heads, RW_N, RW_N), lambda bi, ti: (layer, bi, 0, 0, 0))
    m = nb * pairs * RW_N
    lanes = RW_PAIR * RW_N
    in_specs = [blk] * 6 + ([st_spec] if has_state else [])
    args = [r, w, k, v, kn, bb] + ([s_in] if has_state else [])
    aliases = _threaded_state(args, in_specs, s_prev, 1)
    scratch = [pltpu.VMEM((nb * pairs, RW_N, lanes), f32)]
    if chunked:
        scratch += [pltpu.VMEM((2, nb * pairs, (RW_CHUNK - 2) * (RW_CHUNK // 2), lanes), f32),
                    pltpu.VMEM((nb * pairs, RW_CHUNK, lanes), f32)]
    else:
        scratch += [pltpu.VMEM((m, lanes), bf16)] * 3
    o, s_new = pl.pallas_call(
        functools.partial(_rw_chunk_kernel if chunked else _rw_scan_kernel, nb=nb, tb=tb, pairs=pairs,
                          has_state=has_state, has_prev=s_prev is not None),
        grid=(bsz // nb, tlen // tb),
        in_specs=in_specs,
        out_specs=[blk, st_spec],
        out_shape=[jax.ShapeDtypeStruct((bsz, tlen, width), f32),
                   jax.ShapeDtypeStruct((depth, bsz, heads, RW_N, RW_N), f32)],
        scratch_shapes=scratch,
        input_output_aliases=aliases,
        compiler_params=_params("parallel", "arbitrary"),
        name="rwkv7_chunk_scan" if chunked else "rwkv7_scan",
    )(*args)
    return o, s_new


def _rw_post_kernel(o_ref, r_ref, k_ref, v_ref, g_ref, lw_ref, lbias_ref, rk_ref, y_ref):
    ones_bd = _group_ones(V7X_LANES, RW_N)
    o = o_ref[...]
    mu = _group_sum(o, ones_bd) * (1.0 / RW_N)
    d = o - mu
    var = _group_sum(d * d, ones_bd) * (1.0 / RW_N)
    on = d * lax.rsqrt(var + RW_GN_EPS) * lw_ref[...] + lbias_ref[...]
    rk = r_ref[...].astype(f32) * k_ref[...].astype(f32) * rk_ref[...]
    bonus = _group_sum(rk, ones_bd) * v_ref[...].astype(f32)
    y_ref[...] = ((on + bonus) * g_ref[...].astype(f32)).astype(y_ref.dtype)


def _rw_post(o, r, k, v, g, ln_w, ln_b, r_k):
    n, width = o.shape
    tm = _tile(n, 512)
    blk = pl.BlockSpec((tm, width), lambda i: (i, 0))
    vec = pl.BlockSpec((1, width), lambda i: (0, 0))
    return pl.pallas_call(
        _rw_post_kernel,
        grid=(n // tm,),
        in_specs=[blk] * 5 + [vec] * 3,
        out_specs=blk,
        out_shape=jax.ShapeDtypeStruct((n, width), bf16),
        compiler_params=_params("parallel"),
        name="rwkv7_post",
    )(o, r, k, v, g, ln_w, ln_b, r_k)


def _conv_kernel(*refs, nb, tb, has_state, has_prev):
    sb_ref, sc_ref, sh_ref = refs[:3]
    st_ref = refs[3] if has_state else None
    cw_ref = refs[3 + has_state]
    y_ref, new_ref = refs[4 + has_state + has_prev:]
    ti = pl.program_id(2)
    rows = nb * tb
    w = sb_ref.shape[-1]
    hist = SC_KSIZE - 1

    @pl.when(ti == 0)
    def _():
        new_ref[...] = st_ref[...] if has_state else jnp.zeros((nb, hist, w), f32)

    u = (sc_ref[...].astype(f32) * sh_ref[...].astype(f32)).reshape(rows, w)
    carry = new_ref[...]
    conv = cw_ref[hist:hist + 1, :] * u
    for back in range(1, hist + 1):
        first = [_bcast_rows(carry[:, hist - back + j:hist - back + j + 1, :], nb, tb) for j in range(back)]
        conv = conv + cw_ref[hist - back:hist - back + 1, :] * _shift_rows(u, first, tb, back)
    y_ref[...] = (sb_ref[...].astype(f32).reshape(rows, w) * conv).reshape(nb, tb, w).astype(y_ref.dtype)
    new_ref[...] = u.reshape(nb, tb, w)[:, tb - hist:tb, :]


def _conv(p_sc, st_in, st_prev, layer, depth, conv_w, bsz, tlen):
    width = conv_w.shape[1]
    cw = _tile(width, 1024, align=V7X_LANES)
    ncol = width // cw
    nb, tb = _seq_block(bsz, tlen)
    assert tb >= SC_KSIZE - 1
    has_state = st_in is not None
    p3 = p_sc.reshape(bsz, tlen, 3 * width)
    seg = lambda s: pl.BlockSpec((nb, tb, cw), lambda bi, cj, ti, s=s: (bi, ti, s * ncol + cj))
    st_spec = pl.BlockSpec((None, nb, SC_KSIZE - 1, cw), lambda bi, cj, ti: (layer, bi, 0, cj))
    in_specs = ([seg(0), seg(1), seg(2)] + ([st_spec] if has_state else [])
                + [pl.BlockSpec((SC_KSIZE, cw), lambda bi, cj, ti: (0, cj))])
    args = [p3, p3, p3] + ([st_in] if has_state else []) + [conv_w]
    aliases = _threaded_state(args, in_specs, st_prev, 1)
    y, new = pl.pallas_call(
        functools.partial(_conv_kernel, nb=nb, tb=tb, has_state=has_state, has_prev=st_prev is not None),
        grid=(bsz // nb, ncol, tlen // tb),
        in_specs=in_specs,
        out_specs=[pl.BlockSpec((nb, tb, cw), lambda bi, cj, ti: (bi, ti, cj)), st_spec],
        out_shape=[jax.ShapeDtypeStruct((bsz, tlen, width), bf16),
                   jax.ShapeDtypeStruct((depth, bsz, SC_KSIZE - 1, width), f32)],
        input_output_aliases=aliases,
        compiler_params=_params("parallel", "parallel", "arbitrary"),
        name="short_conv",
    )(*args)
    return y.reshape(bsz * tlen, width), new


def _merge_kernel(a_ref, b_ref, c_ref, ga_ref, gb_ref, gc_ref, pa_ref, pb_ref, pc_ref, m_ref, w_scr):
    @pl.when(pl.program_id(1) == 0)
    def _():
        for s, w_ref in enumerate((pa_ref, pb_ref, pc_ref)):
            w_scr[s] = w_ref[...].astype(bf16)

    m = (_gate_sigmoid(ga_ref[...].astype(f32)) * _dot(a_ref[...], w_scr[0])
         + _gate_sigmoid(gb_ref[...].astype(f32)) * _dot(b_ref[...], w_scr[1])
         + _gate_sigmoid(gc_ref[...].astype(f32)) * _dot(c_ref[...], w_scr[2]))
    m_ref[...] = m.astype(m_ref.dtype)


def _merge(ya, yb, yc, p_gate, w_pa, w_pb, w_pc, layer):
    n = ya.shape[0]
    width, d = w_pa.shape[1:]
    assert w_pb.shape[1] == width and w_pc.shape[1] == width
    tm = _tile(n, 1024)
    tn = _tile(d, 512, align=V7X_LANES)
    nj = d // tn
    act = pl.BlockSpec((tm, width), lambda j, i: (i, 0))
    gate = lambda s: pl.BlockSpec((tm, tn), lambda j, i, s=s: (i, s * nj + j))
    wt = pl.BlockSpec((None, width, tn), lambda j, i: (layer, 0, j))
    return pl.pallas_call(
        _merge_kernel,
        grid=(nj, n // tm),
        in_specs=[act, act, act, gate(0), gate(1), gate(2), wt, wt, wt],
        out_specs=pl.BlockSpec((tm, tn), lambda j, i: (i, j)),
        out_shape=jax.ShapeDtypeStruct((n, d), bf16),
        scratch_shapes=[pltpu.VMEM((3, width, tn), bf16)],
        compiler_params=_params("arbitrary", "arbitrary"),
        name="gated_merge",
    )(ya, yb, yc, p_gate, p_gate, p_gate, w_pa, w_pb, w_pc)


def _oproj_kernel(m_ref, wo_ref, x_ref, g1_ref, g2_ref, x1_ref, h2_ref):
    mix = _dot(m_ref[...], wo_ref[...])
    x1 = x_ref[...] + _rms(mix, g1_ref[...])
    x1_ref[...] = x1
    h2_ref[...] = _rms(x1, g2_ref[...]).astype(h2_ref.dtype)


def _oproj(m, w_o, layer, x, g_post_mix, g_pre_mlp):
    n, d = x.shape
    tm = _tile(n, 512)
    blk = pl.BlockSpec((tm, d), lambda i: (i, 0))
    vec = pl.BlockSpec((1, d), lambda i: (0, 0))
    return pl.pallas_call(
        _oproj_kernel,
        grid=(n // tm,),
        in_specs=[blk, pl.BlockSpec((None, d, d), lambda i: (layer, 0, 0)), blk, vec, vec],
        out_specs=[blk, blk],
        out_shape=[jax.ShapeDtypeStruct((n, d), f32), jax.ShapeDtypeStruct((n, d), bf16)],
        compiler_params=_params("parallel"),
        name="out_proj_norms",
    )(m, w_o, x, g_post_mix, g_pre_mlp)


def _mlp_step(idx, h_ref, w1_ref, w2_ref, ff_ref):
    @pl.when(idx[1] == 0)
    def _():
        ff_ref[...] = jnp.zeros_like(ff_ref)

    t = jnp.maximum(_dot(h_ref[...], w1_ref[...].astype(bf16)), 0.0)
    ff_ref[...] += _dot((t * t).astype(bf16), w2_ref[...].astype(bf16))


def _mlp_kernel(h_hbm, w1_hbm, w2_hbm, ff_hbm, *, layer, tm, fc):
    n, d = h_hbm.shape
    dff = w1_hbm.shape[2]
    deep = pl.Buffered(3)
    pltpu.emit_pipeline(
        _mlp_step,
        grid=(n // tm, dff // fc),
        in_specs=[pl.BlockSpec((tm, d), lambda i, kf: (i, 0)),
                  pl.BlockSpec((d, fc), lambda i, kf: (0, kf), pipeline_mode=deep),
                  pl.BlockSpec((fc, d), lambda i, kf: (kf, 0), pipeline_mode=deep)],
        out_specs=[pl.BlockSpec((tm, d), lambda i, kf: (i, 0))],
        _explicit_indices=True,
    )(h_hbm, w1_hbm.at[layer], w2_hbm.at[layer], ff_hbm)


def _mlp(h2, w1, w2, layer):
    n, d = h2.shape
    dff = w1.shape[2]
    tm = _tile(n, 1024)
    fc = _tile(dff, 512, align=V7X_LANES)
    hbm = pl.BlockSpec(memory_space=pl.ANY)
    return pl.pallas_call(
        functools.partial(_mlp_kernel, layer=layer, tm=tm, fc=fc),
        in_specs=[hbm, hbm, hbm],
        out_specs=hbm,
        out_shape=jax.ShapeDtypeStruct((n, d), f32),
        compiler_params=pltpu.CompilerParams(vmem_limit_bytes=V7X_VMEM_BIG_TILE_BYTES),
        name="relu2_mlp",
    )(h2, w1, w2)


def _residual_kernel(*refs, has_next):
    if has_next:
        x1_ref, ff_ref, g_ref, gn_ref, y_ref, h_ref = refs
    else:
        x1_ref, ff_ref, g_ref, y_ref = refs
    y = x1_ref[...] + _rms(ff_ref[...], g_ref[...])
    y_ref[...] = y
    if has_next:
        h_ref[...] = _rms(y, gn_ref[...]).astype(h_ref.dtype)


def _residual(x1, ff, g_post_mlp, g_next):
    n, d = x1.shape
    tm = _tile(n, 512)
    blk = pl.BlockSpec((tm, d), lambda i: (i, 0))
    vec = pl.BlockSpec((1, d), lambda i: (0, 0))
    has_next = g_next is not None
    outs = pl.pallas_call(
        functools.partial(_residual_kernel, has_next=has_next),
        grid=(n // tm,),
        in_specs=[blk, blk, vec] + ([vec] if has_next else []),
        out_specs=[blk] + ([blk] if has_next else []),
        out_shape=[jax.ShapeDtypeStruct((n, d), f32)] + ([jax.ShapeDtypeStruct((n, d), bf16)] if has_next else []),
        compiler_params=_params("parallel"),
        name="mlp_residual_norm",
    )(*([x1, ff, g_post_mlp] + ([g_next] if has_next else [])))
    return (outs[0], outs[1]) if has_next else (outs[0], None)


def _layer(x, h, bsz, tlen, states_in, states_prev, layer, depth, wts, lw, g_next):
    hg_in, rw_in, shift_in, sc_in = states_in
    hg_prev, rw_prev, shift_prev, sc_prev = states_prev
    w_in, offs = wts["w_in"], wts["offs"]
    p_hg = _proj(h, w_in, layer, offs[0], offs[1] - offs[0], "in_proj_hgrn")
    p_rkv = _proj(h, w_in, layer, offs[1], offs[2] - offs[1], "in_proj_rwkv")
    p_lora = _proj(h, w_in, layer, offs[2], offs[3] - offs[2], "in_proj_rwkv_lora")
    p_sc = _proj(h, w_in, layer, offs[3], offs[4] - offs[3], "in_proj_conv", bf16)
    p_gate = _proj(h, w_in, layer, offs[4], offs[5] - offs[4], "in_proj_gate", bf16)

    ya, hg_new = _hgrn(p_hg, lw["lb"], lw["hg_norm"], hg_in, hg_prev, layer, depth, bsz, tlen)

    seq_dtype = bf16 if _rw_mode(bsz, tlen) == "chunk" else f32
    seqs, shift_new = _rw_prep(p_rkv, p_lora, shift_in, shift_prev, layer, depth, lw["rw_prep"], bsz, tlen,
                               seq_dtype)
    r, w, k, v, kn, bb, g = seqs
    o_rw, rw_new = _rw_scan((r, w, k, v, kn, bb), rw_in, rw_prev, layer, depth, bsz, tlen)
    flat = lambda t: t.reshape(bsz * tlen, t.shape[-1])
    yb = _rw_post(flat(o_rw), flat(r), flat(k), flat(v), flat(g), lw["rw_ln_w"], lw["rw_ln_b"], lw["rw_r_k"])

    yc, sc_new = _conv(p_sc, sc_in, sc_prev, layer, depth, lw["sc_conv_w"], bsz, tlen)

    m = _merge(ya, yb, yc, p_gate, wts["w_pa"], wts["w_pb"], wts["w_pc"], layer)
    x1, h2 = _oproj(m, wts["w_o"], layer, x, lw["g_post_mix"], lw["g_pre_mlp"])
    ff = _mlp(h2, wts["w_ff1"], wts["w_ff2"], layer)
    y, h_next = _residual(x1, ff, lw["g_post_mlp"], g_next)
    return y, h_next, (hg_new, rw_new, shift_new, sc_new)


def kernel(x_prompt, x_sample, state_hgrn, state_rwkv, state_rwkv_shift, state_conv, norm_pre_mix, norm_post_mix, norm_pre_mlp, norm_post_mlp, w_in, hg_lb_logits, hg_norm, w_pa, rw_mu, rw_w0, rw_w2, rw_a0, rw_a2, rw_g2, rw_k_k, rw_k_a, rw_r_k, rw_ln_w, rw_ln_b, w_pb, sc_conv_w, w_pc, w_o, w_ff1, w_ff2):
    depth = w_in.shape[0]
    d_model = x_prompt.shape[-1]
    hg_width = hg_norm.shape[1]
    rw_width = rw_w0.shape[1]
    rw_shift_width = rw_mu.shape[1]
    sc_width = sc_conv_w.shape[2]
    off_rw = 4 * hg_width
    off_lora = off_rw + 3 * rw_width
    off_sc = off_rw + rw_shift_width
    off_gate = off_sc + 3 * sc_width
    assert w_in.shape[2] == off_gate + 3 * d_model

    lb_all = _lower_bounds(hg_lb_logits.astype(f32))
    row = lambda a, l: a[l].reshape(1, -1).astype(f32)
    wts = {"w_in": w_in.astype(f32), "offs": (0, off_rw, off_lora, off_sc, off_gate, w_in.shape[2]),
           "w_pa": w_pa.astype(f32), "w_pb": w_pb.astype(f32), "w_pc": w_pc.astype(f32),
           "w_o": w_o.astype(bf16), "w_ff1": w_ff1.astype(f32), "w_ff2": w_ff2.astype(f32)}

    bp, tp, _ = x_prompt.shape
    bs, ts, _ = x_sample.shape
    yp = x_prompt.reshape(bp * tp, d_model).astype(f32)
    ys = x_sample.reshape(bs * ts, d_model).astype(f32)
    hp = _rmsnorm_cast(yp, row(norm_pre_mix, 0))
    hs = _rmsnorm_cast(ys, row(norm_pre_mix, 0))
    rw_lanes = _rw_mode(bs, ts) == "lanes"
    rw_state = state_rwkv.astype(f32)
    sample_in = (state_hgrn.astype(f32), jnp.transpose(rw_state, (0, 2, 3, 4, 1)) if rw_lanes else rw_state,
                 state_rwkv_shift.astype(f32).reshape(depth, bs, 1, rw_shift_width), state_conv.astype(f32))
    new_p = new_s = (None, None, None, None)
    for l in range(depth):
        lw = {
            "g_post_mix": row(norm_post_mix, l), "g_pre_mlp": row(norm_pre_mlp, l),
            "g_post_mlp": row(norm_post_mlp, l),
            "lb": lb_all[l:l + 1], "hg_norm": row(hg_norm, l),
            "rw_prep": (row(rw_mu, l), row(rw_w0, l), rw_w2[l].astype(bf16), row(rw_a0, l), rw_a2[l].astype(bf16),
                        rw_g2[l].astype(bf16), row(rw_k_k, l), row(rw_k_a, l)),
            "rw_ln_w": row(rw_ln_w, l), "rw_ln_b": row(rw_ln_b, l), "rw_r_k": row(rw_r_k, l),
            "sc_conv_w": sc_conv_w[l].astype(f32),
        }
        g_next = row(norm_pre_mix, l + 1) if l + 1 < depth else None
        yp, hp, new_p = _layer(yp, hp, bp, tp, (None, None, None, None), new_p, l, depth, wts, lw, g_next)
        ys, hs, new_s = _layer(ys, hs, bs, ts, sample_in, new_s, l, depth, wts, lw, g_next)
    shift = lambda s, b: s.reshape(depth, b, rw_shift_width)
    return (yp.reshape(bp, tp, d_model), ys.reshape(bs, ts, d_model),
            new_p[0], new_p[1], shift(new_p[2], bp), new_p[3],
            new_s[0], jnp.transpose(new_s[1], (0, 4, 1, 2, 3)) if rw_lanes else new_s[1],
            shift(new_s[2], bs), new_s[3])
```
